```python
import math
import jax, jax.numpy as jnp
from jax import lax
import numpy as np

D_MODEL = 2048
BATCH = 8
SEQ = 8192
DEPTH = 4

CHUNK = 64
D_MIX = 2 * D_MODEL
SSD_WIDTH = D_MIX // 2
SBA_WIDTH = D_MIX - SSD_WIDTH
SSD_HEAD_DIM = 64
SSD_HEADS = SSD_WIDTH // SSD_HEAD_DIM
SSD_GROUPS = 4
SSD_STATE = 128
SSD_CONV = 4
SBA_HEAD_DIM = 128
SBA_HEADS = SBA_WIDTH // SBA_HEAD_DIM
Q_BLOCK = 128
EPS = 1e-6
CONV_DIM = SSD_WIDTH + 2 * SSD_GROUPS * SSD_STATE
IN_SPLITS = (
    SSD_WIDTH,
    SSD_WIDTH + CONV_DIM,
    SSD_WIDTH + CONV_DIM + SSD_HEADS,
    SSD_WIDTH + CONV_DIM + SSD_HEADS + SBA_WIDTH,
    SSD_WIDTH + CONV_DIM + SSD_HEADS + 2 * SBA_WIDTH,
    SSD_WIDTH + CONV_DIM + SSD_HEADS + 3 * SBA_WIDTH,
)
IN_COLS = SSD_WIDTH + CONV_DIM + SSD_HEADS + 4 * SBA_WIDTH

kernel_name = "hybrid_ssd_stickbreaking_parallel_heads"


def rmsnorm(x, w):
    xf = x.astype(jnp.float32)
    y = xf * lax.rsqrt(jnp.mean(xf * xf, axis=-1, keepdims=True) + EPS)
    return (y * w.astype(jnp.float32)).astype(x.dtype)


def causal_depthwise_conv(u, w, b):
    k_taps = w.shape[0]
    seq = u.shape[1]
    up = jnp.pad(u, ((0, 0), (k_taps - 1, 0), (0, 0)))
    out = b
    for j in range(k_taps):
        out = out + w[j] * up[:, j:j + seq]
    return out


def ssd_chunked_scan(x, dt, a_head, bm, cm):
    bsz, seq, n_heads, hd = x.shape
    g, n = bm.shape[2], bm.shape[3]
    r = n_heads // g
    nc = seq // CHUNK
    xs = (x.astype(jnp.float32) * dt[..., None]).reshape(bsz, nc, CHUNK, g, r, hd)
    da = (dt * a_head).reshape(bsz, nc, CHUNK, g, r).transpose(0, 3, 4, 1, 2)
    bc = bm.astype(jnp.float32).reshape(bsz, nc, CHUNK, g, n)
    cc = cm.astype(jnp.float32).reshape(bsz, nc, CHUNK, g, n)
    a_cum = jnp.cumsum(da, axis=-1)
    tri = jnp.tril(jnp.ones((CHUNK, CHUNK), dtype=bool))
    seg = a_cum[..., :, None] - a_cum[..., None, :]
    decay_in = jnp.exp(jnp.where(tri, seg, -jnp.inf))
    cb = jnp.einsum('bclgn,bcsgn->bgcls', cc, bc)
    y_diag = jnp.einsum('bgcls,bgrcls,bcsgrp->bclgrp', cb, decay_in, xs)
    decay_states = jnp.exp(a_cum[..., -1:] - a_cum)
    states = jnp.einsum('bclgn,bgrcl,bclgrp->cbgrpn', bc, decay_states, xs)
    chunk_decay = jnp.exp(a_cum[..., -1]).transpose(3, 0, 1, 2)

    def step(h, inp):
        s_c, d_c = inp
        return h * d_c[..., None, None] + s_c, h

    _, prev = lax.scan(step, jnp.zeros(states.shape[1:], jnp.float32), (states, chunk_decay))
    y_off = jnp.einsum('bclgn,cbgrpn,bgrcl->bclgrp', cc, prev, jnp.exp(a_cum))
    return (y_diag + y_off).reshape(bsz, seq, n_heads * hd)


def stick_breaking_attention(q, k, v):
    bsz, n_heads, seq, hd = q.shape
    n_blocks = seq // Q_BLOCK
    scale = 1.0 / math.sqrt(hd)
    kf = k.astype(jnp.float32)
    vf = v.astype(jnp.float32)
    key_pos = jnp.arange(seq)

    def block(i):
        start = i * Q_BLOCK
        qb = lax.dynamic_slice_in_dim(q, start, Q_BLOCK, axis=2).astype(jnp.float32)
        z = jnp.einsum('bhqd,bhkd->bhqk', qb, kf) * scale
        qpos = start + jnp.arange(Q_BLOCK)
        earlier = key_pos[None, :] < qpos[:, None]
        log_keep = jnp.where(earlier, jax.nn.log_sigmoid(-z), 0.0)
        later_sum = lax.cumsum(log_keep, axis=3, reverse=True) - log_keep
        weight = jnp.where(earlier, jnp.exp(jax.nn.log_sigmoid(z) + later_sum), 0.0)
        return jnp.einsum('bhqk,bhkd->bhqd', weight, vf)

    out = lax.map(block, jnp.arange(n_blocks))
    return out.transpose(1, 0, 3, 2, 4).reshape(bsz, seq, n_heads * hd)


def hybrid_layer(x, norm_w, w_in, conv_w, conv_b, dt_bias, a_log, d_skip, ssd_norm_w, w_out):
    bsz, seq, _ = x.shape
    h = rmsnorm(x, norm_w)
    proj = h @ w_in
    z, xbc, dt_raw, q, k, v, g = jnp.split(proj, IN_SPLITS, axis=-1)

    xbc = jax.nn.silu(causal_depthwise_conv(xbc, conv_w, conv_b))
    gn = SSD_GROUPS * SSD_STATE
    xs, bm, cm = jnp.split(xbc, (SSD_WIDTH, SSD_WIDTH + gn), axis=-1)
    xs = xs.reshape(bsz, seq, SSD_HEADS, SSD_HEAD_DIM)
    bm = bm.reshape(bsz, seq, SSD_GROUPS, SSD_STATE)
    cm = cm.reshape(bsz, seq, SSD_GROUPS, SSD_STATE)
    dt = jax.nn.softplus(dt_raw.astype(jnp.float32) + dt_bias.astype(jnp.float32))
    a_head = -jnp.exp(a_log.astype(jnp.float32))
    y = ssd_chunked_scan(xs, dt, a_head, bm, cm)
    y = y + (d_skip.astype(jnp.float32)[:, None] * xs.astype(jnp.float32)).reshape(bsz, seq, SSD_WIDTH)
    yg = (y * jax.nn.silu(z.astype(jnp.float32))).reshape(bsz, seq, SSD_GROUPS, SSD_WIDTH // SSD_GROUPS)
    yg = yg * lax.rsqrt(jnp.mean(yg * yg, axis=-1, keepdims=True) + EPS)
    y_ssd = yg.reshape(bsz, seq, SSD_WIDTH) * ssd_norm_w.astype(jnp.float32)

    def heads(t):
        return t.reshape(bsz, seq, SBA_HEADS, SBA_HEAD_DIM).transpose(0, 2, 1, 3)
    o = stick_breaking_attention(heads(q), heads(k), heads(v))
    y_sba = o * jax.nn.silu(g.astype(jnp.float32))

    mix = jnp.concatenate([y_ssd, y_sba], axis=-1).astype(x.dtype)
    return x + mix @ w_out


def _fwd_setup_inputs(seed: int = 0) -> dict:
    key = jax.random.key(seed)
    ks = jax.random.split(key, 12)
    f32 = jnp.float32
    x = jax.random.normal(ks[0], (BATCH, SEQ, D_MODEL), f32)
    norm_w = 1.0 + 0.02 * jax.random.normal(ks[1], (DEPTH, D_MODEL), f32)
    w_in = jax.random.normal(ks[2], (DEPTH, D_MODEL, IN_COLS), f32) * D_MODEL ** -0.5
    conv_w = jax.random.normal(ks[3], (DEPTH, SSD_CONV, CONV_DIM), f32) * SSD_CONV ** -0.5
    conv_b = 0.02 * jax.random.normal(ks[4], (DEPTH, CONV_DIM), f32)
    dt0 = jnp.exp(jax.random.uniform(ks[5], (DEPTH, SSD_HEADS), f32,
                                     math.log(1e-3), math.log(1e-1)))
    dt_bias = dt0 + jnp.log(-jnp.expm1(-dt0))
    a_log = jnp.log(jax.random.uniform(ks[6], (DEPTH, SSD_HEADS), f32, 1.0, 16.0))
    d_skip = 1.0 + 0.02 * jax.random.normal(ks[7], (DEPTH, SSD_HEADS), f32)
    ssd_norm_w = 1.0 + 0.02 * jax.random.normal(ks[8], (DEPTH, SSD_WIDTH), f32)
    w_out = jax.random.normal(ks[9], (DEPTH, D_MIX, D_MODEL), f32) * D_MIX ** -0.5
    final_norm_w = 1.0 + 0.02 * jax.random.normal(ks[10], (D_MODEL,), f32)
    return {"x": x, "norm_w": norm_w, "w_in": w_in, "conv_w": conv_w, "conv_b": conv_b,
            "dt_bias": dt_bias, "a_log": a_log, "d_skip": d_skip, "ssd_norm_w": ssd_norm_w,
            "w_out": w_out, "final_norm_w": final_norm_w}


def _fwd_reference(x, norm_w, w_in, conv_w, conv_b, dt_bias, a_log, d_skip, ssd_norm_w, w_out, final_norm_w):
    h = x
    for i in range(DEPTH):
        h = hybrid_layer(h, norm_w[i], w_in[i], conv_w[i], conv_b[i], dt_bias[i], a_log[i],
                         d_skip[i], ssd_norm_w[i], w_out[i])
    return rmsnorm(h, final_norm_w)


import jax as _jax
import jax.numpy as _jnp

TWIN_FORMAT = 'train_step'
FWD_PARAMS = ['x', 'norm_w', 'w_in', 'conv_w', 'conv_b', 'dt_bias', 'a_log', 'd_skip', 'ssd_norm_w', 'w_out', 'final_norm_w']
TWIN_WEIGHTS = ['norm_w', 'w_in', 'conv_w', 'conv_b', 'dt_bias', 'a_log', 'd_skip', 'ssd_norm_w', 'w_out', 'final_norm_w']
TWIN_DIFF_INPUT = 'x'
TWIN_INPUTS = ['x', 'norm_w', 'w_in', 'conv_w', 'conv_b', 'dt_bias', 'a_log', 'd_skip', 'ssd_norm_w', 'w_out', 'final_norm_w', 'loss_target', 'm_norm_w', 'm_w_in', 'm_conv_w', 'm_conv_b', 'm_dt_bias', 'm_a_log', 'm_d_skip', 'm_ssd_norm_w', 'm_w_out', 'm_final_norm_w', 'v_norm_w', 'v_w_in', 'v_conv_w', 'v_conv_b', 'v_dt_bias', 'v_a_log', 'v_d_skip', 'v_ssd_norm_w', 'v_w_out', 'v_final_norm_w']
TWIN_OUTPUTS = ['loss', 'grad_x', 'grad_norm_w', 'grad_w_in', 'grad_conv_w', 'grad_conv_b', 'grad_dt_bias', 'grad_a_log', 'grad_d_skip', 'grad_ssd_norm_w', 'grad_w_out', 'grad_final_norm_w', 'delta_norm_w', 'delta_w_in', 'delta_conv_w', 'delta_conv_b', 'delta_dt_bias', 'delta_a_log', 'delta_d_skip', 'delta_ssd_norm_w', 'delta_w_out', 'delta_final_norm_w', 'new_m_norm_w', 'new_m_w_in', 'new_m_conv_w', 'new_m_conv_b', 'new_m_dt_bias', 'new_m_a_log', 'new_m_d_skip', 'new_m_ssd_norm_w', 'new_m_w_out', 'new_m_final_norm_w', 'new_v_norm_w', 'new_v_w_in', 'new_v_conv_w', 'new_v_conv_b', 'new_v_dt_bias', 'new_v_a_log', 'new_v_d_skip', 'new_v_ssd_norm_w', 'new_v_w_out', 'new_v_final_norm_w']
TWIN_LEAF_KINDS = {'loss': 'loss', 'grad_x': 'grad_x', 'grad_norm_w': 'grad_w', 'grad_w_in': 'grad_w', 'grad_conv_w': 'grad_w', 'grad_conv_b': 'grad_w', 'grad_dt_bias': 'grad_w', 'grad_a_log': 'grad_w', 'grad_d_skip': 'grad_w', 'grad_ssd_norm_w': 'grad_w', 'grad_w_out': 'grad_w', 'grad_final_norm_w': 'grad_w', 'delta_norm_w': 'delta_w', 'delta_w_in': 'delta_w', 'delta_conv_w': 'delta_w', 'delta_conv_b': 'delta_w', 'delta_dt_bias': 'delta_w', 'delta_a_log': 'delta_w', 'delta_d_skip': 'delta_w', 'delta_ssd_norm_w': 'delta_w', 'delta_w_out': 'delta_w', 'delta_final_norm_w': 'delta_w', 'new_m_norm_w': 'new_m', 'new_m_w_in': 'new_m', 'new_m_conv_w': 'new_m', 'new_m_conv_b': 'new_m', 'new_m_dt_bias': 'new_m', 'new_m_a_log': 'new_m', 'new_m_d_skip': 'new_m', 'new_m_ssd_norm_w': 'new_m', 'new_m_w_out': 'new_m', 'new_m_final_norm_w': 'new_m', 'new_v_norm_w': 'new_v', 'new_v_w_in': 'new_v', 'new_v_conv_w': 'new_v', 'new_v_conv_b': 'new_v', 'new_v_dt_bias': 'new_v', 'new_v_a_log': 'new_v', 'new_v_d_skip': 'new_v', 'new_v_ssd_norm_w': 'new_v', 'new_v_w_out': 'new_v', 'new_v_final_norm_w': 'new_v'}


def _forward(args):
    return _fwd_reference(*[args[k] for k in FWD_PARAMS])


def _output_shape():
    def fwd():
        inp = _fwd_setup_inputs(0)
        return _fwd_reference(*[inp[k] for k in FWD_PARAMS])
    out = _jax.eval_shape(fwd)
    return out.shape, out.dtype

N_MICROBATCH = 1
ADAM_LR = 0.001
ADAM_B1 = 0.9
ADAM_B2 = 0.999
ADAM_EPS = 1e-08
ADAM_WD = 0.01
ADAM_STEP = 10
PER_EXAMPLE_BATCH_AXIS = {'x': 0, 'loss_target': 0}
SHARED_INPUTS = []
_WEIGHT_DTYPES = {'norm_w': _jnp.float32, 'w_in': _jnp.float32, 'conv_w': _jnp.float32, 'conv_b': _jnp.float32, 'dt_bias': _jnp.float32, 'a_log': _jnp.float32, 'd_skip': _jnp.float32, 'ssd_norm_w': _jnp.float32, 'w_out': _jnp.float32, 'final_norm_w': _jnp.float32}
MOMENT_SCALE = {'norm_w': 1.217941e-01, 'w_in': 4.738122e-02, 'conv_w': 6.494020e-02, 'conv_b': 8.858756e-02, 'dt_bias': 1.449403e-01, 'a_log': 2.128553e-01, 'd_skip': 3.794916e-01, 'ssd_norm_w': 7.617850e-02, 'w_out': 7.972036e-02, 'final_norm_w': 3.203760e+01}


def _to_microbatches(a, axis):
    t = _jnp.moveaxis(a, axis, 0)
    t = t.reshape((N_MICROBATCH, t.shape[0] // N_MICROBATCH) + t.shape[1:])
    return _jnp.moveaxis(t, 1, axis + 1)


def setup_inputs(seed: int = 0) -> dict:
    inp = _fwd_setup_inputs(seed)
    key = _jax.random.fold_in(_jax.random.key(seed), 7919)
    shape, _ = _output_shape()
    out = dict(inp)
    out["loss_target"] = _jax.random.normal(_jax.random.fold_in(key, 0), shape, _jnp.float32)
    for i, name in enumerate(TWIN_WEIGHTS):
        w = inp[name].astype(_jnp.float32)
        if MOMENT_SCALE is None:
            s = _jnp.sqrt(_jnp.mean(_jnp.square(w)) + 1e-30)
        else:
            s = MOMENT_SCALE[name]
        km, kv = _jax.random.split(_jax.random.fold_in(key, i + 1))
        out[name] = w
        out["m_" + name] = s * _jax.random.normal(km, w.shape, _jnp.float32)
        out["v_" + name] = (s * s) * _jax.random.uniform(kv, w.shape, _jnp.float32, 0.5, 1.5)
    if N_MICROBATCH > 1:
        for name, axis in PER_EXAMPLE_BATCH_AXIS.items():
            out[name] = _to_microbatches(out[name], axis)
    return {'x': out['x'], 'norm_w': out['norm_w'], 'w_in': out['w_in'], 'conv_w': out['conv_w'], 'conv_b': out['conv_b'], 'dt_bias': out['dt_bias'], 'a_log': out['a_log'], 'd_skip': out['d_skip'], 'ssd_norm_w': out['ssd_norm_w'], 'w_out': out['w_out'], 'final_norm_w': out['final_norm_w'], 'loss_target': out['loss_target'], 'm_norm_w': out['m_norm_w'], 'm_w_in': out['m_w_in'], 'm_conv_w': out['m_conv_w'], 'm_conv_b': out['m_conv_b'], 'm_dt_bias': out['m_dt_bias'], 'm_a_log': out['m_a_log'], 'm_d_skip': out['m_d_skip'], 'm_ssd_norm_w': out['m_ssd_norm_w'], 'm_w_out': out['m_w_out'], 'm_final_norm_w': out['m_final_norm_w'], 'v_norm_w': out['v_norm_w'], 'v_w_in': out['v_w_in'], 'v_conv_w': out['v_conv_w'], 'v_conv_b': out['v_conv_b'], 'v_dt_bias': out['v_dt_bias'], 'v_a_log': out['v_a_log'], 'v_d_skip': out['v_d_skip'], 'v_ssd_norm_w': out['v_ssd_norm_w'], 'v_w_out': out['v_w_out'], 'v_final_norm_w': out['v_final_norm_w']}


def _loss(weights, diff, rest, loss_target):
    with _jax.named_scope("forward"):
        args = {**rest, TWIN_DIFF_INPUT: diff, **{k: w.astype(_WEIGHT_DTYPES[k]) for k, w in weights.items()}}
        y = _forward(args)
    with _jax.named_scope("loss_head"):
        err = _jnp.square(y.astype(_jnp.float32) - loss_target)
        return 0.5 * _jnp.sum(_jnp.mean(err, axis=-1)) if err.ndim else 0.5 * err


def _adamw(w, g, m, v):
    m = ADAM_B1 * m + (1.0 - ADAM_B1) * g
    v = ADAM_B2 * v + (1.0 - ADAM_B2) * _jnp.square(g)
    m_hat = m / (1.0 - ADAM_B1 ** ADAM_STEP)
    v_hat = v / (1.0 - ADAM_B2 ** ADAM_STEP)
    delta = -ADAM_LR * (m_hat / (_jnp.sqrt(v_hat) + ADAM_EPS) + ADAM_WD * w)
    return delta, m, v


def reference(x, norm_w, w_in, conv_w, conv_b, dt_bias, a_log, d_skip, ssd_norm_w, w_out, final_norm_w, loss_target, m_norm_w, m_w_in, m_conv_w, m_conv_b, m_dt_bias, m_a_log, m_d_skip, m_ssd_norm_w, m_w_out, m_final_norm_w, v_norm_w, v_w_in, v_conv_w, v_conv_b, v_dt_bias, v_a_log, v_d_skip, v_ssd_norm_w, v_w_out, v_final_norm_w):
    given = dict(x=x, norm_w=norm_w, w_in=w_in, conv_w=conv_w, conv_b=conv_b, dt_bias=dt_bias, a_log=a_log, d_skip=d_skip, ssd_norm_w=ssd_norm_w, w_out=w_out, final_norm_w=final_norm_w, loss_target=loss_target, m_norm_w=m_norm_w, m_w_in=m_w_in, m_conv_w=m_conv_w, m_conv_b=m_conv_b, m_dt_bias=m_dt_bias, m_a_log=m_a_log, m_d_skip=m_d_skip, m_ssd_norm_w=m_ssd_norm_w, m_w_out=m_w_out, m_final_norm_w=m_final_norm_w, v_norm_w=v_norm_w, v_w_in=v_w_in, v_conv_w=v_conv_w, v_conv_b=v_conv_b, v_dt_bias=v_dt_bias, v_a_log=v_a_log, v_d_skip=v_d_skip, v_ssd_norm_w=v_ssd_norm_w, v_w_out=v_w_out, v_final_norm_w=v_final_norm_w)
    weights = {n: given[n] for n in TWIN_WEIGHTS}
    shared = {n: given[n] for n in SHARED_INPUTS}
    per_example = {n: given[n] for n in ['x']}
    grad_fn = _jax.value_and_grad(_loss, argnums=(0, 1))

    def one_microbatch(ex, loss_target):
        ex = dict(ex)
        diff = ex.pop(TWIN_DIFF_INPUT)
        return grad_fn(weights, diff, {**shared, **ex}, loss_target)

    if N_MICROBATCH == 1:
        loss, (grad_w, grad_x) = one_microbatch(per_example, given["loss_target"])
    else:
        def body(carry, xs):
            loss_sum, grad_sum = carry
            l_k, (gw_k, gx_k) = one_microbatch(xs[0], xs[1])
            with _jax.named_scope("update"):
                return (loss_sum + l_k, _jax.tree.map(_jnp.add, grad_sum, gw_k)), gx_k

        init = (_jnp.zeros((), _jnp.float32), _jax.tree.map(_jnp.zeros_like, weights))
        (loss, grad_w), grad_x = _jax.lax.scan(body, init, (per_example, given["loss_target"]))
    with _jax.named_scope("update"):
        delta_w, new_m, new_v = {}, {}, {}
        for n in TWIN_WEIGHTS:
            delta_w[n], new_m[n], new_v[n] = _adamw(weights[n], grad_w[n], given["m_" + n], given["v_" + n])
    return (loss, grad_x, *[grad_w[n] for n in TWIN_WEIGHTS], *[delta_w[n] for n in TWIN_WEIGHTS],
            *[new_m[n] for n in TWIN_WEIGHTS], *[new_v[n] for n in TWIN_WEIGHTS])
```

```python
import functools
import math

import jax
import jax.numpy as jnp
from jax import lax
from jax.experimental import pallas as pl
from jax.experimental.pallas import tpu as pltpu

F32, BF16 = jnp.float32, jnp.bfloat16
EPS = 1e-6
CHUNK = 64
NGROUPS = 4
NSTATE = 128
KCONV = 4
SSD_HD = 64
SBA_HD = 128
LANES = 128
VMEM_LIMIT = 56 * 1024 * 1024

ADAM_LR, ADAM_B1, ADAM_B2, ADAM_EPS, ADAM_WD, ADAM_STEP = 0.001, 0.9, 0.999, 1e-08, 0.01, 10

NN = ((1,), (0,))
NT = ((1,), (1,))
TN = ((0,), (0,))
MESH = pl.DeviceIdType.MESH


def _dot(a, b, dims):
    return lax.dot_general(a, b, (dims, ((), ())), preferred_element_type=F32)


def _params(*sem):
    return pltpu.CompilerParams(dimension_semantics=sem, vmem_limit_bytes=VMEM_LIMIT)


def _pick(n, target, mult):
    best = None
    for d in range(mult, min(n, target) + 1, mult):
        if n % d == 0:
            best = d
    return n if best is None else best


def _split3(x):
    x1 = x.astype(BF16)
    r1 = x - x1.astype(F32)
    x2 = r1.astype(BF16)
    x3 = (r1 - x2.astype(F32)).astype(BF16)
    return x1, x2, x3


def _split2(x):
    x1 = x.astype(BF16)
    return x1, (x - x1.astype(F32)).astype(BF16)


def _sigmoid(x):
    return 1.0 / (1.0 + jnp.exp(-x))


def _softplus(x):
    return jnp.maximum(x, 0.0) + jnp.log1p(jnp.exp(-jnp.abs(x)))


def _matmul(a, b, mode, out_dtype, name, add=None, tm=1024, tn=512, tk=2048):
    if mode == "nn":
        (m, k), n = a.shape, b.shape[1]
    elif mode == "nt":
        (m, k), n = a.shape, b.shape[0]
    else:
        (k, m), n = a.shape, b.shape[1]
    tm, tn, tk = _pick(m, tm, LANES), _pick(n, tn, LANES), _pick(k, tk, LANES)
    nk = k // tk
    dims = {"nn": NN, "nt": NT, "tn": TN}[mode]
    a_spec = pl.BlockSpec((tk, tm), lambda i, j, kk: (kk, i)) if mode == "tn" else pl.BlockSpec((tm, tk), lambda i, j, kk: (i, kk))
    b_spec = pl.BlockSpec((tn, tk), lambda i, j, kk: (j, kk)) if mode == "nt" else pl.BlockSpec((tk, tn), lambda i, j, kk: (kk, j))
    o_spec = pl.BlockSpec((tm, tn), lambda i, j, kk: (i, j))
    has_add = add is not None

    def body(*refs):
        a_ref, b_ref = refs[0], refs[1]
        add_ref = refs[2] if has_add else None
        o_ref, acc_ref = refs[-2], refs[-1]
        kk = pl.program_id(2)
        part = _dot(a_ref[...], b_ref[...], dims)

        def finish(total):
            if has_add:
                total = total + add_ref[...].astype(F32)
            o_ref[...] = total.astype(out_dtype)

        if nk == 1:
            finish(part)
        else:
            @pl.when(kk == 0)
            def _():
                acc_ref[...] = part

            @pl.when(jnp.logical_and(kk > 0, kk < nk - 1))
            def _():
                acc_ref[...] += part

            @pl.when(kk == nk - 1)
            def _():
                finish(acc_ref[...] + part)

    in_specs = [a_spec, b_spec] + ([o_spec] if has_add else [])
    args = (a, b) + ((add,) if has_add else ())
    return pl.pallas_call(
        body, name=name, grid=(m // tm, n // tn, nk), in_specs=in_specs, out_specs=o_spec,
        out_shape=jax.ShapeDtypeStruct((m, n), out_dtype),
        scratch_shapes=[pltpu.VMEM((tm, tn), F32)],
        compiler_params=_params("parallel", "parallel", "arbitrary"),
    )(*args)


def _rmsnorm_fwd(x, w, name):
    l, d = x.shape
    tr = _pick(l, 512, 8)

    def body(x_ref, w_ref, h_ref):
        xv = x_ref[...]
        r = lax.rsqrt(jnp.mean(xv * xv, axis=-1, keepdims=True) + EPS)
        h_ref[...] = (xv * r * w_ref[...]).astype(BF16)

    return pl.pallas_call(
        body, name=name, grid=(l // tr,),
        in_specs=[pl.BlockSpec((tr, d), lambda i: (i, 0)), pl.BlockSpec((1, d), lambda i: (0, 0))],
        out_specs=pl.BlockSpec((tr, d), lambda i: (i, 0)),
        out_shape=jax.ShapeDtypeStruct((l, d), BF16), compiler_params=_params("parallel"),
    )(x, w)


def _rmsnorm_bwd(dh, x, w, dres, name):
    l, d = x.shape
    tr = _pick(l, 256, 8)

    def body(dh_ref, x_ref, w_ref, dres_ref, dx_ref, dw_ref):
        xv = x_ref[...]
        r = lax.rsqrt(jnp.mean(xv * xv, axis=-1, keepdims=True) + EPS)
        xh = xv * r
        dhv = dh_ref[...]
        dxh = dhv * w_ref[...]
        dx_ref[...] = dres_ref[...] + r * (dxh - xh * jnp.mean(dxh * xh, axis=-1, keepdims=True))
        part = jnp.sum(dhv * xh, axis=0, keepdims=True)

        @pl.when(pl.program_id(0) == 0)
        def _():
            dw_ref[...] = part

        @pl.when(pl.program_id(0) > 0)
        def _():
            dw_ref[...] += part

    row = pl.BlockSpec((tr, d), lambda i: (i, 0))
    vec = pl.BlockSpec((1, d), lambda i: (0, 0))
    return pl.pallas_call(
        body, name=name, grid=(l // tr,), in_specs=[row, row, vec, row], out_specs=[row, vec],
        out_shape=[jax.ShapeDtypeStruct((l, d), F32), jax.ShapeDtypeStruct((1, d), F32)],
        compiler_params=_params("arbitrary"),
    )(dh, x, w, dres)


def _final_loss(h, w, target, name):
    l, d = h.shape
    tr = _pick(l, 256, 8)

    def body(h_ref, w_ref, t_ref, dh_ref, dw_ref, loss_ref):
        xv = h_ref[...]
        r = lax.rsqrt(jnp.mean(xv * xv, axis=-1, keepdims=True) + EPS)
        xh = xv * r
        err = xh * w_ref[...] - t_ref[...]
        dy = err * (1.0 / d)
        dxh = dy * w_ref[...]
        dh_ref[...] = r * (dxh - xh * jnp.mean(dxh * xh, axis=-1, keepdims=True))
        part = jnp.sum(dy * xh, axis=0, keepdims=True)
        lpart = jnp.zeros((8, LANES), F32) + 0.5 * jnp.sum(jnp.mean(err * err, axis=-1, keepdims=True))

        @pl.when(pl.program_id(0) == 0)
        def _():
            dw_ref[...] = part
            loss_ref[...] = lpart

        @pl.when(pl.program_id(0) > 0)
        def _():
            dw_ref[...] += part
            loss_ref[...] += lpart

    row = pl.BlockSpec((tr, d), lambda i: (i, 0))
    vec = pl.BlockSpec((1, d), lambda i: (0, 0))
    return pl.pallas_call(
        body, name=name, grid=(l // tr,), in_specs=[row, vec, row],
        out_specs=[row, vec, pl.BlockSpec((8, LANES), lambda i: (0, 0))],
        out_shape=[jax.ShapeDtypeStruct((l, d), F32), jax.ShapeDtypeStruct((1, d), F32), jax.ShapeDtypeStruct((8, LANES), F32)],
        compiler_params=_params("arbitrary"),
    )(h, w, target)


def _gatenorm_fwd(y, proj_a, w, name):
    l, d = y.shape
    dg = d // NGROUPS
    tr = _pick(l, 256, 8)

    def body(y_ref, z_ref, w_ref, o_ref):
        for g in range(NGROUPS):
            sl = slice(g * dg, (g + 1) * dg)
            zv = z_ref[:, sl]
            u = y_ref[:, sl] * (zv * _sigmoid(zv))
            r = lax.rsqrt(jnp.mean(u * u, axis=-1, keepdims=True) + EPS)
            o_ref[:, sl] = (u * r * w_ref[:, sl]).astype(BF16)

    row = pl.BlockSpec((tr, d), lambda i: (i, 0))
    return pl.pallas_call(
        body, name=name, grid=(l // tr,), in_specs=[row, row, pl.BlockSpec((1, d), lambda i: (0, 0))],
        out_specs=row, out_shape=jax.ShapeDtypeStruct((l, d), BF16), compiler_params=_params("parallel"),
    )(y, proj_a, w)


def _gatenorm_bwd(dout, y, proj_a, w, name):
    l, d = y.shape
    dg = d // NGROUPS
    tr = _pick(l, 256, 8)

    def body(do_ref, y_ref, z_ref, w_ref, dy_ref, dz_ref, dw_ref):
        first = pl.program_id(0) == 0
        for g in range(NGROUPS):
            sl = slice(g * dg, (g + 1) * dg)
            zv = z_ref[:, sl]
            sg = _sigmoid(zv)
            sz = zv * sg
            yv = y_ref[:, sl]
            u = yv * sz
            r = lax.rsqrt(jnp.mean(u * u, axis=-1, keepdims=True) + EPS)
            nh = u * r
            dov = do_ref[:, sl]
            dn = dov * w_ref[:, sl]
            du = r * (dn - nh * jnp.mean(dn * nh, axis=-1, keepdims=True))
            dy_ref[:, sl] = du * sz
            dz_ref[:, sl] = (du * yv * (sg * (1.0 + zv * (1.0 - sg)))).astype(BF16)
            part = jnp.sum(dov * nh, axis=0, keepdims=True)

            @pl.when(first)
            def _():
                dw_ref[:, sl] = part

            @pl.when(jnp.logical_not(first))
            def _():
                dw_ref[:, sl] += part

    row = pl.BlockSpec((tr, d), lambda i: (i, 0))
    vec = pl.BlockSpec((1, d), lambda i: (0, 0))
    return pl.pallas_call(
        body, name=name, grid=(l // tr,), in_specs=[row, row, row, vec], out_specs=[row, row, vec],
        out_shape=[jax.ShapeDtypeStruct((l, d), F32), jax.ShapeDtypeStruct((l, d), BF16), jax.ShapeDtypeStruct((1, d), F32)],
        compiler_params=_params("arbitrary"),
    )(dout, y, proj_a, w)


def _conv_cols(d):
    return _pick(math.gcd(d, d + 2 * NGROUPS * NSTATE), 512, LANES)


def _conv_fwd(proj_a, conv_w, conv_b, d, name):
    l = proj_a.shape[0]
    cdim = d + 2 * NGROUPS * NSTATE
    cw = _conv_cols(d)
    off = d // cw
    tl = _pick(l, 512, 8)
    hb = tl // 8

    def body(u_ref, up_ref, w_ref, b_ref, acc_ref, xbc_ref, ext):
        i = pl.program_id(1)
        ext[0:8, :] = jnp.where(i > 0, up_ref[...], 0.0)
        ext[8:8 + tl, :] = u_ref[...]
        acc = jnp.zeros((tl, cw), F32) + b_ref[...]
        for j in range(KCONV):
            acc = acc + w_ref[j:j + 1, :] * ext[pl.ds(8 - (KCONV - 1) + j, tl), :]
        acc_ref[...] = acc
        xbc_ref[...] = acc * _sigmoid(acc)

    blk = pl.BlockSpec((tl, cw), lambda c, i: (i, c))
    return pl.pallas_call(
        body, name=name, grid=(cdim // cw, l // tl),
        in_specs=[pl.BlockSpec((tl, cw), lambda c, i: (i, c + off)),
                  pl.BlockSpec((8, cw), lambda c, i: (jnp.maximum(i * hb - 1, 0), c + off)),
                  pl.BlockSpec((KCONV, cw), lambda c, i: (0, c)), pl.BlockSpec((1, cw), lambda c, i: (0, c))],
        out_specs=[blk, blk],
        out_shape=[jax.ShapeDtypeStruct((l, cdim), F32), jax.ShapeDtypeStruct((l, cdim), F32)],
        scratch_shapes=[pltpu.VMEM((tl + 8, cw), F32)], compiler_params=_params("parallel", "parallel"),
    )(proj_a, proj_a, conv_w, conv_b)


def _conv_bwd(dxbc, acc, proj_a, conv_w, d, name):
    l, cdim = acc.shape
    cw = _conv_cols(d)
    off = d // cw
    tl = _pick(l, 512, 8)
    hb = tl // 8
    nb = l // tl

    def dsilu(g, a):
        s = _sigmoid(a)
        return g * (s * (1.0 + a * (1.0 - s)))

    def body(g_ref, gn_ref, a_ref, an_ref, u_ref, up_ref, w_ref, du_ref, dw_ref, ext, dext):
        i = pl.program_id(1)
        da = dsilu(g_ref[...], a_ref[...])
        dext[0:tl, :] = da
        dext[tl:tl + 8, :] = jnp.where(i < nb - 1, dsilu(gn_ref[...], an_ref[...]), 0.0)
        ext[0:8, :] = jnp.where(i > 0, up_ref[...], 0.0)
        ext[8:8 + tl, :] = u_ref[...]
        du = jnp.zeros((tl, cw), F32)
        rows = []
        for j in range(KCONV):
            du = du + w_ref[j:j + 1, :] * dext[pl.ds(KCONV - 1 - j, tl), :]
            rows.append(jnp.sum(da * ext[pl.ds(8 - (KCONV - 1) + j, tl), :], axis=0, keepdims=True))
        rows.append(jnp.sum(da, axis=0, keepdims=True))
        du_ref[...] = du.astype(BF16)

        @pl.when(i == 0)
        def _():
            dw_ref[...] = jnp.zeros_like(dw_ref)

        for j, rv in enumerate(rows):
            dw_ref[j:j + 1, :] += rv

    blk = pl.BlockSpec((tl, cw), lambda c, i: (i, c))
    nxt = pl.BlockSpec((8, cw), lambda c, i: (jnp.minimum((i + 1) * hb, l // 8 - 1), c))
    return pl.pallas_call(
        body, name=name, grid=(cdim // cw, nb),
        in_specs=[blk, nxt, blk, nxt,
                  pl.BlockSpec((tl, cw), lambda c, i: (i, c + off)),
                  pl.BlockSpec((8, cw), lambda c, i: (jnp.maximum(i * hb - 1, 0), c + off)),
                  pl.BlockSpec((KCONV, cw), lambda c, i: (0, c))],
        out_specs=[blk, pl.BlockSpec((8, cw), lambda c, i: (0, c))],
        out_shape=[jax.ShapeDtypeStruct((l, cdim), BF16), jax.ShapeDtypeStruct((8, cdim), F32)],
        scratch_shapes=[pltpu.VMEM((tl + 8, cw), F32), pltpu.VMEM((tl + 8, cw), F32)],
        compiler_params=_params("parallel", "arbitrary"),
    )(dxbc, dxbc, acc, acc, proj_a, proj_a, conv_w)


def _ssd_common(dt_ref, bias_ref, alog_ref):
    li = lax.broadcasted_iota(jnp.int32, (CHUNK, CHUNK), 0)
    si = lax.broadcasted_iota(jnp.int32, (CHUNK, CHUNK), 1)
    tri = si <= li
    dtv = _softplus(dt_ref[...] + bias_ref[0])
    a_neg = -jnp.exp(alog_ref[0])
    da = dtv * a_neg
    cs_col = sum(_dot(tri.astype(BF16), p, NN) for p in _split3(da))
    cs_row = sum(_dot(p, (li <= si).astype(BF16), TN) for p in _split3(da))
    return tri, dtv, a_neg, cs_col, cs_row


def _ssd_specs(d, r_heads):
    pg = r_heads * SSD_HD
    nb = d // LANES
    dt_blk = (2 * d + 2 * NGROUPS * NSTATE) // LANES
    x_spec = lambda cmap: pl.BlockSpec((CHUNK, pg), lambda g, c: (cmap(c), g))
    b_spec = lambda cmap: pl.BlockSpec((CHUNK, NSTATE), lambda g, c: (cmap(c), nb + g))
    c_spec = lambda cmap: pl.BlockSpec((CHUNK, NSTATE), lambda g, c: (cmap(c), nb + NGROUPS + g))
    dt_spec = lambda cmap: pl.BlockSpec((CHUNK, LANES), lambda g, c: (cmap(c), dt_blk + g))
    const = pl.BlockSpec((1, 1, LANES), lambda g, c: (g, 0, 0))
    return pg, x_spec, b_spec, c_spec, dt_spec, const


def _ssd_fwd(xbc, proj_a, bias_g, alog_g, dsk_g, d, name):
    l = xbc.shape[0]
    nc = l // CHUNK
    r_heads = d // SSD_HD // NGROUPS
    pg, x_spec, b_spec, c_spec, dt_spec, const = _ssd_specs(d, r_heads)
    ident = lambda c: c

    def body(x_ref, b_ref, c_ref, dt_ref, bias_ref, alog_ref, dsk_ref, y_ref, hs_ref, h_scr):
        @pl.when(pl.program_id(1) == 0)
        def _():
            h_scr[...] = jnp.zeros_like(h_scr)

        tri, dtv, _, cs_col, cs_row = _ssd_common(dt_ref, bias_ref, alog_ref)
        hin = h_scr[...]
        hs_ref[0, 0] = hin
        bm = b_ref[...].astype(BF16)
        cm = c_ref[...].astype(BF16)
        gmat = _dot(cm, bm, NT)
        yoff = _dot(cm, hin.astype(BF16), NN)
        dsk = dsk_ref[0]
        for r in range(r_heads):
            sl = slice(SSD_HD * r, SSD_HD * (r + 1))
            col = cs_col[:, r:r + 1]
            row = cs_row[r:r + 1, :]
            lm = jnp.exp(jnp.where(tri, col - row, -jnp.inf))
            xr = x_ref[:, sl]
            xd = xr * dtv[:, r:r + 1]
            yr = _dot((gmat * lm).astype(BF16), xd.astype(BF16), NN) + jnp.exp(col) * yoff[:, sl] + dsk[:, r:r + 1] * xr
            y_ref[:, sl] = yr
            tot = cs_col[CHUNK - 1:CHUNK, r:r + 1]
            xds = (xd * jnp.exp(tot - col)).astype(BF16)
            h_scr[:, sl] = hin[:, sl] * jnp.exp(tot) + _dot(bm, xds, TN)

    return pl.pallas_call(
        body, name=name, grid=(NGROUPS, nc),
        in_specs=[x_spec(ident), b_spec(ident), c_spec(ident), dt_spec(ident), const, const, const],
        out_specs=[x_spec(ident), pl.BlockSpec((1, 1, NSTATE, pg), lambda g, c: (c, g, 0, 0))],
        out_shape=[jax.ShapeDtypeStruct((l, d), F32), jax.ShapeDtypeStruct((nc, NGROUPS, NSTATE, pg), F32)],
        scratch_shapes=[pltpu.VMEM((NSTATE, pg), F32)], compiler_params=_params("parallel", "arbitrary"),
    )(xbc, xbc, xbc, proj_a, bias_g, alog_g, dsk_g)


def _ssd_bwd(dy, xbc, proj_a, hs, bias_g, alog_g, dsk_g, d, name):
    l = xbc.shape[0]
    nc = l // CHUNK
    r_heads = d // SSD_HD // NGROUPS
    pg, x_spec, b_spec, c_spec, dt_spec, const = _ssd_specs(d, r_heads)
    rev = lambda c: nc - 1 - c
    cdim = d + 2 * NGROUPS * NSTATE

    def body(dy_ref, x_ref, b_ref, c_ref, dt_ref, hs_ref, bias_ref, alog_ref, dsk_ref,
             dx_ref, db_ref, dc_ref, ddt_ref, sums_ref, dh_scr, xds_scr, dye_scr):
        first = pl.program_id(1) == 0

        @pl.when(first)
        def _():
            dh_scr[...] = jnp.zeros_like(dh_scr)
            sums_ref[...] = jnp.zeros_like(sums_ref)

        tri, dtv, a_neg, cs_col, cs_row = _ssd_common(dt_ref, bias_ref, alog_ref)
        lane = lax.broadcasted_iota(jnp.int32, (CHUNK, LANES), 1)
        subl = lax.broadcasted_iota(jnp.int32, (LANES, CHUNK), 0)
        lastrow = lax.broadcasted_iota(jnp.int32, (CHUNK, 1), 0) == CHUNK - 1
        hin = hs_ref[0, 0]
        dhout = dh_scr[...]
        bm = b_ref[...].astype(BF16)
        cm = c_ref[...].astype(BF16)
        gmat = _dot(cm, bm, NT)
        yoff = _dot(cm, hin.astype(BF16), NN)
        qall = _dot(bm, dhout.astype(BF16), NN)
        dsk = dsk_ref[0]
        dg_sum = jnp.zeros((CHUNK, CHUNK), F32)
        dcs_colf = jnp.zeros((CHUNK, LANES), F32)
        dcs_rowf = jnp.zeros((LANES, CHUNK), F32)
        ddt_colf = jnp.zeros((CHUNK, LANES), F32)
        dsk_acc = jnp.zeros((1, LANES), F32)
        for r in range(r_heads):
            sl = slice(SSD_HD * r, SSD_HD * (r + 1))
            col = cs_col[:, r:r + 1]
            row = cs_row[r:r + 1, :]
            lm = jnp.exp(jnp.where(tri, col - row, -jnp.inf))
            mmat = gmat * lm
            xr = x_ref[:, sl]
            dtr = dtv[:, r:r + 1]
            xd = xr * dtr
            xd16 = xd.astype(BF16)
            dyr = dy_ref[:, sl]
            dyr16 = dyr.astype(BF16)
            ecs = jnp.exp(col)
            tot = cs_col[CHUNK - 1:CHUNK, r:r + 1]
            etot = jnp.exp(tot)
            decs = jnp.exp(tot - col)
            dm = _dot(dyr16, xd16, NT)
            dg_sum = dg_sum + dm * lm
            pmat = dm * mmat
            dcs_c = jnp.sum(pmat, axis=1, keepdims=True)
            dcs_r = jnp.sum(pmat, axis=0, keepdims=True)
            dxd = _dot(mmat.astype(BF16), dyr16, TN)
            dye = dyr * ecs
            dye_scr[:, sl] = dye.astype(BF16)
            dcs_c = dcs_c + jnp.sum(dye * yoff[:, sl], axis=1, keepdims=True)
            qr = qall[:, sl]
            dxd = dxd + qr * decs
            ddec = jnp.sum(qr * xd, axis=1, keepdims=True) * decs
            dcs_c = dcs_c - ddec
            dtot = jnp.sum(ddec, axis=0, keepdims=True) + etot * jnp.sum(
                jnp.sum(dhout[:, sl] * hin[:, sl], axis=1, keepdims=True), axis=0, keepdims=True)
            dcs_c = dcs_c + jnp.where(lastrow, dtot, 0.0)
            xds_scr[:, sl] = (xd * decs).astype(BF16)
            dh_scr[:, sl] = dhout[:, sl] * etot
            dx_ref[:, sl] = dxd * dtr + dsk[:, r:r + 1] * dyr
            ddt_colf = ddt_colf + jnp.where(lane == r, jnp.sum(dxd * xr, axis=1, keepdims=True), 0.0)
            dcs_colf = dcs_colf + jnp.where(lane == r, dcs_c, 0.0)
            dcs_rowf = dcs_rowf + jnp.where(subl == r, dcs_r, 0.0)
            dsk_acc = dsk_acc + jnp.where(lane[0:1] == r, jnp.sum(jnp.sum(dyr * xr, axis=1, keepdims=True), axis=0, keepdims=True), 0.0)
        dye16 = dye_scr[...]
        xds16 = xds_scr[...]
        dg16 = dg_sum.astype(BF16)
        dc_ref[...] = _dot(dg16, bm, NN) + _dot(dye16, hin.astype(BF16), NT)
        db_ref[...] = _dot(dg16, cm, TN) + _dot(xds16, dhout.astype(BF16), NT)
        dh_scr[...] += _dot(cm, dye16, TN)
        eye = (lax.broadcasted_iota(jnp.int32, (LANES, LANES), 0) == lax.broadcasted_iota(jnp.int32, (LANES, LANES), 1)).astype(BF16)
        dcs = dcs_colf - sum(_dot(p, eye, TN) for p in _split3(dcs_rowf))
        li = lax.broadcasted_iota(jnp.int32, (CHUNK, CHUNK), 0)
        si = lax.broadcasted_iota(jnp.int32, (CHUNK, CHUNK), 1)
        dda = sum(_dot((li <= si).astype(BF16), p, NN) for p in _split3(dcs))
        ddt = ddt_colf + dda * a_neg
        draw = ddt * _sigmoid(dt_ref[...] + bias_ref[0])
        ddt_ref[...] = draw
        sums_ref[0, 0:1, :] += jnp.sum(draw, axis=0, keepdims=True)
        sums_ref[0, 1:2, :] += jnp.sum(dda * dtv, axis=0, keepdims=True)
        sums_ref[0, 2:3, :] += dsk_acc

    grp = pl.BlockSpec((CHUNK, LANES), lambda g, c: (rev(c), g))
    return pl.pallas_call(
        body, name=name, grid=(NGROUPS, nc),
        in_specs=[x_spec(rev), x_spec(rev), b_spec(rev), c_spec(rev), dt_spec(rev),
                  pl.BlockSpec((1, 1, NSTATE, pg), lambda g, c: (rev(c), g, 0, 0)), const, const, const],
        out_specs=[x_spec(rev), grp, grp, grp, pl.BlockSpec((1, 8, LANES), lambda g, c: (g, 0, 0))],
        out_shape=[jax.ShapeDtypeStruct((l, d), F32), jax.ShapeDtypeStruct((l, NGROUPS * NSTATE), F32),
                   jax.ShapeDtypeStruct((l, NGROUPS * NSTATE), F32),
                   jax.ShapeDtypeStruct((l, NGROUPS * LANES), F32), jax.ShapeDtypeStruct((NGROUPS, 8, LANES), F32)],
        scratch_shapes=[pltpu.VMEM((NSTATE, pg), F32), pltpu.VMEM((CHUNK, pg), BF16), pltpu.VMEM((CHUNK, pg), BF16)],
        compiler_params=_params("parallel", "arbitrary"),
    )(dy, xbc, xbc, xbc, proj_a, hs, bias_g, alog_g, dsk_g)


def _sba_tile(l):
    return _pick(l, 256, LANES)


def _sba_scores(q, kb, causal, diag, scale):
    z = _dot(q, kb, NT) * scale
    sp = _softplus(z)
    lk = -sp
    if diag:
        lk = jnp.where(causal, lk, 0.0)
    return lk, z - sp


def _sba_fwd(proj_a, proj_c, d, g_off, name):
    l = proj_c.shape[0]
    nh = d // SBA_HD
    t = _sba_tile(l)
    assert l // t <= LANES
    scale = 1.0 / math.sqrt(SBA_HD)

    def body(q_ref, k_ref, v_ref, g_ref, o_ref, y_ref, rs_ref):
        i = pl.program_id(1)
        q = q_ref[...]
        ti = lax.broadcasted_iota(jnp.int32, (t, t), 0)
        si = lax.broadcasted_iota(jnp.int32, (t, t), 1)
        lane = lax.broadcasted_iota(jnp.int32, (t, LANES), 1)
        uex = (ti > si).astype(BF16)
        causal = si < ti

        def tile(j, rsum, acc, rs, diag):
            start = pl.multiple_of(j * t, t)
            kb = k_ref[pl.ds(start, t), :]
            vb = v_ref[pl.ds(start, t), :]
            lk, la = _sba_scores(q, kb, causal, diag, scale)
            hi, lo = _split2(lk)
            later = _dot(hi, uex, NN) + _dot(lo, uex, NN)
            w = jnp.exp(la + later + rsum)
            if diag:
                w = jnp.where(causal, w, 0.0)
            acc = acc + _dot(w.astype(BF16), vb, NN)
            rs = jnp.where(lane == j, rsum, rs)
            return rsum + later[:, 0:1] + lk[:, 0:1], acc, rs

        carry = tile(i, jnp.zeros((t, 1), F32), jnp.zeros((t, SBA_HD), F32), jnp.zeros((t, LANES), F32), True)
        _, acc, rs = lax.fori_loop(0, i, lambda jj, c: tile(i - 1 - jj, c[0], c[1], c[2], False), carry)
        o_ref[...] = acc
        rs_ref[...] = rs
        gv = g_ref[...]
        y_ref[...] = (acc * (gv * _sigmoid(gv))).astype(BF16)

    goff = g_off // SBA_HD
    blk = lambda off: pl.BlockSpec((t, SBA_HD), lambda h, i: (i, h + off))
    full = lambda off: pl.BlockSpec((l, SBA_HD), lambda h, i: (0, h + off))
    out = pl.BlockSpec((t, SBA_HD), lambda h, i: (i, h))
    return pl.pallas_call(
        body, name=name, grid=(nh, l // t), in_specs=[blk(0), full(nh), full(2 * nh), blk(goff)], out_specs=[out, out, out],
        out_shape=[jax.ShapeDtypeStruct((l, d), F32), jax.ShapeDtypeStruct((l, d), BF16), jax.ShapeDtypeStruct((l, d), F32)],
        compiler_params=_params("parallel", "arbitrary"),
    )(proj_c, proj_c, proj_c, proj_a)


def _sba_bwd(dys, o, rs, proj_a, proj_c, d, g_off, name):
    l = proj_c.shape[0]
    nh = d // SBA_HD
    t = _sba_tile(l)
    scale = 1.0 / math.sqrt(SBA_HD)

    def body(dy_ref, o_ref, rs_ref, q_ref, k_ref, v_ref, g_ref, dq_ref, dk_ref, dv_ref, dg_ref, dk_acc, dv_acc):
        i = pl.program_id(1)
        nq = pl.num_programs(1)

        @pl.when(i == 0)
        def _():
            dk_acc[...] = jnp.zeros_like(dk_acc)
            dv_acc[...] = jnp.zeros_like(dv_acc)

        q = q_ref[...]
        gv = g_ref[...]
        sg = _sigmoid(gv)
        dyv = dy_ref[...]
        dg_ref[...] = (dyv * o_ref[...] * (sg * (1.0 + gv * (1.0 - sg)))).astype(BF16)
        do16 = (dyv * (gv * sg)).astype(BF16)
        rs = rs_ref[...]
        ti = lax.broadcasted_iota(jnp.int32, (t, t), 0)
        si = lax.broadcasted_iota(jnp.int32, (t, t), 1)
        lane = lax.broadcasted_iota(jnp.int32, (t, LANES), 1)
        uex = (ti > si).astype(BF16)
        ulow = (ti < si).astype(BF16)
        causal = si < ti

        def tile(j, epre, dq, diag):
            start = pl.multiple_of(j * t, t)
            kb = k_ref[pl.ds(start, t), :]
            vb = v_ref[pl.ds(start, t), :]
            rsum = jnp.sum(jnp.where(lane == j, rs, 0.0), axis=1, keepdims=True)
            lk, la = _sba_scores(q, kb, causal, diag, scale)
            hi, lo = _split2(lk)
            later = _dot(hi, uex, NN) + _dot(lo, uex, NN)
            w = jnp.exp(la + later + rsum)
            if diag:
                w = jnp.where(causal, w, 0.0)
            e = w * _dot(do16, vb, NT)
            ehi, elo = _split2(e)
            epx = _dot(ehi, ulow, NN) + _dot(elo, ulow, NN)
            dz = (e - jnp.exp(la) * (e + epre + epx)) * scale
            if diag:
                dz = jnp.where(causal, dz, 0.0)
            dz16 = dz.astype(BF16)
            dq = dq + _dot(dz16, kb, NN)
            dk_acc[pl.ds(start, t), :] += _dot(dz16, q, TN)
            dv_acc[pl.ds(start, t), :] += _dot(w.astype(BF16), do16, TN)
            return epre + epx[:, t - 1:t] + e[:, t - 1:t], dq

        carry = (jnp.zeros((t, 1), F32), jnp.zeros((t, SBA_HD), F32))
        carry = lax.fori_loop(0, i, lambda j, c: tile(j, c[0], c[1], False), carry)
        _, dq = tile(i, carry[0], carry[1], True)
        dq_ref[...] = dq.astype(BF16)

        @pl.when(i == nq - 1)
        def _():
            dk_ref[...] = dk_acc[...].astype(BF16)
            dv_ref[...] = dv_acc[...].astype(BF16)

    goff = g_off // SBA_HD
    blk = lambda off: pl.BlockSpec((t, SBA_HD), lambda h, i: (i, h + off))
    full = lambda off: pl.BlockSpec((l, SBA_HD), lambda h, i: (0, h + off))
    out = pl.BlockSpec((t, SBA_HD), lambda h, i: (i, h))
    outfull = pl.BlockSpec((l, SBA_HD), lambda h, i: (0, h))
    sd = jax.ShapeDtypeStruct((l, d), BF16)
    return pl.pallas_call(
        body, name=name, grid=(nh, l // t),
        in_specs=[out, out, out, blk(0), full(nh), full(2 * nh), blk(goff)],
        out_specs=[out, outfull, outfull, out], out_shape=[sd, sd, sd, sd],
        scratch_shapes=[pltpu.VMEM((l, SBA_HD), F32), pltpu.VMEM((l, SBA_HD), F32)],
        compiler_params=_params("parallel", "arbitrary"),
    )(dys, o, rs, proj_c, proj_c, proj_c, proj_a)


def _adamw_math(w, g, m, v):
    m = ADAM_B1 * m + (1.0 - ADAM_B1) * g
    v = ADAM_B2 * v + (1.0 - ADAM_B2) * (g * g)
    m_hat = m / (1.0 - ADAM_B1 ** ADAM_STEP)
    v_hat = v / (1.0 - ADAM_B2 ** ADAM_STEP)
    delta = -ADAM_LR * (m_hat / (jnp.sqrt(v_hat) + ADAM_EPS) + ADAM_WD * w)
    return delta, m, v


def _adamw(w, g, m, v, name):
    a, r, c = w.shape
    tr = _pick(r, max(8, (1 << 19) // c // 8 * 8), 8)

    def body(w_ref, g_ref, m_ref, v_ref, d_ref, nm_ref, nv_ref):
        dl, nm, nv = _adamw_math(w_ref[...], g_ref[...], m_ref[...], v_ref[...])
        d_ref[...] = dl
        nm_ref[...] = nm
        nv_ref[...] = nv

    blk = pl.BlockSpec((1, tr, c), lambda i, j: (i, j, 0))
    sd = jax.ShapeDtypeStruct(w.shape, F32)
    return pl.pallas_call(
        body, name=name, grid=(a, r // tr), in_specs=[blk] * 4, out_specs=[blk] * 3, out_shape=[sd] * 3,
        compiler_params=_params("parallel", "parallel"),
    )(w, g, m, v)


def _dims(d):
    cdim = d + 2 * NGROUPS * NSTATE
    heads = d // SSD_HD
    r_heads = heads // NGROUPS
    g_off = d + cdim + NGROUPS * LANES
    na = g_off + d
    nc = 3 * d
    return cdim, heads, r_heads, na, nc, g_off


def _pack_w_in(w_in, d):
    cdim, heads, r_heads, na, nc, g_off = _dims(d)
    o = d + cdim
    w_dt = w_in[:, o:o + heads].reshape(d, NGROUPS, r_heads)
    w_dt = jnp.pad(w_dt, ((0, 0), (0, 0), (0, LANES - r_heads))).reshape(d, NGROUPS * LANES)
    return jnp.concatenate([w_in[:, :o], w_dt, w_in[:, o + heads + nc:]], axis=1), w_in[:, o + heads:o + heads + nc]


def _unpack_w_in(ga, gc, d):
    cdim, heads, r_heads, na, nc, g_off = _dims(d)
    o = d + cdim
    g_dt = ga[:, o:g_off].reshape(d, NGROUPS, LANES)[:, :, :r_heads].reshape(d, heads)
    return jnp.concatenate([ga[:, :o], g_dt, gc, ga[:, g_off:]], axis=1)


def _group_vec(v, r_heads):
    return jnp.pad(v.reshape(NGROUPS, 1, r_heads), ((0, 0), (0, 0), (0, LANES - r_heads)))


def _layer_fwd(x, p, d, tag):
    cdim, heads, r_heads, na, nc, g_off = _dims(d)
    h = _rmsnorm_fwd(x, p["norm_w"], f"norm_f{tag}")
    proj_a = _matmul(h, p["wa"], "nn", F32, f"inproj_a{tag}")
    proj_c = _matmul(h, p["wc"], "nn", BF16, f"inproj_c{tag}")
    acc, xbc = _conv_fwd(proj_a, p["conv_w"], p["conv_b"], d, f"conv_f{tag}")
    y, hs = _ssd_fwd(xbc, proj_a, p["bias_g"], p["alog_g"], p["dsk_g"], d, f"ssd_f{tag}")
    y_ssd = _gatenorm_fwd(y, proj_a, p["ssd_norm_w"], f"gate_f{tag}")
    o, y_sba, rs = _sba_fwd(proj_a, proj_c, d, g_off, f"sba_f{tag}")
    mix = jnp.concatenate([y_ssd, y_sba], axis=1)
    x_next = _matmul(mix, p["w_out"], "nn", F32, f"outproj{tag}", add=x)
    return x_next, dict(x=x, h=h, proj_a=proj_a, proj_c=proj_c, acc=acc, xbc=xbc, y=y, hs=hs, o=o, rs=rs, mix=mix)


def _layer_bwd(dxn, s, p, d, tag):
    cdim, heads, r_heads, na, nc, g_off = _dims(d)
    dxn16 = dxn.astype(BF16)
    dmix = _matmul(dxn16, p["w_out"], "nt", F32, f"dmix{tag}")
    g_w_out = _matmul(s["mix"], dxn16, "tn", F32, f"dwout{tag}")
    dq, dk, dv, dg = _sba_bwd(dmix[:, d:], s["o"], s["rs"], s["proj_a"], s["proj_c"], d, g_off, f"sba_b{tag}")
    dy, dz, g_ssd_norm = _gatenorm_bwd(dmix[:, :d], s["y"], s["proj_a"], p["ssd_norm_w"], f"gate_b{tag}")
    dx_s, db_s, dc_s, ddt, sums = _ssd_bwd(dy, s["xbc"], s["proj_a"], s["hs"], p["bias_g"], p["alog_g"], p["dsk_g"], d, f"ssd_b{tag}")
    dxbc = jnp.concatenate([dx_s, db_s, dc_s], axis=1)
    du, g_conv = _conv_bwd(dxbc, s["acc"], s["proj_a"], p["conv_w"], d, f"conv_b{tag}")
    dproj_a = jnp.concatenate([dz, du, ddt.astype(BF16), dg], axis=1)
    dproj_c = jnp.concatenate([dq, dk, dv], axis=1)
    g_wa = _matmul(s["h"], dproj_a, "tn", F32, f"dwin_a{tag}")
    g_wc = _matmul(s["h"], dproj_c, "tn", F32, f"dwin_c{tag}")
    dh = _matmul(dproj_a, p["wa"], "nt", F32, f"dh_a{tag}")
    dh = _matmul(dproj_c, p["wc"], "nt", F32, f"dh_c{tag}", add=dh)
    dx, g_norm = _rmsnorm_bwd(dh, s["x"], p["norm_w"], dxn, f"norm_b{tag}")
    a_neg = -jnp.exp(p["alog_g"][:, 0, :r_heads].reshape(heads))
    grads = dict(
        norm_w=g_norm[0], w_in=_unpack_w_in(g_wa, g_wc, d), conv_w=g_conv[:KCONV], conv_b=g_conv[KCONV],
        dt_bias=sums[:, 0, :r_heads].reshape(heads), a_log=sums[:, 1, :r_heads].reshape(heads) * a_neg,
        d_skip=sums[:, 2, :r_heads].reshape(heads), ssd_norm_w=g_ssd_norm[0], w_out=g_w_out)
    return dx, grads


def _local_step(x, target, w_in16, w_out16, conv_w, small):
    l, d = x.shape
    depth = w_in16.shape[0]
    r_heads = _dims(d)[2]
    layers = []
    for i in range(depth):
        wa, wc = _pack_w_in(w_in16[i], d)
        layers.append(dict(
            norm_w=small["norm_w"][i][None], wa=wa, wc=wc, conv_w=conv_w[i], conv_b=small["conv_b"][i][None],
            bias_g=_group_vec(small["dt_bias"][i], r_heads), alog_g=_group_vec(small["a_log"][i], r_heads),
            dsk_g=_group_vec(small["d_skip"][i], r_heads), ssd_norm_w=small["ssd_norm_w"][i][None], w_out=w_out16[i]))
    saved = []
    hcur = x
    for i in range(depth):
        hcur, s = _layer_fwd(hcur, layers[i], d, str(i))
        saved.append(s)
    dh, g_final, loss = _final_loss(hcur, small["final_norm_w"][None], target, "final_loss")
    grads = [None] * depth
    for i in reversed(range(depth)):
        dh, grads[i] = _layer_bwd(dh, saved[i], layers[i], d, str(i))
    stacked = {k: jnp.stack([g[k] for g in grads]) for k in grads[0]}
    stacked["final_norm_w"] = g_final[0]
    return loss[0, 0], dh, stacked


HBM = pl.BlockSpec(memory_space=pl.ANY)
NCHIP = 4
NDEV = 8


def _mesh_pos():
    x, y, c = lax.axis_index("x"), lax.axis_index("y"), lax.axis_index("c")
    chips = [(1 - x, y), (x, 1 - y), (1 - x, 1 - y)]
    return x, y, c, chips


def _remote(src, dst, send_sem, recv_sem, dev):
    return pltpu.make_async_remote_copy(src_ref=src, dst_ref=dst, send_sem=send_sem, recv_sem=recv_sem,
                                        device_id=dev, device_id_type=MESH)


def _gather_weights(shards):
    n = len(shards)
    hl = shards[0].shape[0] // 2

    def body(*refs):
        ins, outs = refs[:n], refs[n:2 * n]
        send, recv, loc = refs[2 * n:]
        x, y, c, chips = _mesh_pos()
        k = 2 * x + y
        half = pl.ds(c * hl, hl)
        other = pl.ds((1 - c) * hl, hl)
        copies = []
        for a in range(n):
            own = pltpu.make_async_copy(ins[a], outs[a].at[k], loc.at[a])
            own.start()
            copies.append(own)
            for j, (px, py) in enumerate(chips):
                _remote(ins[a].at[half], outs[a].at[k, half], send.at[a, j], recv.at[a, j], (px, py, c)).start()
        for a in range(n):
            for j, (px, py) in enumerate(chips):
                kj = 2 * px + py
                got = outs[a].at[kj, half]
                _remote(got, got, send.at[a, j], recv.at[a, j], (px, py, c)).wait_recv()
                _remote(got, got, send.at[a, 3 + j], recv.at[a, 3 + j], (x, y, 1 - c)).start()
        for a in range(n):
            for j, (px, py) in enumerate(chips):
                kj = 2 * px + py
                _remote(outs[a].at[kj, other], outs[a].at[kj, other], send.at[a, 3 + j], recv.at[a, 3 + j], (x, y, 1 - c)).wait_recv()
            for j, (px, py) in enumerate(chips):
                kj = 2 * px + py
                _remote(ins[a].at[half], outs[a].at[k, half], send.at[a, j], recv.at[a, j], (px, py, c)).wait_send()
                _remote(outs[a].at[kj, half], outs[a].at[kj, half], send.at[a, 3 + j], recv.at[a, 3 + j], (x, y, 1 - c)).wait_send()
        for own in copies:
            own.wait()

    return pl.pallas_call(
        body, name="gather_weights", in_specs=[HBM] * n, out_specs=[HBM] * n,
        out_shape=[jax.ShapeDtypeStruct((NCHIP,) + s.shape, s.dtype) for s in shards],
        scratch_shapes=[pltpu.SemaphoreType.DMA((n, 6)), pltpu.SemaphoreType.DMA((n, 6)), pltpu.SemaphoreType.DMA((n,))],
    )(*shards)


def _swap_halves(parts):
    n = len(parts)
    hl = parts[0].shape[1] // 2

    def body(*refs):
        ins, outs = refs[:n], refs[n:2 * n]
        send, recv = refs[2 * n:]
        x, y, c, _ = _mesh_pos()
        cps = [_remote(ins[a].at[:, pl.ds((1 - c) * hl, hl)], outs[a], send.at[a], recv.at[a], (x, y, 1 - c)) for a in range(n)]
        for cp in cps:
            cp.start()
        for cp in cps:
            cp.wait()

    return pl.pallas_call(
        body, name="grad_swap_halves", in_specs=[HBM] * n, out_specs=[HBM] * n,
        out_shape=[jax.ShapeDtypeStruct((NCHIP, hl) + p.shape[2:], p.dtype) for p in parts],
        scratch_shapes=[pltpu.SemaphoreType.DMA((n,)), pltpu.SemaphoreType.DMA((n,))],
    )(*parts)


def _exchange_shards(parts):
    n = len(parts)

    def body(*refs):
        ins, outs = refs[:n], refs[n:2 * n]
        send, recv, loc = refs[2 * n:]
        x, y, c, chips = _mesh_pos()
        k = 2 * x + y
        work = []
        for a in range(n):
            own = pltpu.make_async_copy(ins[a].at[k], outs[a].at[k], loc.at[a])
            own.start()
            work.append(own)
            for j, (px, py) in enumerate(chips):
                cp = _remote(ins[a].at[2 * px + py], outs[a].at[k], send.at[a, j], recv.at[a, j], (px, py, c))
                cp.start()
                work.append(cp)
        for w in work:
            w.wait()

    return pl.pallas_call(
        body, name="grad_exchange", in_specs=[HBM] * n, out_specs=[HBM] * n,
        out_shape=[jax.ShapeDtypeStruct(p.shape, p.dtype) for p in parts],
        scratch_shapes=[pltpu.SemaphoreType.DMA((n, 3)), pltpu.SemaphoreType.DMA((n, 3)), pltpu.SemaphoreType.DMA((n,))],
    )(*parts)


def _join_halves(halves):
    n = len(halves)
    hl = halves[0].shape[0]

    def body(*refs):
        ins, outs = refs[:n], refs[n:2 * n]
        send, recv, loc = refs[2 * n:]
        x, y, c, _ = _mesh_pos()
        mine = pl.ds(c * hl, hl)
        work = []
        for a in range(n):
            own = pltpu.make_async_copy(ins[a], outs[a].at[mine], loc.at[a])
            own.start()
            work.append(own)
            cp = _remote(ins[a], outs[a].at[mine], send.at[a], recv.at[a], (x, y, 1 - c))
            cp.start()
            work.append(cp)
        for w in work:
            w.wait()

    return pl.pallas_call(
        body, name="grad_join_halves", in_specs=[HBM] * n, out_specs=[HBM] * n,
        out_shape=[jax.ShapeDtypeStruct((2 * hl,) + h.shape[1:], h.dtype) for h in halves],
        scratch_shapes=[pltpu.SemaphoreType.DMA((n,)), pltpu.SemaphoreType.DMA((n,)), pltpu.SemaphoreType.DMA((n,))],
    )(*halves)


def _all_sum_small(vec, name):
    r = vec.shape[0]

    def body(v_ref, o_ref, buf, send, recv):
        x, y, c, _ = _mesh_pos()
        me = 4 * x + 2 * y + c
        buf[me] = v_ref[...]
        cps = []
        for mask in range(1, NDEV):
            fx, fy, fc = (mask >> 2) & 1, (mask >> 1) & 1, mask & 1
            peer = (1 - x if fx else x, 1 - y if fy else y, 1 - c if fc else c)
            cp = _remote(v_ref, buf.at[me], send.at[mask - 1], recv.at[mask - 1], peer)
            cp.start()
            cps.append(cp)
        for cp in cps:
            cp.wait()
        total = buf[0]
        for dev in range(1, NDEV):
            total = total + buf[dev]
        o_ref[...] = total

    vm = pl.BlockSpec(memory_space=pltpu.VMEM)
    return pl.pallas_call(
        body, name=name, in_specs=[vm], out_specs=vm, out_shape=jax.ShapeDtypeStruct((r, LANES), F32),
        scratch_shapes=[pltpu.VMEM((NDEV, r, LANES), F32), pltpu.SemaphoreType.DMA((NDEV - 1,)), pltpu.SemaphoreType.DMA((NDEV - 1,))],
    )(vec)


def _add_pairs(a, b, out_dtype, name):
    n0, n1, r, c = a.shape
    tr = _pick(r, max(8, (1 << 19) // c // 8 * 8), 8)

    def body(a_ref, b_ref, o_ref):
        o_ref[...] = (a_ref[...] + b_ref[...]).astype(out_dtype)

    blk = pl.BlockSpec((1, 1, tr, c), lambda i, j, t: (i, j, t, 0))
    return pl.pallas_call(
        body, name=name, grid=(n0, n1, r // tr), in_specs=[blk, blk], out_specs=blk,
        out_shape=jax.ShapeDtypeStruct(a.shape, out_dtype), compiler_params=_params("parallel", "parallel", "parallel"),
    )(a, b)


def _sum_chips(p, name):
    _, hl, r, c = p.shape
    tr = _pick(r, max(16, (1 << 19) // c // 16 * 16), 16)

    def body(p_ref, o_ref):
        total = p_ref[0].astype(F32)
        for j in range(1, NCHIP):
            total = total + p_ref[j].astype(F32)
        o_ref[...] = total

    return pl.pallas_call(
        body, name=name, grid=(hl, r // tr),
        in_specs=[pl.BlockSpec((NCHIP, 1, tr, c), lambda i, t: (0, i, t, 0))],
        out_specs=pl.BlockSpec((1, tr, c), lambda i, t: (i, t, 0)),
        out_shape=jax.ShapeDtypeStruct((hl, r, c), F32), compiler_params=_params("parallel", "parallel"),
    )(p)


def _reduce_scatter(parts):
    c = lax.axis_index("c")
    hl = parts[0].shape[1] // 2
    theirs = _swap_halves(parts)
    mine = [lax.dynamic_slice_in_dim(p, c * hl, hl, axis=1) for p in parts]
    chip_sum = [_add_pairs(m, t, BF16, f"grad_pair_sum{i}") for i, (m, t) in enumerate(zip(mine, theirs))]
    gathered = _exchange_shards(chip_sum)
    halves = [_sum_chips(g, f"grad_chip_sum{i}") for i, g in enumerate(gathered)]
    return _join_halves(halves)


SMALL = ("norm_w", "conv_w", "conv_b", "dt_bias", "a_log", "d_skip", "ssd_norm_w", "final_norm_w")


def _pack(arrays):
    flat = jnp.concatenate([a.reshape(-1).astype(F32) for a in arrays])
    rows = -(-flat.shape[0] // (8 * LANES)) * 8
    return jnp.pad(flat, (0, rows * LANES - flat.shape[0])).reshape(rows, LANES)


def _unpack(vec, shapes):
    flat = vec.reshape(-1)
    out, pos = [], 0
    for s in shapes:
        n = math.prod(s)
        out.append(flat[pos:pos + n].reshape(s))
        pos += n
    return out


def kernel(x, norm_w, w_in, conv_w, conv_b, dt_bias, a_log, d_skip, ssd_norm_w, w_out, final_norm_w, loss_target, m_norm_w, m_w_in, m_conv_w, m_conv_b, m_dt_bias, m_a_log, m_d_skip, m_ssd_norm_w, m_w_out, m_final_norm_w, v_norm_w, v_w_in, v_conv_w, v_conv_b, v_dt_bias, v_a_log, v_d_skip, v_ssd_norm_w, v_w_out, v_final_norm_w):
    depth, d, ics = w_in.shape
    cs = conv_w.shape[2]
    xi, yi, ci = lax.axis_index("x"), lax.axis_index("y"), lax.axis_index("c")
    k = 2 * xi + yi

    placed = lax.dynamic_update_slice(jnp.zeros((depth, KCONV, NCHIP, cs), F32), conv_w[:, :, None, :], (0, 0, k, 0))
    placed = jnp.where(ci == 0, placed, 0.0)
    conv_full = _unpack(_all_sum_small(_pack([placed]), "gather_conv_w"), [(depth, KCONV, NCHIP * cs)])[0]

    wi_all, wo_all = _gather_weights([w_in.astype(BF16), w_out.astype(BF16)])
    w_in_full = wi_all.transpose(1, 2, 0, 3).reshape(depth, d, NCHIP * ics)
    w_out_full = wo_all.transpose(1, 0, 2, 3).reshape(depth, 2 * d, d)

    small = dict(norm_w=norm_w, conv_b=conv_b, dt_bias=dt_bias, a_log=a_log, d_skip=d_skip, ssd_norm_w=ssd_norm_w, final_norm_w=final_norm_w)
    loss_local, gx, g = _local_step(x[0], loss_target[0], w_in_full, w_out_full, conv_full, small)

    g_in = g["w_in"].reshape(depth, d, NCHIP, ics).transpose(2, 0, 1, 3)
    g_out = g["w_out"].reshape(depth, NCHIP, 2 * d // NCHIP, d).transpose(1, 0, 2, 3)
    grad_w_in, grad_w_out = _reduce_scatter([g_in, g_out])

    names = list(SMALL)
    total = _all_sum_small(_pack([g[n] for n in names] + [loss_local]), "sum_small_grads")
    parts = _unpack(total, [g[n].shape for n in names] + [()])
    grads = dict(zip(names, parts[:-1]))
    loss = parts[-1]
    grads["conv_w"] = lax.dynamic_index_in_dim(grads["conv_w"].reshape(depth, KCONV, NCHIP, cs), k, axis=2, keepdims=False)
    grads["w_in"], grads["w_out"] = grad_w_in, grad_w_out

    w = dict(norm_w=norm_w, w_in=w_in, conv_w=conv_w, conv_b=conv_b, dt_bias=dt_bias, a_log=a_log, d_skip=d_skip,
             ssd_norm_w=ssd_norm_w, w_out=w_out, final_norm_w=final_norm_w)
    m = dict(norm_w=m_norm_w, w_in=m_w_in, conv_w=m_conv_w, conv_b=m_conv_b, dt_bias=m_dt_bias, a_log=m_a_log, d_skip=m_d_skip,
             ssd_norm_w=m_ssd_norm_w, w_out=m_w_out, final_norm_w=m_final_norm_w)
    v = dict(norm_w=v_norm_w, w_in=v_w_in, conv_w=v_conv_w, conv_b=v_conv_b, dt_bias=v_dt_bias, a_log=v_a_log, d_skip=v_d_skip,
             ssd_norm_w=v_ssd_norm_w, w_out=v_w_out, final_norm_w=v_final_norm_w)
    delta, new_m, new_v = {}, {}, {}
    for n in ("w_in", "w_out"):
        delta[n], new_m[n], new_v[n] = _adamw(w[n], grads[n], m[n], v[n], f"adamw_{n}")
    shapes = [w[n].shape for n in names]
    packed = [_pack([t[n] for n in names])[None] for t in (w, grads, m, v)]
    for res, out in zip(_adamw(*packed, "adamw_small"), (delta, new_m, new_v)):
        out.update(zip(names, _unpack(res[0], shapes)))

    order = ("norm_w", "w_in", "conv_w", "conv_b", "dt_bias", "a_log", "d_skip", "ssd_norm_w", "w_out", "final_norm_w")
    return (loss, gx[None], *[grads[n] for n in order], *[delta[n] for n in order], *[new_m[n] for n in order], *[new_v[n] for n in order])
```

```python
import functools
import math

import jax
import jax.numpy as jnp
from jax import lax
from jax.experimental import pallas as pl
from jax.experimental.pallas import tpu as pltpu

F32, BF16 = jnp.float32, jnp.bfloat16
EPS = 1e-6
CHUNK = 64
NGROUPS = 4
NSTATE = 128
KCONV = 4
SSD_HD = 64
SSD_HD_SHIFT = SSD_HD.bit_length() - 1
SBA_HD = 128
LANES = 128
VMEM_LIMIT = 56 * 1024 * 1024

ADAM_LR, ADAM_B1, ADAM_B2, ADAM_EPS, ADAM_WD, ADAM_STEP = 0.001, 0.9, 0.999, 1e-08, 0.01, 10

NN = ((1,), (0,))
NT = ((1,), (1,))
TN = ((0,), (0,))
MESH = pl.DeviceIdType.MESH


def _dot(a, b, dims):
    return lax.dot_general(a, b, (dims, ((), ())), preferred_element_type=F32)


def _params(*sem):
    return pltpu.CompilerParams(dimension_semantics=sem, vmem_limit_bytes=VMEM_LIMIT)


def _pick(n, target, mult):
    best = None
    for d in range(mult, min(n, target) + 1, mult):
        if n % d == 0:
            best = d
    return n if best is None else best


def _split3(x):
    x1 = x.astype(BF16)
    r1 = x - x1.astype(F32)
    x2 = r1.astype(BF16)
    x3 = (r1 - x2.astype(F32)).astype(BF16)
    return x1, x2, x3


def _split2(x):
    x1 = x.astype(BF16)
    return x1, (x - x1.astype(F32)).astype(BF16)


def _sigmoid(x):
    return 1.0 / (1.0 + jnp.exp(-x))


def _softplus(x):
    return jnp.maximum(x, 0.0) + jnp.log1p(jnp.exp(-jnp.abs(x)))


def _matmul(a, b, mode, out_dtype, name, add=None, tm=1024, tn=512, tk=2048):
    if mode == "nn":
        (m, k), n = a.shape, b.shape[1]
    elif mode == "nt":
        (m, k), n = a.shape, b.shape[0]
    else:
        (k, m), n = a.shape, b.shape[1]
    tm, tn, tk = _pick(m, tm, LANES), _pick(n, tn, LANES), _pick(k, tk, LANES)
    nk = k // tk
    dims = {"nn": NN, "nt": NT, "tn": TN}[mode]
    a_spec = pl.BlockSpec((tk, tm), lambda i, j, kk: (kk, i)) if mode == "tn" else pl.BlockSpec((tm, tk), lambda i, j, kk: (i, kk))
    b_spec = pl.BlockSpec((tn, tk), lambda i, j, kk: (j, kk)) if mode == "nt" else pl.BlockSpec((tk, tn), lambda i, j, kk: (kk, j))
    o_spec = pl.BlockSpec((tm, tn), lambda i, j, kk: (i, j))
    has_add = add is not None

    def body(*refs):
        a_ref, b_ref = refs[0], refs[1]
        add_ref = refs[2] if has_add else None
        o_ref, acc_ref = refs[-2], refs[-1]
        kk = pl.program_id(2)
        part = _dot(a_ref[...], b_ref[...], dims)

        def finish(total):
            if has_add:
                total = total + add_ref[...].astype(F32)
            o_ref[...] = total.astype(out_dtype)

        if nk == 1:
            finish(part)
        else:
            @pl.when(kk == 0)
            def _():
                acc_ref[...] = part

            @pl.when(jnp.logical_and(kk > 0, kk < nk - 1))
            def _():
                acc_ref[...] += part

            @pl.when(kk == nk - 1)
            def _():
                finish(acc_ref[...] + part)

    in_specs = [a_spec, b_spec] + ([o_spec] if has_add else [])
    args = (a, b) + ((add,) if has_add else ())
    return pl.pallas_call(
        body, name=name, grid=(m // tm, n // tn, nk), in_specs=in_specs, out_specs=o_spec,
        out_shape=jax.ShapeDtypeStruct((m, n), out_dtype),
        scratch_shapes=[pltpu.VMEM((tm, tn), F32)],
        compiler_params=_params("parallel", "parallel", "arbitrary"),
    )(*args)


def _rmsnorm_fwd(x, w, name):
    l, d = x.shape
    tr = _pick(l, 512, 8)

    def body(x_ref, w_ref, h_ref):
        xv = x_ref[...]
        r = lax.rsqrt(jnp.mean(xv * xv, axis=-1, keepdims=True) + EPS)
        h_ref[...] = (xv * r * w_ref[...]).astype(BF16)

    return pl.pallas_call(
        body, name=name, grid=(l // tr,),
        in_specs=[pl.BlockSpec((tr, d), lambda i: (i, 0)), pl.BlockSpec((1, d), lambda i: (0, 0))],
        out_specs=pl.BlockSpec((tr, d), lambda i: (i, 0)),
        out_shape=jax.ShapeDtypeStruct((l, d), BF16), compiler_params=_params("parallel"),
    )(x, w)


def _rmsnorm_bwd(dh, x, w, dres, name):
    l, d = x.shape
    tr = _pick(l, 256, 8)

    def body(dh_ref, x_ref, w_ref, dres_ref, dx_ref, dw_ref):
        xv = x_ref[...]
        r = lax.rsqrt(jnp.mean(xv * xv, axis=-1, keepdims=True) + EPS)
        xh = xv * r
        dhv = dh_ref[...]
        dxh = dhv * w_ref[...]
        dx_ref[...] = dres_ref[...] + r * (dxh - xh * jnp.mean(dxh * xh, axis=-1, keepdims=True))
        part = jnp.sum(dhv * xh, axis=0, keepdims=True)

        @pl.when(pl.program_id(0) == 0)
        def _():
            dw_ref[...] = part

        @pl.when(pl.program_id(0) > 0)
        def _():
            dw_ref[...] += part

    row = pl.BlockSpec((tr, d), lambda i: (i, 0))
    vec = pl.BlockSpec((1, d), lambda i: (0, 0))
    return pl.pallas_call(
        body, name=name, grid=(l // tr,), in_specs=[row, row, vec, row], out_specs=[row, vec],
        out_shape=[jax.ShapeDtypeStruct((l, d), F32), jax.ShapeDtypeStruct((1, d), F32)],
        compiler_params=_params("arbitrary"),
    )(dh, x, w, dres)


def _final_loss(h, w, target, name):
    l, d = h.shape
    tr = _pick(l, 256, 8)

    def body(h_ref, w_ref, t_ref, dh_ref, dw_ref, loss_ref):
        xv = h_ref[...]
        r = lax.rsqrt(jnp.mean(xv * xv, axis=-1, keepdims=True) + EPS)
        xh = xv * r
        err = xh * w_ref[...] - t_ref[...]
        dy = err * (1.0 / d)
        dxh = dy * w_ref[...]
        dh_ref[...] = r * (dxh - xh * jnp.mean(dxh * xh, axis=-1, keepdims=True))
        part = jnp.sum(dy * xh, axis=0, keepdims=True)
        lpart = jnp.zeros((8, LANES), F32) + 0.5 * jnp.sum(jnp.mean(err * err, axis=-1, keepdims=True))

        @pl.when(pl.program_id(0) == 0)
        def _():
            dw_ref[...] = part
            loss_ref[...] = lpart

        @pl.when(pl.program_id(0) > 0)
        def _():
            dw_ref[...] += part
            loss_ref[...] += lpart

    row = pl.BlockSpec((tr, d), lambda i: (i, 0))
    vec = pl.BlockSpec((1, d), lambda i: (0, 0))
    return pl.pallas_call(
        body, name=name, grid=(l // tr,), in_specs=[row, vec, row],
        out_specs=[row, vec, pl.BlockSpec((8, LANES), lambda i: (0, 0))],
        out_shape=[jax.ShapeDtypeStruct((l, d), F32), jax.ShapeDtypeStruct((1, d), F32), jax.ShapeDtypeStruct((8, LANES), F32)],
        compiler_params=_params("arbitrary"),
    )(h, w, target)


def _gatenorm_fwd(y, proj_a, w, name):
    l, d = y.shape
    dg = d // NGROUPS
    tr = _pick(l, 256, 8)

    def body(y_ref, z_ref, w_ref, o_ref):
        for g in range(NGROUPS):
            sl = slice(g * dg, (g + 1) * dg)
            zv = z_ref[:, sl]
            u = y_ref[:, sl] * (zv * _sigmoid(zv))
            r = lax.rsqrt(jnp.mean(u * u, axis=-1, keepdims=True) + EPS)
            o_ref[:, sl] = (u * r * w_ref[:, sl]).astype(BF16)

    row = pl.BlockSpec((tr, d), lambda i: (i, 0))
    return pl.pallas_call(
        body, name=name, grid=(l // tr,), in_specs=[row, row, pl.BlockSpec((1, d), lambda i: (0, 0))],
        out_specs=row, out_shape=jax.ShapeDtypeStruct((l, d), BF16), compiler_params=_params("parallel"),
    )(y, proj_a, w)


def _gatenorm_bwd(dout, y, proj_a, w, name):
    l, d = y.shape
    dg = d // NGROUPS
    tr = _pick(l, 256, 8)

    def body(do_ref, y_ref, z_ref, w_ref, dy_ref, dz_ref, dw_ref):
        first = pl.program_id(0) == 0
        for g in range(NGROUPS):
            sl = slice(g * dg, (g + 1) * dg)
            zv = z_ref[:, sl]
            sg = _sigmoid(zv)
            sz = zv * sg
            yv = y_ref[:, sl]
            u = yv * sz
            r = lax.rsqrt(jnp.mean(u * u, axis=-1, keepdims=True) + EPS)
            nh = u * r
            dov = do_ref[:, sl]
            dn = dov * w_ref[:, sl]
            du = r * (dn - nh * jnp.mean(dn * nh, axis=-1, keepdims=True))
            dy_ref[:, sl] = du * sz
            dz_ref[:, sl] = (du * yv * (sg * (1.0 + zv * (1.0 - sg)))).astype(BF16)
            part = jnp.sum(dov * nh, axis=0, keepdims=True)

            @pl.when(first)
            def _():
                dw_ref[:, sl] = part

            @pl.when(jnp.logical_not(first))
            def _():
                dw_ref[:, sl] += part

    row = pl.BlockSpec((tr, d), lambda i: (i, 0))
    vec = pl.BlockSpec((1, d), lambda i: (0, 0))
    return pl.pallas_call(
        body, name=name, grid=(l // tr,), in_specs=[row, row, row, vec], out_specs=[row, row, vec],
        out_shape=[jax.ShapeDtypeStruct((l, d), F32), jax.ShapeDtypeStruct((l, d), BF16), jax.ShapeDtypeStruct((1, d), F32)],
        compiler_params=_params("arbitrary"),
    )(dout, y, proj_a, w)


def _conv_cols(d):
    return _pick(math.gcd(d, d + 2 * NGROUPS * NSTATE), 512, LANES)


def _conv_fwd(proj_a, conv_w, conv_b, d, name):
    l = proj_a.shape[0]
    cdim = d + 2 * NGROUPS * NSTATE
    cw = _conv_cols(d)
    off = d // cw
    tl = _pick(l, 512, 8)
    hb = tl // 8

    def body(u_ref, up_ref, w_ref, b_ref, acc_ref, xbc_ref, ext):
        i = pl.program_id(1)
        ext[0:8, :] = jnp.where(i > 0, up_ref[...], 0.0)
        ext[8:8 + tl, :] = u_ref[...]
        acc = jnp.zeros((tl, cw), F32) + b_ref[...]
        for j in range(KCONV):
            acc = acc + w_ref[j:j + 1, :] * ext[pl.ds(8 - (KCONV - 1) + j, tl), :]
        acc_ref[...] = acc
        xbc_ref[...] = acc * _sigmoid(acc)

    blk = pl.BlockSpec((tl, cw), lambda c, i: (i, c))
    return pl.pallas_call(
        body, name=name, grid=(cdim // cw, l // tl),
        in_specs=[pl.BlockSpec((tl, cw), lambda c, i: (i, c + off)),
                  pl.BlockSpec((8, cw), lambda c, i: (jnp.maximum(i * hb - 1, 0), c + off)),
                  pl.BlockSpec((KCONV, cw), lambda c, i: (0, c)), pl.BlockSpec((1, cw), lambda c, i: (0, c))],
        out_specs=[blk, blk],
        out_shape=[jax.ShapeDtypeStruct((l, cdim), F32), jax.ShapeDtypeStruct((l, cdim), F32)],
        scratch_shapes=[pltpu.VMEM((tl + 8, cw), F32)], compiler_params=_params("parallel", "parallel"),
    )(proj_a, proj_a, conv_w, conv_b)


def _conv_bwd(dxbc, acc, proj_a, conv_w, d, name):
    l, cdim = acc.shape
    cw = _conv_cols(d)
    off = d // cw
    tl = _pick(l, 512, 8)
    hb = tl // 8
    nb = l // tl

    def dsilu(g, a):
        s = _sigmoid(a)
        return g * (s * (1.0 + a * (1.0 - s)))

    def body(g_ref, gn_ref, a_ref, an_ref, u_ref, up_ref, w_ref, du_ref, dw_ref, ext, dext):
        i = pl.program_id(1)
        da = dsilu(g_ref[...], a_ref[...])
        dext[0:tl, :] = da
        dext[tl:tl + 8, :] = jnp.where(i < nb - 1, dsilu(gn_ref[...], an_ref[...]), 0.0)
        ext[0:8, :] = jnp.where(i > 0, up_ref[...], 0.0)
        ext[8:8 + tl, :] = u_ref[...]
        du = jnp.zeros((tl, cw), F32)
        rows = []
        for j in range(KCONV):
            du = du + w_ref[j:j + 1, :] * dext[pl.ds(KCONV - 1 - j, tl), :]
            rows.append(jnp.sum(da * ext[pl.ds(8 - (KCONV - 1) + j, tl), :], axis=0, keepdims=True))
        rows.append(jnp.sum(da, axis=0, keepdims=True))
        du_ref[...] = du.astype(BF16)

        @pl.when(i == 0)
        def _():
            dw_ref[...] = jnp.zeros_like(dw_ref)

        for j, rv in enumerate(rows):
            dw_ref[j:j + 1, :] += rv

    blk = pl.BlockSpec((tl, cw), lambda c, i: (i, c))
    nxt = pl.BlockSpec((8, cw), lambda c, i: (jnp.minimum((i + 1) * hb, l // 8 - 1), c))
    return pl.pallas_call(
        body, name=name, grid=(cdim // cw, nb),
        in_specs=[blk, nxt, blk, nxt,
                  pl.BlockSpec((tl, cw), lambda c, i: (i, c + off)),
                  pl.BlockSpec((8, cw), lambda c, i: (jnp.maximum(i * hb - 1, 0), c + off)),
                  pl.BlockSpec((KCONV, cw), lambda c, i: (0, c))],
        out_specs=[blk, pl.BlockSpec((8, cw), lambda c, i: (0, c))],
        out_shape=[jax.ShapeDtypeStruct((l, cdim), BF16), jax.ShapeDtypeStruct((8, cdim), F32)],
        scratch_shapes=[pltpu.VMEM((tl + 8, cw), F32), pltpu.VMEM((tl + 8, cw), F32)],
        compiler_params=_params("parallel", "arbitrary"),
    )(dxbc, dxbc, acc, acc, proj_a, proj_a, conv_w)


def _ssd_common(dt_ref, bias_ref, alog_ref):
    li = lax.broadcasted_iota(jnp.int32, (CHUNK, CHUNK), 0)
    si = lax.broadcasted_iota(jnp.int32, (CHUNK, CHUNK), 1)
    tri = si <= li
    dtv = _softplus(dt_ref[...] + bias_ref[0])
    a_neg = -jnp.exp(alog_ref[0])
    da = dtv * a_neg
    cs_col = sum(_dot(tri.astype(BF16), p, NN) for p in _split3(da))
    cs_row = sum(_dot(p, (li <= si).astype(BF16), TN) for p in _split3(da))
    return tri, dtv, a_neg, cs_col, cs_row


def _dot2(x, t16, dims):
    hi, lo = _split2(x)
    return _dot(hi, t16, dims) + _dot(lo, t16, dims)


def _ssd_expand(dtv, cs_col, dsk, pg):
    expm = (lax.shift_right_logical(lax.broadcasted_iota(jnp.int32, (LANES, pg), 1), SSD_HD_SHIFT)
            == lax.broadcasted_iota(jnp.int32, (LANES, pg), 0)).astype(BF16)
    tot = cs_col[CHUNK - 1:CHUNK, :]
    stack = jnp.concatenate([dtv, jnp.exp(cs_col), jnp.exp(tot - cs_col), jnp.broadcast_to(dsk, (CHUNK, LANES)),
                             jnp.broadcast_to(jnp.exp(tot), (CHUNK, LANES))], axis=0)
    ex = _dot2(stack, expm, NN)
    return tuple(ex[CHUNK * a:CHUNK * (a + 1)] for a in range(5))


def _ssd_specs(d, r_heads):
    pg = r_heads * SSD_HD
    nb = d // LANES
    dt_blk = (2 * d + 2 * NGROUPS * NSTATE) // LANES
    x_spec = lambda cmap: pl.BlockSpec((CHUNK, pg), lambda g, c: (cmap(c), g))
    b_spec = lambda cmap: pl.BlockSpec((CHUNK, NSTATE), lambda g, c: (cmap(c), nb + g))
    c_spec = lambda cmap: pl.BlockSpec((CHUNK, NSTATE), lambda g, c: (cmap(c), nb + NGROUPS + g))
    dt_spec = lambda cmap: pl.BlockSpec((CHUNK, LANES), lambda g, c: (cmap(c), dt_blk + g))
    const = pl.BlockSpec((1, 1, LANES), lambda g, c: (g, 0, 0))
    return pg, x_spec, b_spec, c_spec, dt_spec, const


def _ssd_fwd(xbc, proj_a, bias_g, alog_g, dsk_g, d, name):
    l = xbc.shape[0]
    nc = l // CHUNK
    r_heads = d // SSD_HD // NGROUPS
    pg, x_spec, b_spec, c_spec, dt_spec, const = _ssd_specs(d, r_heads)
    ident = lambda c: c

    def body(x_ref, b_ref, c_ref, dt_ref, bias_ref, alog_ref, dsk_ref, y_ref, hs_ref, h_scr):
        @pl.when(pl.program_id(1) == 0)
        def _():
            h_scr[...] = jnp.zeros_like(h_scr)

        tri, dtv, _, cs_col, cs_row = _ssd_common(dt_ref, bias_ref, alog_ref)
        hin = h_scr[...]
        hs_ref[0, 0] = hin
        bm = b_ref[...].astype(BF16)
        cm = c_ref[...].astype(BF16)
        xv = x_ref[...]
        e_dt, e_ecs, e_decs, e_dsk, e_etot = _ssd_expand(dtv, cs_col, dsk_ref[0], pg)
        xd = xv * e_dt
        xd16 = xd.astype(BF16)
        gmat = _dot(cm, bm, NT)
        for r in range(r_heads):
            sl = slice(SSD_HD * r, SSD_HD * (r + 1))
            lm = jnp.exp(jnp.where(tri, cs_col[:, r:r + 1] - cs_row[r:r + 1, :], -jnp.inf))
            y_ref[:, sl] = _dot((gmat * lm).astype(BF16), xd16[:, sl], NN)
        y_ref[...] += e_ecs * _dot(cm, hin.astype(BF16), NN) + e_dsk * xv
        h_scr[...] = hin * e_etot[0:1] + _dot(bm, (xd * e_decs).astype(BF16), TN)

    return pl.pallas_call(
        body, name=name, grid=(NGROUPS, nc),
        in_specs=[x_spec(ident), b_spec(ident), c_spec(ident), dt_spec(ident), const, const, const],
        out_specs=[x_spec(ident), pl.BlockSpec((1, 1, NSTATE, pg), lambda g, c: (c, g, 0, 0))],
        out_shape=[jax.ShapeDtypeStruct((l, d), F32), jax.ShapeDtypeStruct((nc, NGROUPS, NSTATE, pg), F32)],
        scratch_shapes=[pltpu.VMEM((NSTATE, pg), F32)], compiler_params=_params("parallel", "arbitrary"),
    )(xbc, xbc, xbc, proj_a, bias_g, alog_g, dsk_g)


def _ssd_bwd(dy, xbc, proj_a, hs, bias_g, alog_g, dsk_g, d, name):
    l = xbc.shape[0]
    nc = l // CHUNK
    r_heads = d // SSD_HD // NGROUPS
    pg, x_spec, b_spec, c_spec, dt_spec, const = _ssd_specs(d, r_heads)
    rev = lambda c: nc - 1 - c
    cdim = d + 2 * NGROUPS * NSTATE

    def body(dy_ref, x_ref, b_ref, c_ref, dt_ref, hs_ref, bias_ref, alog_ref, dsk_ref,
             dx_ref, db_ref, dc_ref, ddt_ref, sums_ref, dh_scr, p_scr, pt_scr, dxd_scr):
        first = pl.program_id(1) == 0

        @pl.when(first)
        def _():
            dh_scr[...] = jnp.zeros_like(dh_scr)
            sums_ref[...] = jnp.zeros_like(sums_ref)

        tri, dtv, a_neg, cs_col, cs_row = _ssd_common(dt_ref, bias_ref, alog_ref)
        li = lax.broadcasted_iota(jnp.int32, (CHUNK, CHUNK), 0)
        si = lax.broadcasted_iota(jnp.int32, (CHUNK, CHUNK), 1)
        tri_t = li <= si
        lastrow = lax.broadcasted_iota(jnp.int32, (CHUNK, 1), 0) == CHUNK - 1
        indm = (lax.shift_right_logical(lax.broadcasted_iota(jnp.int32, (pg, LANES), 0), SSD_HD_SHIFT)
                == lax.broadcasted_iota(jnp.int32, (pg, LANES), 1)).astype(BF16)
        hin = hs_ref[0, 0]
        dhout = dh_scr[...]
        hin16 = hin.astype(BF16)
        dhout16 = dhout.astype(BF16)
        bm = b_ref[...].astype(BF16)
        cm = c_ref[...].astype(BF16)
        xv = x_ref[...]
        dyv = dy_ref[...]
        e_dt, e_ecs, e_decs, e_dsk, e_etot = _ssd_expand(dtv, cs_col, dsk_ref[0], pg)
        xd = xv * e_dt
        xd16 = xd.astype(BF16)
        dy16 = dyv.astype(BF16)
        dye = dyv * e_ecs
        dye16 = dye.astype(BF16)
        gmat = _dot(cm, bm, NT)
        gmat_t = _dot(bm, cm, NT)
        dg = jnp.zeros((CHUNK, CHUNK), F32)
        dg_t = jnp.zeros((CHUNK, CHUNK), F32)
        for r in range(r_heads):
            sl = slice(SSD_HD * r, SSD_HD * (r + 1))
            col = cs_col[:, r:r + 1]
            row = cs_row[r:r + 1, :]
            lm = jnp.exp(jnp.where(tri, col - row, -jnp.inf))
            lm_t = jnp.exp(jnp.where(tri_t, row - col, -jnp.inf))
            dm = _dot(dy16[:, sl], xd16[:, sl], NT)
            dm_t = _dot(xd16[:, sl], dy16[:, sl], NT)
            dg = dg + dm * lm
            dg_t = dg_t + dm_t * lm_t
            m_t = gmat_t * lm_t
            p_scr[:, sl] = dm * (gmat * lm)
            pt_scr[:, sl] = dm_t * m_t
            dxd_scr[:, sl] = _dot(m_t.astype(BF16), dy16[:, sl], NN)
        yoff = _dot(cm, hin16, NN)
        qall = _dot(bm, dhout16, NN)
        dxd = dxd_scr[...] + qall * e_decs
        hh = jnp.broadcast_to(jnp.sum(dhout * hin, axis=0, keepdims=True), (8, pg))
        red = _dot2(jnp.concatenate([dye * yoff, qall * xd, dxd * xv, dyv * xv, p_scr[...], pt_scr[...], hh], axis=0), indm, NN)
        r_yoff, r_q, r_dt, r_dsk, r_p, r_pt = (red[CHUNK * a:CHUNK * (a + 1)] for a in range(6))
        tot = cs_col[CHUNK - 1:CHUNK, :]
        ddec = r_q * jnp.exp(tot - cs_col)
        dtot = jnp.sum(ddec, axis=0, keepdims=True) + jnp.exp(tot) * red[6 * CHUNK:6 * CHUNK + 1]
        dcs = r_p - r_pt + r_yoff - ddec + jnp.where(lastrow, dtot, 0.0)
        dda = sum(_dot(tri_t.astype(BF16), p, NN) for p in _split3(dcs))
        draw = (r_dt + dda * a_neg) * _sigmoid(dt_ref[...] + bias_ref[0])
        ddt_ref[...] = draw
        sums_ref[0, 0:1, :] += jnp.sum(draw, axis=0, keepdims=True)
        sums_ref[0, 1:2, :] += jnp.sum(dda * dtv, axis=0, keepdims=True)
        sums_ref[0, 2:3, :] += jnp.sum(r_dsk, axis=0, keepdims=True)
        dx_ref[...] = dxd * e_dt + e_dsk * dyv
        dc_ref[...] = _dot(dg.astype(BF16), bm, NN) + _dot(dye16, hin16, NT)
        db_ref[...] = _dot(dg_t.astype(BF16), cm, NN) + _dot((xd * e_decs).astype(BF16), dhout16, NT)
        dh_scr[...] = dhout * e_etot[0:1] + _dot(cm, dye16, TN)

    grp = pl.BlockSpec((CHUNK, LANES), lambda g, c: (rev(c), g))
    return pl.pallas_call(
        body, name=name, grid=(NGROUPS, nc),
        in_specs=[x_spec(rev), x_spec(rev), b_spec(rev), c_spec(rev), dt_spec(rev),
                  pl.BlockSpec((1, 1, NSTATE, pg), lambda g, c: (rev(c), g, 0, 0)), const, const, const],
        out_specs=[x_spec(rev), grp, grp, grp, pl.BlockSpec((1, 8, LANES), lambda g, c: (g, 0, 0))],
        out_shape=[jax.ShapeDtypeStruct((l, d), F32), jax.ShapeDtypeStruct((l, NGROUPS * NSTATE), F32),
                   jax.ShapeDtypeStruct((l, NGROUPS * NSTATE), F32),
                   jax.ShapeDtypeStruct((l, NGROUPS * LANES), F32), jax.ShapeDtypeStruct((NGROUPS, 8, LANES), F32)],
        scratch_shapes=[pltpu.VMEM((NSTATE, pg), F32), pltpu.VMEM((CHUNK, pg), F32), pltpu.VMEM((CHUNK, pg), F32),
                        pltpu.VMEM((CHUNK, pg), F32)],
        compiler_params=_params("parallel", "arbitrary"),
    )(dy, xbc, xbc, xbc, proj_a, hs, bias_g, alog_g, dsk_g)


LOG2E = 1.4426950408889634
SBA_TK = 256
SBA_TQ_FWD = 1024
SBA_TQ_BWD = 512
SBA_ROW_PARTS = 4


def _sba_tiles(l, tq_target):
    tk = _pick(l, SBA_TK, LANES)
    tq = _pick(l, tq_target, tk)
    assert l // tk <= LANES
    return tq, tk


def _sba_scores(q, kb, valid, scale):
    z2 = _dot(q, kb, NT) * (scale * LOG2E)
    t2 = jnp.log2(1.0 + jnp.exp2(-jnp.abs(z2)))
    la = jnp.minimum(z2, 0.0) - t2
    lk = la - z2
    if valid is not None:
        lk = jnp.where(valid, lk, 0.0)
    return lk, la


def _sba_fwd(proj_a, proj_c, d, g_off, name):
    l = proj_c.shape[0]
    nh = d // SBA_HD
    tq, tk = _sba_tiles(l, SBA_TQ_FWD)
    band = tq // tk
    scale = 1.0 / math.sqrt(SBA_HD)
    rq = tq // SBA_ROW_PARTS if tq % (SBA_ROW_PARTS * 16) == 0 else tq
    parts = [pl.ds(p * rq, rq) for p in range(tq // rq)]

    def body(q_ref, k_ref, v_ref, g_ref, o_ref, y_ref, rs_ref, rs_scr):
        i = pl.program_id(1)
        ki = lax.broadcasted_iota(jnp.int32, (tk, tk), 0)
        kj = lax.broadcasted_iota(jnp.int32, (tk, tk), 1)
        uex = (ki > kj).astype(BF16)
        lane = lax.broadcasted_iota(jnp.int32, (rq, LANES), 1)
        rs_scr[...] = jnp.zeros_like(rs_scr)
        qs = [q_ref[ps, :] for ps in parts]

        def tile(j, carry, masked):
            start = pl.multiple_of(j * tk, tk)
            kb = k_ref[pl.ds(start, tk), :]
            vb = v_ref[pl.ds(start, tk), :]
            valid = [None] * len(parts)
            if masked:
                rows = lax.broadcasted_iota(jnp.int32, (rq, tk), 0)
                cols = lax.broadcasted_iota(jnp.int32, (rq, tk), 1)
                valid = [cols - rows < i * tq + p * rq - j * tk for p in range(len(parts))]
            sc = [_sba_scores(qs[p], kb, valid[p], scale) for p in range(len(parts))]
            later = [_dot2(lk, uex, NN) for lk, _ in sc]
            out = []
            for p, ps in enumerate(parts):
                rsum, acc = carry[p]
                w = jnp.exp2(sc[p][1] + later[p] + rsum)
                if masked:
                    w = jnp.where(valid[p], w, 0.0)
                acc = acc + _dot(w.astype(BF16), vb, NN)
                rs_scr[ps, :] = jnp.where(lane == j, rsum, rs_scr[ps, :])
                out.append((rsum + later[p][:, 0:1] + sc[p][0][:, 0:1], acc))
            return tuple(out)

        carry = tuple((jnp.zeros((rq, 1), F32), jnp.zeros((rq, SBA_HD), F32)) for _ in parts)
        top = (i + 1) * band - 1
        carry = lax.fori_loop(0, band, lambda jj, c: tile(top - jj, c, True), carry)
        carry = lax.fori_loop(0, i * band, lambda jj, c: tile(i * band - 1 - jj, c, False), carry)
        rs_ref[...] = rs_scr[...]
        for p, ps in enumerate(parts):
            acc = carry[p][1]
            o_ref[ps, :] = acc
            gv = g_ref[ps, :]
            y_ref[ps, :] = (acc * (gv * _sigmoid(gv))).astype(BF16)

    goff = g_off // SBA_HD
    blk = lambda off: pl.BlockSpec((tq, SBA_HD), lambda h, i: (i, h + off))
    full = lambda off: pl.BlockSpec((l, SBA_HD), lambda h, i: (0, h + off))
    out = pl.BlockSpec((tq, SBA_HD), lambda h, i: (i, h))
    return pl.pallas_call(
        body, name=name, grid=(nh, l // tq), in_specs=[blk(0), full(nh), full(2 * nh), blk(goff)], out_specs=[out, out, out],
        scratch_shapes=[pltpu.VMEM((tq, LANES), F32)],
        out_shape=[jax.ShapeDtypeStruct((l, d), F32), jax.ShapeDtypeStruct((l, d), BF16), jax.ShapeDtypeStruct((l, d), F32)],
        compiler_params=_params("parallel", "arbitrary"),
    )(proj_c, proj_c, proj_c, proj_a)


def _sba_bwd(dys, o, rs, proj_a, proj_c, d, g_off, name):
    l = proj_c.shape[0]
    nh = d // SBA_HD
    tq, tk = _sba_tiles(l, SBA_TQ_BWD)
    band = tq // tk
    scale = 1.0 / math.sqrt(SBA_HD)

    def body(dy_ref, o_ref, rs_ref, q_ref, k_ref, v_ref, g_ref, dq_ref, dk_ref, dv_ref, dg_ref, dk_acc, dv_acc):
        i = pl.program_id(1)
        nq = pl.num_programs(1)

        @pl.when(i == 0)
        def _():
            dk_acc[...] = jnp.zeros_like(dk_acc)
            dv_acc[...] = jnp.zeros_like(dv_acc)

        ki = lax.broadcasted_iota(jnp.int32, (tk, tk), 0)
        kj = lax.broadcasted_iota(jnp.int32, (tk, tk), 1)
        uex = (ki > kj).astype(BF16)
        ulow = (ki < kj).astype(BF16)
        lane = lax.broadcasted_iota(jnp.int32, (tq, LANES), 1)
        q = q_ref[...]
        gv = g_ref[...]
        sg = _sigmoid(gv)
        dyv = dy_ref[...]
        dg_ref[...] = (dyv * o_ref[...] * (sg * (1.0 + gv * (1.0 - sg)))).astype(BF16)
        do16 = (dyv * (gv * sg)).astype(BF16)

        def tile(j, carry, masked):
            epre, dq = carry
            start = pl.multiple_of(j * tk, tk)
            kb = k_ref[pl.ds(start, tk), :]
            vb = v_ref[pl.ds(start, tk), :]
            rsum = jnp.sum(jnp.where(lane == j, rs_ref[...], 0.0), axis=1, keepdims=True)
            valid = None
            if masked:
                rows = lax.broadcasted_iota(jnp.int32, (tq, tk), 0)
                cols = lax.broadcasted_iota(jnp.int32, (tq, tk), 1)
                valid = cols - rows < i * tq - j * tk
            lk, la = _sba_scores(q, kb, valid, scale)
            hi, lo = _split2(lk)
            later = _dot(hi, uex, NN) + _dot(lo, uex, NN)
            w = jnp.exp2(la + later + rsum)
            if masked:
                w = jnp.where(valid, w, 0.0)
            e = w * _dot(do16, vb, NT)
            ehi, elo = _split2(e)
            epx = _dot(ehi, ulow, NN) + _dot(elo, ulow, NN)
            dz = (e - jnp.exp2(la) * (e + epre + epx)) * scale
            if masked:
                dz = jnp.where(valid, dz, 0.0)
            dz16 = dz.astype(BF16)
            dq = dq + _dot(dz16, kb, NN)
            dk_acc[pl.ds(start, tk), :] += _dot(dz16, q, TN)
            dv_acc[pl.ds(start, tk), :] += _dot(w.astype(BF16), do16, TN)
            return epre + epx[:, tk - 1:tk] + e[:, tk - 1:tk], dq

        carry = (jnp.zeros((tq, 1), F32), jnp.zeros((tq, SBA_HD), F32))
        carry = lax.fori_loop(0, i * band, lambda j, c: tile(j, c, False), carry)
        carry = lax.fori_loop(0, band, lambda jj, c: tile(i * band + jj, c, True), carry)
        dq_ref[...] = carry[1].astype(BF16)

        @pl.when(i == nq - 1)
        def _():
            dk_ref[...] = dk_acc[...].astype(BF16)
            dv_ref[...] = dv_acc[...].astype(BF16)

    goff = g_off // SBA_HD
    blk = lambda off: pl.BlockSpec((tq, SBA_HD), lambda h, i: (i, h + off))
    full = lambda off: pl.BlockSpec((l, SBA_HD), lambda h, i: (0, h + off))
    out = pl.BlockSpec((tq, SBA_HD), lambda h, i: (i, h))
    outfull = pl.BlockSpec((l, SBA_HD), lambda h, i: (0, h))
    sd = jax.ShapeDtypeStruct((l, d), BF16)
    return pl.pallas_call(
        body, name=name, grid=(nh, l // tq),
        in_specs=[blk(nh), out, out, blk(0), full(nh), full(2 * nh), blk(goff)],
        out_specs=[out, outfull, outfull, out], out_shape=[sd, sd, sd, sd],
        scratch_shapes=[pltpu.VMEM((l, SBA_HD), F32), pltpu.VMEM((l, SBA_HD), F32)],
        compiler_params=_params("parallel", "arbitrary"),
    )(dys, o, rs, proj_c, proj_c, proj_c, proj_a)


def _adamw_math(w, g, m, v):
    m = ADAM_B1 * m + (1.0 - ADAM_B1) * g
    v = ADAM_B2 * v + (1.0 - ADAM_B2) * (g * g)
    m_hat = m / (1.0 - ADAM_B1 ** ADAM_STEP)
    v_hat = v / (1.0 - ADAM_B2 ** ADAM_STEP)
    delta = -ADAM_LR * (m_hat / (jnp.sqrt(v_hat) + ADAM_EPS) + ADAM_WD * w)
    return delta, m, v


def _adamw(w, g, m, v, name):
    a, r, c = w.shape
    tr = _pick(r, max(8, (1 << 19) // c // 8 * 8), 8)

    def body(w_ref, g_ref, m_ref, v_ref, d_ref, nm_ref, nv_ref):
        dl, nm, nv = _adamw_math(w_ref[...], g_ref[...], m_ref[...], v_ref[...])
        d_ref[...] = dl
        nm_ref[...] = nm
        nv_ref[...] = nv

    blk = pl.BlockSpec((1, tr, c), lambda i, j: (i, j, 0))
    sd = jax.ShapeDtypeStruct(w.shape, F32)
    return pl.pallas_call(
        body, name=name, grid=(a, r // tr), in_specs=[blk] * 4, out_specs=[blk] * 3, out_shape=[sd] * 3,
        compiler_params=_params("parallel", "parallel"),
    )(w, g, m, v)


def _dims(d):
    cdim = d + 2 * NGROUPS * NSTATE
    heads = d // SSD_HD
    r_heads = heads // NGROUPS
    g_off = d + cdim + NGROUPS * LANES
    na = g_off + d
    nc = 3 * d
    return cdim, heads, r_heads, na, nc, g_off


def _pack_w_in(w_in, d):
    cdim, heads, r_heads, na, nc, g_off = _dims(d)
    o = d + cdim
    w_dt = w_in[:, o:o + heads].reshape(d, NGROUPS, r_heads)
    w_dt = jnp.pad(w_dt, ((0, 0), (0, 0), (0, LANES - r_heads))).reshape(d, NGROUPS * LANES)
    return jnp.concatenate([w_in[:, :o], w_dt, w_in[:, o + heads + nc:]], axis=1), w_in[:, o + heads:o + heads + nc]


def _unpack_w_in(ga, gc, d):
    cdim, heads, r_heads, na, nc, g_off = _dims(d)
    o = d + cdim
    g_dt = ga[:, o:g_off].reshape(d, NGROUPS, LANES)[:, :, :r_heads].reshape(d, heads)
    return jnp.concatenate([ga[:, :o], g_dt, gc, ga[:, g_off:]], axis=1)


def _group_vec(v, r_heads):
    return jnp.pad(v.reshape(NGROUPS, 1, r_heads), ((0, 0), (0, 0), (0, LANES - r_heads)))


def _layer_fwd(x, p, d, tag):
    cdim, heads, r_heads, na, nc, g_off = _dims(d)
    h = _rmsnorm_fwd(x, p["norm_w"], f"norm_f{tag}")
    proj_a = _matmul(h, p["wa"], "nn", F32, f"inproj_a{tag}")
    proj_c = _matmul(h, p["wc"], "nn", BF16, f"inproj_c{tag}")
    acc, xbc = _conv_fwd(proj_a, p["conv_w"], p["conv_b"], d, f"conv_f{tag}")
    y, hs = _ssd_fwd(xbc, proj_a, p["bias_g"], p["alog_g"], p["dsk_g"], d, f"ssd_f{tag}")
    y_ssd = _gatenorm_fwd(y, proj_a, p["ssd_norm_w"], f"gate_f{tag}")
    o, y_sba, rs = _sba_fwd(proj_a, proj_c, d, g_off, f"sba_f{tag}")
    mix = jnp.concatenate([y_ssd, y_sba], axis=1)
    x_next = _matmul(mix, p["w_out"], "nn", F32, f"outproj{tag}", add=x)
    return x_next, dict(x=x, h=h, proj_a=proj_a, proj_c=proj_c, acc=acc, xbc=xbc, y=y, hs=hs, o=o, rs=rs, mix=mix)


def _layer_bwd(dxn, s, p, d, tag):
    cdim, heads, r_heads, na, nc, g_off = _dims(d)
    dxn16 = dxn.astype(BF16)
    dmix = _matmul(dxn16, p["w_out"], "nt", F32, f"dmix{tag}")
    g_w_out = _matmul(s["mix"], dxn16, "tn", F32, f"dwout{tag}")
    dq, dk, dv, dg = _sba_bwd(dmix, s["o"], s["rs"], s["proj_a"], s["proj_c"], d, g_off, f"sba_b{tag}")
    dy, dz, g_ssd_norm = _gatenorm_bwd(dmix, s["y"], s["proj_a"], p["ssd_norm_w"], f"gate_b{tag}")
    dx_s, db_s, dc_s, ddt, sums = _ssd_bwd(dy, s["xbc"], s["proj_a"], s["hs"], p["bias_g"], p["alog_g"], p["dsk_g"], d, f"ssd_b{tag}")
    dxbc = jnp.concatenate([dx_s, db_s, dc_s], axis=1)
    du, g_conv = _conv_bwd(dxbc, s["acc"], s["proj_a"], p["conv_w"], d, f"conv_b{tag}")
    dproj_a = jnp.concatenate([dz, du, ddt.astype(BF16), dg], axis=1)
    dproj_c = jnp.concatenate([dq, dk, dv], axis=1)
    g_wa = _matmul(s["h"], dproj_a, "tn", F32, f"dwin_a{tag}")
    g_wc = _matmul(s["h"], dproj_c, "tn", F32, f"dwin_c{tag}")
    dh = _matmul(dproj_a, p["wa"], "nt", F32, f"dh_a{tag}")
    dh = _matmul(dproj_c, p["wc"], "nt", F32, f"dh_c{tag}", add=dh)
    dx, g_norm = _rmsnorm_bwd(dh, s["x"], p["norm_w"], dxn, f"norm_b{tag}")
    a_neg = -jnp.exp(p["alog_g"][:, 0, :r_heads].reshape(heads))
    grads = dict(
        norm_w=g_norm[0], w_in=_unpack_w_in(g_wa, g_wc, d), conv_w=g_conv[:KCONV], conv_b=g_conv[KCONV],
        dt_bias=sums[:, 0, :r_heads].reshape(heads), a_log=sums[:, 1, :r_heads].reshape(heads) * a_neg,
        d_skip=sums[:, 2, :r_heads].reshape(heads), ssd_norm_w=g_ssd_norm[0], w_out=g_w_out)
    return dx, grads


def _local_step(x, target, w_in16, w_out16, conv_w, small):
    l, d = x.shape
    depth = w_in16.shape[0]
    r_heads = _dims(d)[2]
    layers = []
    for i in range(depth):
        wa, wc = _pack_w_in(w_in16[i], d)
        layers.append(dict(
            norm_w=small["norm_w"][i][None], wa=wa, wc=wc, conv_w=conv_w[i], conv_b=small["conv_b"][i][None],
            bias_g=_group_vec(small["dt_bias"][i], r_heads), alog_g=_group_vec(small["a_log"][i], r_heads),
            dsk_g=_group_vec(small["d_skip"][i], r_heads), ssd_norm_w=small["ssd_norm_w"][i][None], w_out=w_out16[i]))
    saved = []
    hcur = x
    for i in range(depth):
        hcur, s = _layer_fwd(hcur, layers[i], d, str(i))
        saved.append(s)
    dh, g_final, loss = _final_loss(hcur, small["final_norm_w"][None], target, "final_loss")
    grads = [None] * depth
    for i in reversed(range(depth)):
        dh, grads[i] = _layer_bwd(dh, saved[i], layers[i], d, str(i))
    stacked = {k: jnp.stack([g[k] for g in grads]) for k in grads[0]}
    stacked["final_norm_w"] = g_final[0]
    return loss[0, 0], dh, stacked


HBM = pl.BlockSpec(memory_space=pl.ANY)
NCHIP = 4
NDEV = 8


def _mesh_pos():
    x, y, c = lax.axis_index("x"), lax.axis_index("y"), lax.axis_index("c")
    chips = [(1 - x, y), (x, 1 - y), (1 - x, 1 - y)]
    return x, y, c, chips


def _remote(src, dst, send_sem, recv_sem, dev):
    return pltpu.make_async_remote_copy(src_ref=src, dst_ref=dst, send_sem=send_sem, recv_sem=recv_sem,
                                        device_id=dev, device_id_type=MESH)


def _gather_weights(shards):
    n = len(shards)
    hl = shards[0].shape[0] // 2

    def body(*refs):
        ins, outs = refs[:n], refs[n:2 * n]
        send, recv, loc = refs[2 * n:]
        x, y, c, chips = _mesh_pos()
        k = 2 * x + y
        half = pl.ds(c * hl, hl)
        other = pl.ds((1 - c) * hl, hl)
        copies = []
        for a in range(n):
            own = pltpu.make_async_copy(ins[a], outs[a].at[k], loc.at[a])
            own.start()
            copies.append(own)
            for j, (px, py) in enumerate(chips):
                _remote(ins[a].at[half], outs[a].at[k, half], send.at[a, j], recv.at[a, j], (px, py, c)).start()
        for a in range(n):
            for j, (px, py) in enumerate(chips):
                kj = 2 * px + py
                got = outs[a].at[kj, half]
                _remote(got, got, send.at[a, j], recv.at[a, j], (px, py, c)).wait_recv()
                _remote(got, got, send.at[a, 3 + j], recv.at[a, 3 + j], (x, y, 1 - c)).start()
        for a in range(n):
            for j, (px, py) in enumerate(chips):
                kj = 2 * px + py
                _remote(outs[a].at[kj, other], outs[a].at[kj, other], send.at[a, 3 + j], recv.at[a, 3 + j], (x, y, 1 - c)).wait_recv()
            for j, (px, py) in enumerate(chips):
                kj = 2 * px + py
                _remote(ins[a].at[half], outs[a].at[k, half], send.at[a, j], recv.at[a, j], (px, py, c)).wait_send()
                _remote(outs[a].at[kj, half], outs[a].at[kj, half], send.at[a, 3 + j], recv.at[a, 3 + j], (x, y, 1 - c)).wait_send()
        for own in copies:
            own.wait()

    return pl.pallas_call(
        body, name="gather_weights", in_specs=[HBM] * n, out_specs=[HBM] * n,
        out_shape=[jax.ShapeDtypeStruct((NCHIP,) + s.shape, s.dtype) for s in shards],
        scratch_shapes=[pltpu.SemaphoreType.DMA((n, 6)), pltpu.SemaphoreType.DMA((n, 6)), pltpu.SemaphoreType.DMA((n,))],
    )(*shards)


def _swap_halves(parts):
    n = len(parts)
    hl = parts[0].shape[1] // 2

    def body(*refs):
        ins, outs = refs[:n], refs[n:2 * n]
        send, recv = refs[2 * n:]
        x, y, c, _ = _mesh_pos()
        cps = [_remote(ins[a].at[:, pl.ds((1 - c) * hl, hl)], outs[a], send.at[a], recv.at[a], (x, y, 1 - c)) for a in range(n)]
        for cp in cps:
            cp.start()
        for cp in cps:
            cp.wait()

    return pl.pallas_call(
        body, name="grad_swap_halves", in_specs=[HBM] * n, out_specs=[HBM] * n,
        out_shape=[jax.ShapeDtypeStruct((NCHIP, hl) + p.shape[2:], p.dtype) for p in parts],
        scratch_shapes=[pltpu.SemaphoreType.DMA((n,)), pltpu.SemaphoreType.DMA((n,))],
    )(*parts)


def _exchange_shards(parts):
    n = len(parts)

    def body(*refs):
        ins, outs = refs[:n], refs[n:2 * n]
        send, recv, loc = refs[2 * n:]
        x, y, c, chips = _mesh_pos()
        k = 2 * x + y
        work = []
        for a in range(n):
            own = pltpu.make_async_copy(ins[a].at[k], outs[a].at[k], loc.at[a])
            own.start()
            work.append(own)
            for j, (px, py) in enumerate(chips):
                cp = _remote(ins[a].at[2 * px + py], outs[a].at[k], send.at[a, j], recv.at[a, j], (px, py, c))
                cp.start()
                work.append(cp)
        for w in work:
            w.wait()

    return pl.pallas_call(
        body, name="grad_exchange", in_specs=[HBM] * n, out_specs=[HBM] * n,
        out_shape=[jax.ShapeDtypeStruct(p.shape, p.dtype) for p in parts],
        scratch_shapes=[pltpu.SemaphoreType.DMA((n, 3)), pltpu.SemaphoreType.DMA((n, 3)), pltpu.SemaphoreType.DMA((n,))],
    )(*parts)


def _join_halves(halves):
    n = len(halves)
    hl = halves[0].shape[0]

    def body(*refs):
        ins, outs = refs[:n], refs[n:2 * n]
        send, recv, loc = refs[2 * n:]
        x, y, c, _ = _mesh_pos()
        mine = pl.ds(c * hl, hl)
        work = []
        for a in range(n):
            own = pltpu.make_async_copy(ins[a], outs[a].at[mine], loc.at[a])
            own.start()
            work.append(own)
            cp = _remote(ins[a], outs[a].at[mine], send.at[a], recv.at[a], (x, y, 1 - c))
            cp.start()
            work.append(cp)
        for w in work:
            w.wait()

    return pl.pallas_call(
        body, name="grad_join_halves", in_specs=[HBM] * n, out_specs=[HBM] * n,
        out_shape=[jax.ShapeDtypeStruct((2 * hl,) + h.shape[1:], h.dtype) for h in halves],
        scratch_shapes=[pltpu.SemaphoreType.DMA((n,)), pltpu.SemaphoreType.DMA((n,)), pltpu.SemaphoreType.DMA((n,))],
    )(*halves)


def _all_sum_small(vec, name):
    r = vec.shape[0]

    def body(v_ref, o_ref, buf, send, recv):
        x, y, c, _ = _mesh_pos()
        me = 4 * x + 2 * y + c
        buf[me] = v_ref[...]
        cps = []
        for mask in range(1, NDEV):
            fx, fy, fc = (mask >> 2) & 1, (mask >> 1) & 1, mask & 1
            peer = (1 - x if fx else x, 1 - y if fy else y, 1 - c if fc else c)
            cp = _remote(v_ref, buf.at[me], send.at[mask - 1], recv.at[mask - 1], peer)
            cp.start()
            cps.append(cp)
        for cp in cps:
            cp.wait()
        total = buf[0]
        for dev in range(1, NDEV):
            total = total + buf[dev]
        o_ref[...] = total

    vm = pl.BlockSpec(memory_space=pltpu.VMEM)
    return pl.pallas_call(
        body, name=name, in_specs=[vm], out_specs=vm, out_shape=jax.ShapeDtypeStruct((r, LANES), F32),
        scratch_shapes=[pltpu.VMEM((NDEV, r, LANES), F32), pltpu.SemaphoreType.DMA((NDEV - 1,)), pltpu.SemaphoreType.DMA((NDEV - 1,))],
    )(vec)


def _add_pairs(a, b, out_dtype, name):
    n0, n1, r, c = a.shape
    tr = _pick(r, max(8, (1 << 19) // c // 8 * 8), 8)

    def body(a_ref, b_ref, o_ref):
        o_ref[...] = (a_ref[...] + b_ref[...]).astype(out_dtype)

    blk = pl.BlockSpec((1, 1, tr, c), lambda i, j, t: (i, j, t, 0))
    return pl.pallas_call(
        body, name=name, grid=(n0, n1, r // tr), in_specs=[blk, blk], out_specs=blk,
        out_shape=jax.ShapeDtypeStruct(a.shape, out_dtype), compiler_params=_params("parallel", "parallel", "parallel"),
    )(a, b)


def _sum_chips(p, name):
    _, hl, r, c = p.shape
    tr = _pick(r, max(16, (1 << 19) // c // 16 * 16), 16)

    def body(p_ref, o_ref):
        total = p_ref[0].astype(F32)
        for j in range(1, NCHIP):
            total = total + p_ref[j].astype(F32)
        o_ref[...] = total

    return pl.pallas_call(
        body, name=name, grid=(hl, r // tr),
        in_specs=[pl.BlockSpec((NCHIP, 1, tr, c), lambda i, t: (0, i, t, 0))],
        out_specs=pl.BlockSpec((1, tr, c), lambda i, t: (i, t, 0)),
        out_shape=jax.ShapeDtypeStruct((hl, r, c), F32), compiler_params=_params("parallel", "parallel"),
    )(p)


def _reduce_scatter(parts):
    c = lax.axis_index("c")
    hl = parts[0].shape[1] // 2
    theirs = _swap_halves(parts)
    mine = [lax.dynamic_slice_in_dim(p, c * hl, hl, axis=1) for p in parts]
    chip_sum = [_add_pairs(m, t, BF16, f"grad_pair_sum{i}") for i, (m, t) in enumerate(zip(mine, theirs))]
    gathered = _exchange_shards(chip_sum)
    halves = [_sum_chips(g, f"grad_chip_sum{i}") for i, g in enumerate(gathered)]
    return _join_halves(halves)


SMALL = ("norm_w", "conv_w", "conv_b", "dt_bias", "a_log", "d_skip", "ssd_norm_w", "final_norm_w")


def _pack(arrays):
    flat = jnp.concatenate([a.reshape(-1).astype(F32) for a in arrays])
    rows = -(-flat.shape[0] // (8 * LANES)) * 8
    return jnp.pad(flat, (0, rows * LANES - flat.shape[0])).reshape(rows, LANES)


def _unpack(vec, shapes):
    flat = vec.reshape(-1)
    out, pos = [], 0
    for s in shapes:
        n = math.prod(s)
        out.append(flat[pos:pos + n].reshape(s))
        pos += n
    return out


def kernel(x, norm_w, w_in, conv_w, conv_b, dt_bias, a_log, d_skip, ssd_norm_w, w_out, final_norm_w, loss_target, m_norm_w, m_w_in, m_conv_w, m_conv_b, m_dt_bias, m_a_log, m_d_skip, m_ssd_norm_w, m_w_out, m_final_norm_w, v_norm_w, v_w_in, v_conv_w, v_conv_b, v_dt_bias, v_a_log, v_d_skip, v_ssd_norm_w, v_w_out, v_final_norm_w):
    depth, d, ics = w_in.shape
    cs = conv_w.shape[2]
    xi, yi, ci = lax.axis_index("x"), lax.axis_index("y"), lax.axis_index("c")
    k = 2 * xi + yi

    placed = lax.dynamic_update_slice(jnp.zeros((depth, KCONV, NCHIP, cs), F32), conv_w[:, :, None, :], (0, 0, k, 0))
    placed = jnp.where(ci == 0, placed, 0.0)
    conv_full = _unpack(_all_sum_small(_pack([placed]), "gather_conv_w"), [(depth, KCONV, NCHIP * cs)])[0]

    wi_all, wo_all = _gather_weights([w_in.astype(BF16), w_out.astype(BF16)])
    w_in_full = wi_all.transpose(1, 2, 0, 3).reshape(depth, d, NCHIP * ics)
    w_out_full = wo_all.transpose(1, 0, 2, 3).reshape(depth, 2 * d, d)

    small = dict(norm_w=norm_w, conv_b=conv_b, dt_bias=dt_bias, a_log=a_log, d_skip=d_skip, ssd_norm_w=ssd_norm_w, final_norm_w=final_norm_w)
    loss_local, gx, g = _local_step(x[0], loss_target[0], w_in_full, w_out_full, conv_full, small)

    g_in = g["w_in"].reshape(depth, d, NCHIP, ics).transpose(2, 0, 1, 3)
    g_out = g["w_out"].reshape(depth, NCHIP, 2 * d // NCHIP, d).transpose(1, 0, 2, 3)
    grad_w_in, grad_w_out = _reduce_scatter([g_in, g_out])

    names = list(SMALL)
    total = _all_sum_small(_pack([g[n] for n in names] + [loss_local]), "sum_small_grads")
    parts = _unpack(total, [g[n].shape for n in names] + [()])
    grads = dict(zip(names, parts[:-1]))
    loss = parts[-1]
    grads["conv_w"] = lax.dynamic_index_in_dim(grads["conv_w"].reshape(depth, KCONV, NCHIP, cs), k, axis=2, keepdims=False)
    grads["w_in"], grads["w_out"] = grad_w_in, grad_w_out

    w = dict(norm_w=norm_w, w_in=w_in, conv_w=conv_w, conv_b=conv_b, dt_bias=dt_bias, a_log=a_log, d_skip=d_skip,
             ssd_norm_w=ssd_norm_w, w_out=w_out, final_norm_w=final_norm_w)
    m = dict(norm_w=m_norm_w, w_in=m_w_in, conv_w=m_conv_w, conv_b=m_conv_b, dt_bias=m_dt_bias, a_log=m_a_log, d_skip=m_d_skip,
             ssd_norm_w=m_ssd_norm_w, w_out=m_w_out, final_norm_w=m_final_norm_w)
    v = dict(norm_w=v_norm_w, w_in=v_w_in, conv_w=v_conv_w, conv_b=v_conv_b, dt_bias=v_dt_bias, a_log=v_a_log, d_skip=v_d_skip,
             ssd_norm_w=v_ssd_norm_w, w_out=v_w_out, final_norm_w=v_final_norm_w)
    delta, new_m, new_v = {}, {}, {}
    for n in ("w_in", "w_out"):
        delta[n], new_m[n], new_v[n] = _adamw(w[n], grads[n], m[n], v[n], f"adamw_{n}")
    shapes = [w[n].shape for n in names]
    packed = [_pack([t[n] for n in names])[None] for t in (w, grads, m, v)]
    for res, out in zip(_adamw(*packed, "adamw_small"), (delta, new_m, new_v)):
        out.update(zip(names, _unpack(res[0], shapes)))

    order = ("norm_w", "w_in", "conv_w", "conv_b", "dt_bias", "a_log", "d_skip", "ssd_norm_w", "w_out", "final_norm_w")
    return (loss, gx[None], *[grads[n] for n in order], *[delta[n] for n in order], *[new_m[n] for n in order], *[new_v[n] for n in order])
```

```python
import functools
import math

import jax
import jax.numpy as jnp
from jax import lax
from jax.experimental import pallas as pl
from jax.experimental.pallas import tpu as pltpu

F32, BF16 = jnp.float32, jnp.bfloat16
EPS = 1e-6
CHUNK = 64
NGROUPS = 4
NSTATE = 128
KCONV = 4
SSD_HD = 64
SSD_HD_SHIFT = SSD_HD.bit_length() - 1
SBA_HD = 128
LANES = 128
VMEM_LIMIT = 56 * 1024 * 1024

ADAM_LR, ADAM_B1, ADAM_B2, ADAM_EPS, ADAM_WD, ADAM_STEP = 0.001, 0.9, 0.999, 1e-08, 0.01, 10

NN = ((1,), (0,))
NT = ((1,), (1,))
TN = ((0,), (0,))
MESH = pl.DeviceIdType.MESH


def _dot(a, b, dims):
    return lax.dot_general(a, b, (dims, ((), ())), preferred_element_type=F32)


def _params(*sem):
    return pltpu.CompilerParams(dimension_semantics=sem, vmem_limit_bytes=VMEM_LIMIT)


def _pick(n, target, mult):
    best = None
    for d in range(mult, min(n, target) + 1, mult):
        if n % d == 0:
            best = d
    return n if best is None else best


def _split3(x):
    x1 = x.astype(BF16)
    r1 = x - x1.astype(F32)
    x2 = r1.astype(BF16)
    x3 = (r1 - x2.astype(F32)).astype(BF16)
    return x1, x2, x3


def _split2(x):
    x1 = x.astype(BF16)
    return x1, (x - x1.astype(F32)).astype(BF16)


def _sigmoid(x):
    return 1.0 / (1.0 + jnp.exp(-x))


def _softplus(x):
    return jnp.maximum(x, 0.0) + jnp.log1p(jnp.exp(-jnp.abs(x)))


def _matmul(a, b, mode, out_dtype, name, add=None, tm=1024, tn=512, tk=2048):
    if mode == "nn":
        (m, k), n = a.shape, b.shape[1]
    elif mode == "nt":
        (m, k), n = a.shape, b.shape[0]
    else:
        (k, m), n = a.shape, b.shape[1]
    tm, tn, tk = _pick(m, tm, LANES), _pick(n, tn, LANES), _pick(k, tk, LANES)
    nk = k // tk
    dims = {"nn": NN, "nt": NT, "tn": TN}[mode]
    a_spec = pl.BlockSpec((tk, tm), lambda i, j, kk: (kk, i)) if mode == "tn" else pl.BlockSpec((tm, tk), lambda i, j, kk: (i, kk))
    b_spec = pl.BlockSpec((tn, tk), lambda i, j, kk: (j, kk)) if mode == "nt" else pl.BlockSpec((tk, tn), lambda i, j, kk: (kk, j))
    o_spec = pl.BlockSpec((tm, tn), lambda i, j, kk: (i, j))
    has_add = add is not None

    def body(*refs):
        a_ref, b_ref = refs[0], refs[1]
        add_ref = refs[2] if has_add else None
        o_ref, acc_ref = refs[-2], refs[-1]
        kk = pl.program_id(2)
        part = _dot(a_ref[...], b_ref[...], dims)

        def finish(total):
            if has_add:
                total = total + add_ref[...].astype(F32)
            o_ref[...] = total.astype(out_dtype)

        if nk == 1:
            finish(part)
        else:
            @pl.when(kk == 0)
            def _():
                acc_ref[...] = part

            @pl.when(jnp.logical_and(kk > 0, kk < nk - 1))
            def _():
                acc_ref[...] += part

            @pl.when(kk == nk - 1)
            def _():
                finish(acc_ref[...] + part)

    in_specs = [a_spec, b_spec] + ([o_spec] if has_add else [])
    args = (a, b) + ((add,) if has_add else ())
    return pl.pallas_call(
        body, name=name, grid=(m // tm, n // tn, nk), in_specs=in_specs, out_specs=o_spec,
        out_shape=jax.ShapeDtypeStruct((m, n), out_dtype),
        scratch_shapes=[pltpu.VMEM((tm, tn), F32)],
        compiler_params=_params("parallel", "parallel", "arbitrary"),
    )(*args)


def _rmsnorm_fwd(x, w, name):
    l, d = x.shape
    tr = _pick(l, 512, 8)

    def body(x_ref, w_ref, h_ref):
        xv = x_ref[...]
        r = lax.rsqrt(jnp.mean(xv * xv, axis=-1, keepdims=True) + EPS)
        h_ref[...] = (xv * r * w_ref[...]).astype(BF16)

    return pl.pallas_call(
        body, name=name, grid=(l // tr,),
        in_specs=[pl.BlockSpec((tr, d), lambda i: (i, 0)), pl.BlockSpec((1, d), lambda i: (0, 0))],
        out_specs=pl.BlockSpec((tr, d), lambda i: (i, 0)),
        out_shape=jax.ShapeDtypeStruct((l, d), BF16), compiler_params=_params("parallel"),
    )(x, w)


def _rmsnorm_bwd(dh, x, w, dres, name):
    l, d = x.shape
    tr = _pick(l, 256, 8)

    def body(dh_ref, x_ref, w_ref, dres_ref, dx_ref, dw_ref):
        xv = x_ref[...]
        r = lax.rsqrt(jnp.mean(xv * xv, axis=-1, keepdims=True) + EPS)
        xh = xv * r
        dhv = dh_ref[...]
        dxh = dhv * w_ref[...]
        dx_ref[...] = dres_ref[...] + r * (dxh - xh * jnp.mean(dxh * xh, axis=-1, keepdims=True))
        part = jnp.sum(dhv * xh, axis=0, keepdims=True)

        @pl.when(pl.program_id(0) == 0)
        def _():
            dw_ref[...] = part

        @pl.when(pl.program_id(0) > 0)
        def _():
            dw_ref[...] += part

    row = pl.BlockSpec((tr, d), lambda i: (i, 0))
    vec = pl.BlockSpec((1, d), lambda i: (0, 0))
    return pl.pallas_call(
        body, name=name, grid=(l // tr,), in_specs=[row, row, vec, row], out_specs=[row, vec],
        out_shape=[jax.ShapeDtypeStruct((l, d), F32), jax.ShapeDtypeStruct((1, d), F32)],
        compiler_params=_params("arbitrary"),
    )(dh, x, w, dres)


def _final_loss(h, w, target, name):
    l, d = h.shape
    tr = _pick(l, 256, 8)

    def body(h_ref, w_ref, t_ref, dh_ref, dw_ref, loss_ref):
        xv = h_ref[...]
        r = lax.rsqrt(jnp.mean(xv * xv, axis=-1, keepdims=True) + EPS)
        xh = xv * r
        err = xh * w_ref[...] - t_ref[...]
        dy = err * (1.0 / d)
        dxh = dy * w_ref[...]
        dh_ref[...] = r * (dxh - xh * jnp.mean(dxh * xh, axis=-1, keepdims=True))
        part = jnp.sum(dy * xh, axis=0, keepdims=True)
        lpart = jnp.zeros((8, LANES), F32) + 0.5 * jnp.sum(jnp.mean(err * err, axis=-1, keepdims=True))

        @pl.when(pl.program_id(0) == 0)
        def _():
            dw_ref[...] = part
            loss_ref[...] = lpart

        @pl.when(pl.program_id(0) > 0)
        def _():
            dw_ref[...] += part
            loss_ref[...] += lpart

    row = pl.BlockSpec((tr, d), lambda i: (i, 0))
    vec = pl.BlockSpec((1, d), lambda i: (0, 0))
    return pl.pallas_call(
        body, name=name, grid=(l // tr,), in_specs=[row, vec, row],
        out_specs=[row, vec, pl.BlockSpec((8, LANES), lambda i: (0, 0))],
        out_shape=[jax.ShapeDtypeStruct((l, d), F32), jax.ShapeDtypeStruct((1, d), F32), jax.ShapeDtypeStruct((8, LANES), F32)],
        compiler_params=_params("arbitrary"),
    )(h, w, target)


def _gatenorm_fwd(y, proj_a, w, name):
    l, d = y.shape
    dg = d // NGROUPS
    tr = _pick(l, 256, 8)

    def body(y_ref, z_ref, w_ref, o_ref):
        for g in range(NGROUPS):
            sl = slice(g * dg, (g + 1) * dg)
            zv = z_ref[:, sl]
            u = y_ref[:, sl] * (zv * _sigmoid(zv))
            r = lax.rsqrt(jnp.mean(u * u, axis=-1, keepdims=True) + EPS)
            o_ref[:, sl] = (u * r * w_ref[:, sl]).astype(BF16)

    row = pl.BlockSpec((tr, d), lambda i: (i, 0))
    return pl.pallas_call(
        body, name=name, grid=(l // tr,), in_specs=[row, row, pl.BlockSpec((1, d), lambda i: (0, 0))],
        out_specs=row, out_shape=jax.ShapeDtypeStruct((l, d), BF16), compiler_params=_params("parallel"),
    )(y, proj_a, w)


def _gatenorm_bwd(dout, y, proj_a, w, name):
    l, d = y.shape
    dg = d // NGROUPS
    tr = _pick(l, 256, 8)

    def body(do_ref, y_ref, z_ref, w_ref, dy_ref, dz_ref, dw_ref):
        first = pl.program_id(0) == 0
        for g in range(NGROUPS):
            sl = slice(g * dg, (g + 1) * dg)
            zv = z_ref[:, sl]
            sg = _sigmoid(zv)
            sz = zv * sg
            yv = y_ref[:, sl]
            u = yv * sz
            r = lax.rsqrt(jnp.mean(u * u, axis=-1, keepdims=True) + EPS)
            nh = u * r
            dov = do_ref[:, sl]
            dn = dov * w_ref[:, sl]
            du = r * (dn - nh * jnp.mean(dn * nh, axis=-1, keepdims=True))
            dy_ref[:, sl] = du * sz
            dz_ref[:, sl] = (du * yv * (sg * (1.0 + zv * (1.0 - sg)))).astype(BF16)
            part = jnp.sum(dov * nh, axis=0, keepdims=True)

            @pl.when(first)
            def _():
                dw_ref[:, sl] = part

            @pl.when(jnp.logical_not(first))
            def _():
                dw_ref[:, sl] += part

    row = pl.BlockSpec((tr, d), lambda i: (i, 0))
    vec = pl.BlockSpec((1, d), lambda i: (0, 0))
    return pl.pallas_call(
        body, name=name, grid=(l // tr,), in_specs=[row, row, row, vec], out_specs=[row, row, vec],
        out_shape=[jax.ShapeDtypeStruct((l, d), F32), jax.ShapeDtypeStruct((l, d), BF16), jax.ShapeDtypeStruct((1, d), F32)],
        compiler_params=_params("arbitrary"),
    )(dout, y, proj_a, w)


def _conv_cols(d):
    return _pick(math.gcd(d, d + 2 * NGROUPS * NSTATE), 512, LANES)


def _conv_fwd(proj_a, conv_w, conv_b, d, name):
    l = proj_a.shape[0]
    cdim = d + 2 * NGROUPS * NSTATE
    cw = _conv_cols(d)
    off = d // cw
    tl = _pick(l, 512, 8)
    hb = tl // 8

    def body(u_ref, up_ref, w_ref, b_ref, acc_ref, xbc_ref, ext):
        i = pl.program_id(1)
        ext[0:8, :] = jnp.where(i > 0, up_ref[...], 0.0)
        ext[8:8 + tl, :] = u_ref[...]
        acc = jnp.zeros((tl, cw), F32) + b_ref[...]
        for j in range(KCONV):
            acc = acc + w_ref[j:j + 1, :] * ext[pl.ds(8 - (KCONV - 1) + j, tl), :]
        acc_ref[...] = acc
        xbc_ref[...] = acc * _sigmoid(acc)

    blk = pl.BlockSpec((tl, cw), lambda c, i: (i, c))
    return pl.pallas_call(
        body, name=name, grid=(cdim // cw, l // tl),
        in_specs=[pl.BlockSpec((tl, cw), lambda c, i: (i, c + off)),
                  pl.BlockSpec((8, cw), lambda c, i: (jnp.maximum(i * hb - 1, 0), c + off)),
                  pl.BlockSpec((KCONV, cw), lambda c, i: (0, c)), pl.BlockSpec((1, cw), lambda c, i: (0, c))],
        out_specs=[blk, blk],
        out_shape=[jax.ShapeDtypeStruct((l, cdim), F32), jax.ShapeDtypeStruct((l, cdim), F32)],
        scratch_shapes=[pltpu.VMEM((tl + 8, cw), F32)], compiler_params=_params("parallel", "parallel"),
    )(proj_a, proj_a, conv_w, conv_b)


def _conv_bwd(dxbc, acc, proj_a, conv_w, d, name):
    l, cdim = acc.shape
    cw = _conv_cols(d)
    off = d // cw
    tl = _pick(l, 512, 8)
    hb = tl // 8
    nb = l // tl

    def dsilu(g, a):
        s = _sigmoid(a)
        return g * (s * (1.0 + a * (1.0 - s)))

    def body(g_ref, gn_ref, a_ref, an_ref, u_ref, up_ref, w_ref, du_ref, dw_ref, ext, dext):
        i = pl.program_id(1)
        da = dsilu(g_ref[...], a_ref[...])
        dext[0:tl, :] = da
        dext[tl:tl + 8, :] = jnp.where(i < nb - 1, dsilu(gn_ref[...], an_ref[...]), 0.0)
        ext[0:8, :] = jnp.where(i > 0, up_ref[...], 0.0)
        ext[8:8 + tl, :] = u_ref[...]
        du = jnp.zeros((tl, cw), F32)
        rows = []
        for j in range(KCONV):
            du = du + w_ref[j:j + 1, :] * dext[pl.ds(KCONV - 1 - j, tl), :]
            rows.append(jnp.sum(da * ext[pl.ds(8 - (KCONV - 1) + j, tl), :], axis=0, keepdims=True))
        rows.append(jnp.sum(da, axis=0, keepdims=True))
        du_ref[...] = du.astype(BF16)

        @pl.when(i == 0)
        def _():
            dw_ref[...] = jnp.zeros_like(dw_ref)

        for j, rv in enumerate(rows):
            dw_ref[j:j + 1, :] += rv

    blk = pl.BlockSpec((tl, cw), lambda c, i: (i, c))
    nxt = pl.BlockSpec((8, cw), lambda c, i: (jnp.minimum((i + 1) * hb, l // 8 - 1), c))
    return pl.pallas_call(
        body, name=name, grid=(cdim // cw, nb),
        in_specs=[blk, nxt, blk, nxt,
                  pl.BlockSpec((tl, cw), lambda c, i: (i, c + off)),
                  pl.BlockSpec((8, cw), lambda c, i: (jnp.maximum(i * hb - 1, 0), c + off)),
                  pl.BlockSpec((KCONV, cw), lambda c, i: (0, c))],
        out_specs=[blk, pl.BlockSpec((8, cw), lambda c, i: (0, c))],
        out_shape=[jax.ShapeDtypeStruct((l, cdim), BF16), jax.ShapeDtypeStruct((8, cdim), F32)],
        scratch_shapes=[pltpu.VMEM((tl + 8, cw), F32), pltpu.VMEM((tl + 8, cw), F32)],
        compiler_params=_params("parallel", "arbitrary"),
    )(dxbc, dxbc, acc, acc, proj_a, proj_a, conv_w)


def _ssd_common(dt_ref, bias_ref, alog_ref):
    li = lax.broadcasted_iota(jnp.int32, (CHUNK, CHUNK), 0)
    si = lax.broadcasted_iota(jnp.int32, (CHUNK, CHUNK), 1)
    tri = si <= li
    dtv = _softplus(dt_ref[...] + bias_ref[0])
    a_neg = -jnp.exp(alog_ref[0])
    da = dtv * a_neg
    cs_col = sum(_dot(tri.astype(BF16), p, NN) for p in _split3(da))
    cs_row = sum(_dot(p, (li <= si).astype(BF16), TN) for p in _split3(da))
    return tri, dtv, a_neg, cs_col, cs_row


def _dot2(x, t16, dims):
    hi, lo = _split2(x)
    return _dot(hi, t16, dims) + _dot(lo, t16, dims)


def _ssd_expand(dtv, cs_col, dsk, pg):
    expm = (lax.shift_right_logical(lax.broadcasted_iota(jnp.int32, (LANES, pg), 1), SSD_HD_SHIFT)
            == lax.broadcasted_iota(jnp.int32, (LANES, pg), 0)).astype(BF16)
    tot = cs_col[CHUNK - 1:CHUNK, :]
    stack = jnp.concatenate([dtv, jnp.exp(cs_col), jnp.exp(tot - cs_col), jnp.broadcast_to(dsk, (CHUNK, LANES)),
                             jnp.broadcast_to(jnp.exp(tot), (CHUNK, LANES))], axis=0)
    ex = _dot2(stack, expm, NN)
    return tuple(ex[CHUNK * a:CHUNK * (a + 1)] for a in range(5))


def _ssd_specs(d, r_heads):
    pg = r_heads * SSD_HD
    nb = d // LANES
    dt_blk = (2 * d + 2 * NGROUPS * NSTATE) // LANES
    x_spec = lambda cmap: pl.BlockSpec((CHUNK, pg), lambda g, c: (cmap(c), g))
    b_spec = lambda cmap: pl.BlockSpec((CHUNK, NSTATE), lambda g, c: (cmap(c), nb + g))
    c_spec = lambda cmap: pl.BlockSpec((CHUNK, NSTATE), lambda g, c: (cmap(c), nb + NGROUPS + g))
    dt_spec = lambda cmap: pl.BlockSpec((CHUNK, LANES), lambda g, c: (cmap(c), dt_blk + g))
    const = pl.BlockSpec((1, 1, LANES), lambda g, c: (g, 0, 0))
    return pg, x_spec, b_spec, c_spec, dt_spec, const


def _ssd_fwd(xbc, proj_a, bias_g, alog_g, dsk_g, d, name):
    l = xbc.shape[0]
    nc = l // CHUNK
    r_heads = d // SSD_HD // NGROUPS
    pg, x_spec, b_spec, c_spec, dt_spec, const = _ssd_specs(d, r_heads)
    ident = lambda c: c

    def body(x_ref, b_ref, c_ref, dt_ref, bias_ref, alog_ref, dsk_ref, y_ref, hs_ref, h_scr):
        @pl.when(pl.program_id(1) == 0)
        def _():
            h_scr[...] = jnp.zeros_like(h_scr)

        tri, dtv, _, cs_col, cs_row = _ssd_common(dt_ref, bias_ref, alog_ref)
        hin = h_scr[...]
        hs_ref[0, 0] = hin
        bm = b_ref[...].astype(BF16)
        cm = c_ref[...].astype(BF16)
        xv = x_ref[...]
        e_dt, e_ecs, e_decs, e_dsk, e_etot = _ssd_expand(dtv, cs_col, dsk_ref[0], pg)
        xd = xv * e_dt
        xd16 = xd.astype(BF16)
        gmat = _dot(cm, bm, NT)
        for r in range(r_heads):
            sl = slice(SSD_HD * r, SSD_HD * (r + 1))
            lm = jnp.exp(jnp.where(tri, cs_col[:, r:r + 1] - cs_row[r:r + 1, :], -jnp.inf))
            y_ref[:, sl] = _dot((gmat * lm).astype(BF16), xd16[:, sl], NN)
        y_ref[...] += e_ecs * _dot(cm, hin.astype(BF16), NN) + e_dsk * xv
        h_scr[...] = hin * e_etot[0:1] + _dot(bm, (xd * e_decs).astype(BF16), TN)

    return pl.pallas_call(
        body, name=name, grid=(NGROUPS, nc),
        in_specs=[x_spec(ident), b_spec(ident), c_spec(ident), dt_spec(ident), const, const, const],
        out_specs=[x_spec(ident), pl.BlockSpec((1, 1, NSTATE, pg), lambda g, c: (c, g, 0, 0))],
        out_shape=[jax.ShapeDtypeStruct((l, d), F32), jax.ShapeDtypeStruct((nc, NGROUPS, NSTATE, pg), F32)],
        scratch_shapes=[pltpu.VMEM((NSTATE, pg), F32)], compiler_params=_params("parallel", "arbitrary"),
    )(xbc, xbc, xbc, proj_a, bias_g, alog_g, dsk_g)


def _ssd_bwd(dy, xbc, proj_a, hs, bias_g, alog_g, dsk_g, d, name):
    l = xbc.shape[0]
    nc = l // CHUNK
    r_heads = d // SSD_HD // NGROUPS
    pg, x_spec, b_spec, c_spec, dt_spec, const = _ssd_specs(d, r_heads)
    rev = lambda c: nc - 1 - c
    cdim = d + 2 * NGROUPS * NSTATE

    def body(dy_ref, x_ref, b_ref, c_ref, dt_ref, hs_ref, bias_ref, alog_ref, dsk_ref,
             dx_ref, db_ref, dc_ref, ddt_ref, sums_ref, dh_scr, p_scr, pt_scr, dxd_scr):
        first = pl.program_id(1) == 0

        @pl.when(first)
        def _():
            dh_scr[...] = jnp.zeros_like(dh_scr)
            sums_ref[...] = jnp.zeros_like(sums_ref)

        tri, dtv, a_neg, cs_col, cs_row = _ssd_common(dt_ref, bias_ref, alog_ref)
        li = lax.broadcasted_iota(jnp.int32, (CHUNK, CHUNK), 0)
        si = lax.broadcasted_iota(jnp.int32, (CHUNK, CHUNK), 1)
        tri_t = li <= si
        lastrow = lax.broadcasted_iota(jnp.int32, (CHUNK, 1), 0) == CHUNK - 1
        indm = (lax.shift_right_logical(lax.broadcasted_iota(jnp.int32, (pg, LANES), 0), SSD_HD_SHIFT)
                == lax.broadcasted_iota(jnp.int32, (pg, LANES), 1)).astype(BF16)
        hin = hs_ref[0, 0]
        dhout = dh_scr[...]
        hin16 = hin.astype(BF16)
        dhout16 = dhout.astype(BF16)
        bm = b_ref[...].astype(BF16)
        cm = c_ref[...].astype(BF16)
        xv = x_ref[...]
        dyv = dy_ref[...]
        e_dt, e_ecs, e_decs, e_dsk, e_etot = _ssd_expand(dtv, cs_col, dsk_ref[0], pg)
        xd = xv * e_dt
        xd16 = xd.astype(BF16)
        dy16 = dyv.astype(BF16)
        dye = dyv * e_ecs
        dye16 = dye.astype(BF16)
        gmat = _dot(cm, bm, NT)
        gmat_t = _dot(bm, cm, NT)
        dg = jnp.zeros((CHUNK, CHUNK), F32)
        dg_t = jnp.zeros((CHUNK, CHUNK), F32)
        for r in range(r_heads):
            sl = slice(SSD_HD * r, SSD_HD * (r + 1))
            col = cs_col[:, r:r + 1]
            row = cs_row[r:r + 1, :]
            lm = jnp.exp(jnp.where(tri, col - row, -jnp.inf))
            lm_t = jnp.exp(jnp.where(tri_t, row - col, -jnp.inf))
            dm = _dot(dy16[:, sl], xd16[:, sl], NT)
            dm_t = _dot(xd16[:, sl], dy16[:, sl], NT)
            dg = dg + dm * lm
            dg_t = dg_t + dm_t * lm_t
            m_t = gmat_t * lm_t
            p_scr[:, sl] = dm * (gmat * lm)
            pt_scr[:, sl] = dm_t * m_t
            dxd_scr[:, sl] = _dot(m_t.astype(BF16), dy16[:, sl], NN)
        yoff = _dot(cm, hin16, NN)
        qall = _dot(bm, dhout16, NN)
        dxd = dxd_scr[...] + qall * e_decs
        hh = jnp.broadcast_to(jnp.sum(dhout * hin, axis=0, keepdims=True), (8, pg))
        red = _dot2(jnp.concatenate([dye * yoff, qall * xd, dxd * xv, dyv * xv, p_scr[...], pt_scr[...], hh], axis=0), indm, NN)
        r_yoff, r_q, r_dt, r_dsk, r_p, r_pt = (red[CHUNK * a:CHUNK * (a + 1)] for a in range(6))
        tot = cs_col[CHUNK - 1:CHUNK, :]
        ddec = r_q * jnp.exp(tot - cs_col)
        dtot = jnp.sum(ddec, axis=0, keepdims=True) + jnp.exp(tot) * red[6 * CHUNK:6 * CHUNK + 1]
        dcs = r_p - r_pt + r_yoff - ddec + jnp.where(lastrow, dtot, 0.0)
        dda = sum(_dot(tri_t.astype(BF16), p, NN) for p in _split3(dcs))
        draw = (r_dt + dda * a_neg) * _sigmoid(dt_ref[...] + bias_ref[0])
        ddt_ref[...] = draw
        sums_ref[0, 0:1, :] += jnp.sum(draw, axis=0, keepdims=True)
        sums_ref[0, 1:2, :] += jnp.sum(dda * dtv, axis=0, keepdims=True)
        sums_ref[0, 2:3, :] += jnp.sum(r_dsk, axis=0, keepdims=True)
        dx_ref[...] = dxd * e_dt + e_dsk * dyv
        dc_ref[...] = _dot(dg.astype(BF16), bm, NN) + _dot(dye16, hin16, NT)
        db_ref[...] = _dot(dg_t.astype(BF16), cm, NN) + _dot((xd * e_decs).astype(BF16), dhout16, NT)
        dh_scr[...] = dhout * e_etot[0:1] + _dot(cm, dye16, TN)

    grp = pl.BlockSpec((CHUNK, LANES), lambda g, c: (rev(c), g))
    return pl.pallas_call(
        body, name=name, grid=(NGROUPS, nc),
        in_specs=[x_spec(rev), x_spec(rev), b_spec(rev), c_spec(rev), dt_spec(rev),
                  pl.BlockSpec((1, 1, NSTATE, pg), lambda g, c: (rev(c), g, 0, 0)), const, const, const],
        out_specs=[x_spec(rev), grp, grp, grp, pl.BlockSpec((1, 8, LANES), lambda g, c: (g, 0, 0))],
        out_shape=[jax.ShapeDtypeStruct((l, d), F32), jax.ShapeDtypeStruct((l, NGROUPS * NSTATE), F32),
                   jax.ShapeDtypeStruct((l, NGROUPS * NSTATE), F32),
                   jax.ShapeDtypeStruct((l, NGROUPS * LANES), F32), jax.ShapeDtypeStruct((NGROUPS, 8, LANES), F32)],
        scratch_shapes=[pltpu.VMEM((NSTATE, pg), F32), pltpu.VMEM((CHUNK, pg), F32), pltpu.VMEM((CHUNK, pg), F32),
                        pltpu.VMEM((CHUNK, pg), F32)],
        compiler_params=_params("parallel", "arbitrary"),
    )(dy, xbc, xbc, xbc, proj_a, hs, bias_g, alog_g, dsk_g)


LOG2E = 1.4426950408889634
SBA_TK = 256
SBA_TQ_FWD = 1024
SBA_TQ_BWD = 1024
SBA_ROW_PARTS = 8


def _sba_tiles(l, tq_target):
    tk = _pick(l, SBA_TK, LANES)
    tq = _pick(l, tq_target, tk)
    assert l // tk <= LANES
    return tq, tk


def _sba_scores(qk, valid, scale):
    z2 = qk * (scale * LOG2E)
    t2 = jnp.log2(1.0 + jnp.exp2(-jnp.abs(z2)))
    la = jnp.minimum(z2, 0.0) - t2
    lk = la - z2
    if valid is not None:
        lk = jnp.where(valid, lk, 0.0)
    return lk, la


def _sba_fwd(proj_a, proj_c, d, g_off, name):
    l = proj_c.shape[0]
    nh = d // SBA_HD
    tq, tk = _sba_tiles(l, SBA_TQ_FWD)
    band = tq // tk
    scale = 1.0 / math.sqrt(SBA_HD)
    rq = tq // SBA_ROW_PARTS if tq % (SBA_ROW_PARTS * 16) == 0 else tq
    parts = [pl.ds(p * rq, rq) for p in range(tq // rq)]

    def body(q_ref, k_ref, v_ref, g_ref, o_ref, y_ref, rs_ref, rs_scr, qk_a, qk_b):
        i = pl.program_id(1)
        ki = lax.broadcasted_iota(jnp.int32, (tk, tk), 0)
        kj = lax.broadcasted_iota(jnp.int32, (tk, tk), 1)
        uex = (ki > kj).astype(BF16)
        lane = lax.broadcasted_iota(jnp.int32, (rq, LANES), 1)
        rs_scr[...] = jnp.zeros_like(rs_scr)
        qs = [q_ref[ps, :] for ps in parts]

        def qk_into(j, qk_scr):
            kb = k_ref[pl.ds(pl.multiple_of(j * tk, tk), tk), :]
            for p, ps in enumerate(parts):
                qk_scr[ps, :] = _dot(qs[p], kb, NT)

        def tile(j, carry, masked, qk_scr=None):
            start = pl.multiple_of(j * tk, tk)
            vb = v_ref[pl.ds(start, tk), :]
            valid = [None] * len(parts)
            if masked:
                rows = lax.broadcasted_iota(jnp.int32, (rq, tk), 0)
                cols = lax.broadcasted_iota(jnp.int32, (rq, tk), 1)
                valid = [cols - rows < i * tq + p * rq - j * tk for p in range(len(parts))]
            if qk_scr is None:
                kb = k_ref[pl.ds(start, tk), :]
                qk = [_dot(qs[p], kb, NT) for p in range(len(parts))]
            else:
                qk = [qk_scr[ps, :] for ps in parts]
            sc = [_sba_scores(qk[p], valid[p], scale) for p in range(len(parts))]
            later = [_dot2(lk, uex, NN) for lk, _ in sc]
            out = []
            for p, ps in enumerate(parts):
                rsum, acc = carry[p]
                w = jnp.exp2(sc[p][1] + later[p] + rsum)
                if masked:
                    w = jnp.where(valid[p], w, 0.0)
                acc = acc + _dot(w.astype(BF16), vb, NN)
                rs_scr[ps, :] = jnp.where(lane == j, rsum, rs_scr[ps, :])
                out.append((rsum + later[p][:, 0:1] + sc[p][0][:, 0:1], acc))
            return tuple(out)

        carry = tuple((jnp.zeros((rq, 1), F32), jnp.zeros((rq, SBA_HD), F32)) for _ in parts)
        top = (i + 1) * band - 1
        carry = lax.fori_loop(0, band, lambda jj, c: tile(top - jj, c, True), carry)
        nfull = i * band
        if band % 2 == 0:
            @pl.when(nfull > 0)
            def _():
                qk_into(nfull - 1, qk_a)

            def pair(m, c):
                ja = nfull - 1 - 2 * m
                qk_into(ja - 1, qk_b)
                c = tile(ja, c, False, qk_a)
                qk_into(jnp.maximum(ja - 2, 0), qk_a)
                return tile(ja - 1, c, False, qk_b)

            carry = lax.fori_loop(0, nfull // 2, pair, carry)
        else:
            carry = lax.fori_loop(0, nfull, lambda jj, c: tile(nfull - 1 - jj, c, False), carry)
        rs_ref[...] = rs_scr[...]
        for p, ps in enumerate(parts):
            acc = carry[p][1]
            o_ref[ps, :] = acc
            gv = g_ref[ps, :]
            y_ref[ps, :] = (acc * (gv * _sigmoid(gv))).astype(BF16)

    goff = g_off // SBA_HD
    blk = lambda off: pl.BlockSpec((tq, SBA_HD), lambda h, i: (i, h + off))
    full = lambda off: pl.BlockSpec((l, SBA_HD), lambda h, i: (0, h + off))
    out = pl.BlockSpec((tq, SBA_HD), lambda h, i: (i, h))
    return pl.pallas_call(
        body, name=name, grid=(nh, l // tq), in_specs=[blk(0), full(nh), full(2 * nh), blk(goff)], out_specs=[out, out, out],
        scratch_shapes=[pltpu.VMEM((tq, LANES), F32), pltpu.VMEM((tq, tk), F32), pltpu.VMEM((tq, tk), F32)],
        out_shape=[jax.ShapeDtypeStruct((l, d), F32), jax.ShapeDtypeStruct((l, d), BF16), jax.ShapeDtypeStruct((l, d), F32)],
        compiler_params=_params("parallel", "arbitrary"),
    )(proj_c, proj_c, proj_c, proj_a)


def _sba_bwd(dys, o, rs, proj_a, proj_c, d, g_off, name):
    l = proj_c.shape[0]
    nh = d // SBA_HD
    tq, tk = _sba_tiles(l, SBA_TQ_BWD)
    band = tq // tk
    scale = 1.0 / math.sqrt(SBA_HD)

    def body(dy_ref, o_ref, rs_ref, q_ref, k_ref, v_ref, g_ref, dq_ref, dk_ref, dv_ref, dg_ref, dk_acc, dv_acc,
             qk_a, qk_b, dw_a, dw_b):
        i = pl.program_id(1)
        nq = pl.num_programs(1)

        @pl.when(i == 0)
        def _():
            dk_acc[...] = jnp.zeros_like(dk_acc)
            dv_acc[...] = jnp.zeros_like(dv_acc)

        ki = lax.broadcasted_iota(jnp.int32, (tk, tk), 0)
        kj = lax.broadcasted_iota(jnp.int32, (tk, tk), 1)
        uex = (ki > kj).astype(BF16)
        ulow = (ki < kj).astype(BF16)
        lane = lax.broadcasted_iota(jnp.int32, (tq, LANES), 1)
        q = q_ref[...]
        gv = g_ref[...]
        sg = _sigmoid(gv)
        dyv = dy_ref[...]
        dg_ref[...] = (dyv * o_ref[...] * (sg * (1.0 + gv * (1.0 - sg)))).astype(BF16)
        do16 = (dyv * (gv * sg)).astype(BF16)

        def ahead(j, qk_scr, dw_scr):
            start = pl.multiple_of(j * tk, tk)
            qk_scr[...] = _dot(q, k_ref[pl.ds(start, tk), :], NT)
            dw_scr[...] = _dot(do16, v_ref[pl.ds(start, tk), :], NT)

        def tile(j, carry, masked, qk_scr=None, dw_scr=None):
            epre, dq = carry
            start = pl.multiple_of(j * tk, tk)
            kb = k_ref[pl.ds(start, tk), :]
            rsum = jnp.sum(jnp.where(lane == j, rs_ref[...], 0.0), axis=1, keepdims=True)
            valid = None
            if masked:
                rows = lax.broadcasted_iota(jnp.int32, (tq, tk), 0)
                cols = lax.broadcasted_iota(jnp.int32, (tq, tk), 1)
                valid = cols - rows < i * tq - j * tk
            if qk_scr is None:
                qk = _dot(q, kb, NT)
                dw = _dot(do16, v_ref[pl.ds(start, tk), :], NT)
            else:
                qk = qk_scr[...]
                dw = dw_scr[...]
            lk, la = _sba_scores(qk, valid, scale)
            later = _dot2(lk, uex, NN)
            w = jnp.exp2(la + later + rsum)
            if masked:
                w = jnp.where(valid, w, 0.0)
            e = w * dw
            epx = _dot2(e, ulow, NN)
            dz = (e - jnp.exp2(la) * (e + epre + epx)) * scale
            if masked:
                dz = jnp.where(valid, dz, 0.0)
            dz16 = dz.astype(BF16)
            dq = dq + _dot(dz16, kb, NN)
            dk_acc[pl.ds(start, tk), :] += _dot(dz16, q, TN)
            dv_acc[pl.ds(start, tk), :] += _dot(w.astype(BF16), do16, TN)
            return epre + epx[:, tk - 1:tk] + e[:, tk - 1:tk], dq

        carry = (jnp.zeros((tq, 1), F32), jnp.zeros((tq, SBA_HD), F32))
        nfull = i * band
        if band % 2 == 0:
            @pl.when(nfull > 0)
            def _():
                ahead(0, qk_a, dw_a)

            def pair(m, c):
                ja = 2 * m
                ahead(ja + 1, qk_b, dw_b)
                c = tile(ja, c, False, qk_a, dw_a)
                ahead(jnp.minimum(ja + 2, nfull - 1), qk_a, dw_a)
                return tile(ja + 1, c, False, qk_b, dw_b)

            carry = lax.fori_loop(0, nfull // 2, pair, carry)
        else:
            carry = lax.fori_loop(0, nfull, lambda j, c: tile(j, c, False), carry)
        carry = lax.fori_loop(0, band, lambda jj, c: tile(nfull + jj, c, True), carry)
        dq_ref[...] = carry[1].astype(BF16)

        @pl.when(i == nq - 1)
        def _():
            dk_ref[...] = dk_acc[...].astype(BF16)
            dv_ref[...] = dv_acc[...].astype(BF16)

    goff = g_off // SBA_HD
    blk = lambda off: pl.BlockSpec((tq, SBA_HD), lambda h, i: (i, h + off))
    full = lambda off: pl.BlockSpec((l, SBA_HD), lambda h, i: (0, h + off))
    out = pl.BlockSpec((tq, SBA_HD), lambda h, i: (i, h))
    outfull = pl.BlockSpec((l, SBA_HD), lambda h, i: (0, h))
    sd = jax.ShapeDtypeStruct((l, d), BF16)
    return pl.pallas_call(
        body, name=name, grid=(nh, l // tq),
        in_specs=[blk(nh), out, out, blk(0), full(nh), full(2 * nh), blk(goff)],
        out_specs=[out, outfull, outfull, out], out_shape=[sd, sd, sd, sd],
        scratch_shapes=[pltpu.VMEM((l, SBA_HD), F32), pltpu.VMEM((l, SBA_HD), F32)] + [pltpu.VMEM((tq, tk), F32)] * 4,
        compiler_params=_params("parallel", "arbitrary"),
    )(dys, o, rs, proj_c, proj_c, proj_c, proj_a)


def _adamw_math(w, g, m, v):
    m = ADAM_B1 * m + (1.0 - ADAM_B1) * g
    v = ADAM_B2 * v + (1.0 - ADAM_B2) * (g * g)
    m_hat = m / (1.0 - ADAM_B1 ** ADAM_STEP)
    v_hat = v / (1.0 - ADAM_B2 ** ADAM_STEP)
    delta = -ADAM_LR * (m_hat / (jnp.sqrt(v_hat) + ADAM_EPS) + ADAM_WD * w)
    return delta, m, v


def _adamw(w, g, m, v, name):
    a, r, c = w.shape
    tr = _pick(r, max(8, (1 << 19) // c // 8 * 8), 8)

    def body(w_ref, g_ref, m_ref, v_ref, d_ref, nm_ref, nv_ref):
        dl, nm, nv = _adamw_math(w_ref[...], g_ref[...], m_ref[...], v_ref[...])
        d_ref[...] = dl
        nm_ref[...] = nm
        nv_ref[...] = nv

    blk = pl.BlockSpec((1, tr, c), lambda i, j: (i, j, 0))
    sd = jax.ShapeDtypeStruct(w.shape, F32)
    return pl.pallas_call(
        body, name=name, grid=(a, r // tr), in_specs=[blk] * 4, out_specs=[blk] * 3, out_shape=[sd] * 3,
        compiler_params=_params("parallel", "parallel"),
    )(w, g, m, v)


def _dims(d):
    cdim = d + 2 * NGROUPS * NSTATE
    heads = d // SSD_HD
    r_heads = heads // NGROUPS
    g_off = d + cdim + NGROUPS * LANES
    na = g_off + d
    nc = 3 * d
    return cdim, heads, r_heads, na, nc, g_off


def _pack_w_in(w_in, d):
    cdim, heads, r_heads, na, nc, g_off = _dims(d)
    o = d + cdim
    w_dt = w_in[:, o:o + heads].reshape(d, NGROUPS, r_heads)
    w_dt = jnp.pad(w_dt, ((0, 0), (0, 0), (0, LANES - r_heads))).reshape(d, NGROUPS * LANES)
    return jnp.concatenate([w_in[:, :o], w_dt, w_in[:, o + heads + nc:]], axis=1), w_in[:, o + heads:o + heads + nc]


def _unpack_w_in(ga, gc, d):
    cdim, heads, r_heads, na, nc, g_off = _dims(d)
    o = d + cdim
    g_dt = ga[:, o:g_off].reshape(d, NGROUPS, LANES)[:, :, :r_heads].reshape(d, heads)
    return jnp.concatenate([ga[:, :o], g_dt, gc, ga[:, g_off:]], axis=1)


def _group_vec(v, r_heads):
    return jnp.pad(v.reshape(NGROUPS, 1, r_heads), ((0, 0), (0, 0), (0, LANES - r_heads)))


def _layer_fwd(x, p, d, tag):
    cdim, heads, r_heads, na, nc, g_off = _dims(d)
    h = _rmsnorm_fwd(x, p["norm_w"], f"norm_f{tag}")
    proj_a = _matmul(h, p["wa"], "nn", F32, f"inproj_a{tag}")
    proj_c = _matmul(h, p["wc"], "nn", BF16, f"inproj_c{tag}")
    acc, xbc = _conv_fwd(proj_a, p["conv_w"], p["conv_b"], d, f"conv_f{tag}")
    y, hs = _ssd_fwd(xbc, proj_a, p["bias_g"], p["alog_g"], p["dsk_g"], d, f"ssd_f{tag}")
    y_ssd = _gatenorm_fwd(y, proj_a, p["ssd_norm_w"], f"gate_f{tag}")
    o, y_sba, rs = _sba_fwd(proj_a, proj_c, d, g_off, f"sba_f{tag}")
    mix = jnp.concatenate([y_ssd, y_sba], axis=1)
    x_next = _matmul(mix, p["w_out"], "nn", F32, f"outproj{tag}", add=x)
    return x_next, dict(x=x, h=h, proj_a=proj_a, proj_c=proj_c, acc=acc, xbc=xbc, y=y, hs=hs, o=o, rs=rs, mix=mix)


def _layer_bwd(dxn, s, p, d, tag):
    cdim, heads, r_heads, na, nc, g_off = _dims(d)
    dxn16 = dxn.astype(BF16)
    dmix = _matmul(dxn16, p["w_out"], "nt", F32, f"dmix{tag}")
    g_w_out = _matmul(s["mix"], dxn16, "tn", F32, f"dwout{tag}")
    dq, dk, dv, dg = _sba_bwd(dmix, s["o"], s["rs"], s["proj_a"], s["proj_c"], d, g_off, f"sba_b{tag}")
    dy, dz, g_ssd_norm = _gatenorm_bwd(dmix, s["y"], s["proj_a"], p["ssd_norm_w"], f"gate_b{tag}")
    dx_s, db_s, dc_s, ddt, sums = _ssd_bwd(dy, s["xbc"], s["proj_a"], s["hs"], p["bias_g"], p["alog_g"], p["dsk_g"], d, f"ssd_b{tag}")
    dxbc = jnp.concatenate([dx_s, db_s, dc_s], axis=1)
    du, g_conv = _conv_bwd(dxbc, s["acc"], s["proj_a"], p["conv_w"], d, f"conv_b{tag}")
    dproj_a = jnp.concatenate([dz, du, ddt.astype(BF16), dg], axis=1)
    dproj_c = jnp.concatenate([dq, dk, dv], axis=1)
    g_wa = _matmul(s["h"], dproj_a, "tn", F32, f"dwin_a{tag}")
    g_wc = _matmul(s["h"], dproj_c, "tn", F32, f"dwin_c{tag}")
    dh = _matmul(dproj_a, p["wa"], "nt", F32, f"dh_a{tag}")
    dh = _matmul(dproj_c, p["wc"], "nt", F32, f"dh_c{tag}", add=dh)
    dx, g_norm = _rmsnorm_bwd(dh, s["x"], p["norm_w"], dxn, f"norm_b{tag}")
    a_neg = -jnp.exp(p["alog_g"][:, 0, :r_heads].reshape(heads))
    grads = dict(
        norm_w=g_norm[0], w_in=_unpack_w_in(g_wa, g_wc, d), conv_w=g_conv[:KCONV], conv_b=g_conv[KCONV],
        dt_bias=sums[:, 0, :r_heads].reshape(heads), a_log=sums[:, 1, :r_heads].reshape(heads) * a_neg,
        d_skip=sums[:, 2, :r_heads].reshape(heads), ssd_norm_w=g_ssd_norm[0], w_out=g_w_out)
    return dx, grads


def _local_step(x, target, w_in16, w_out16, conv_w, small):
    l, d = x.shape
    depth = w_in16.shape[0]
    r_heads = _dims(d)[2]
    layers = []
    for i in range(depth):
        wa, wc = _pack_w_in(w_in16[i], d)
        layers.append(dict(
            norm_w=small["norm_w"][i][None], wa=wa, wc=wc, conv_w=conv_w[i], conv_b=small["conv_b"][i][None],
            bias_g=_group_vec(small["dt_bias"][i], r_heads), alog_g=_group_vec(small["a_log"][i], r_heads),
            dsk_g=_group_vec(small["d_skip"][i], r_heads), ssd_norm_w=small["ssd_norm_w"][i][None], w_out=w_out16[i]))
    saved = []
    hcur = x
    for i in range(depth):
        hcur, s = _layer_fwd(hcur, layers[i], d, str(i))
        saved.append(s)
    dh, g_final, loss = _final_loss(hcur, small["final_norm_w"][None], target, "final_loss")
    grads = [None] * depth
    for i in reversed(range(depth)):
        dh, grads[i] = _layer_bwd(dh, saved[i], layers[i], d, str(i))
    stacked = {k: jnp.stack([g[k] for g in grads]) for k in grads[0]}
    stacked["final_norm_w"] = g_final[0]
    return loss[0, 0], dh, stacked


HBM = pl.BlockSpec(memory_space=pl.ANY)
NCHIP = 4
NDEV = 8


def _mesh_pos():
    x, y, c = lax.axis_index("x"), lax.axis_index("y"), lax.axis_index("c")
    chips = [(1 - x, y), (x, 1 - y), (1 - x, 1 - y)]
    return x, y, c, chips


def _remote(src, dst, send_sem, recv_sem, dev):
    return pltpu.make_async_remote_copy(src_ref=src, dst_ref=dst, send_sem=send_sem, recv_sem=recv_sem,
                                        device_id=dev, device_id_type=MESH)


def _gather_weights(shards):
    n = len(shards)
    hl = shards[0].shape[0] // 2

    def body(*refs):
        ins, outs = refs[:n], refs[n:2 * n]
        send, recv = refs[2 * n:]
        x, y, c, chips = _mesh_pos()
        k = 2 * x + y
        half = pl.ds(c * hl, hl)
        other = pl.ds((1 - c) * hl, hl)
        for a in range(n):
            for j, (px, py) in enumerate(chips):
                _remote(ins[a].at[half], outs[a].at[k, half], send.at[a, j], recv.at[a, j], (px, py, c)).start()
        for a in range(n):
            for j, (px, py) in enumerate(chips):
                kj = 2 * px + py
                got = outs[a].at[kj, half]
                _remote(got, got, send.at[a, j], recv.at[a, j], (px, py, c)).wait_recv()
                _remote(got, got, send.at[a, 3 + j], recv.at[a, 3 + j], (x, y, 1 - c)).start()
        for a in range(n):
            for j, (px, py) in enumerate(chips):
                kj = 2 * px + py
                _remote(outs[a].at[kj, other], outs[a].at[kj, other], send.at[a, 3 + j], recv.at[a, 3 + j], (x, y, 1 - c)).wait_recv()
            for j, (px, py) in enumerate(chips):
                kj = 2 * px + py
                _remote(ins[a].at[half], outs[a].at[k, half], send.at[a, j], recv.at[a, j], (px, py, c)).wait_send()
                _remote(outs[a].at[kj, half], outs[a].at[kj, half], send.at[a, 3 + j], recv.at[a, 3 + j], (x, y, 1 - c)).wait_send()

    return pl.pallas_call(
        body, name="gather_weights", in_specs=[HBM] * n, out_specs=[HBM] * n,
        out_shape=[jax.ShapeDtypeStruct((NCHIP,) + s.shape, s.dtype) for s in shards],
        scratch_shapes=[pltpu.SemaphoreType.DMA((n, 6)), pltpu.SemaphoreType.DMA((n, 6))],
    )(*shards)


def _swap_halves(parts):
    n = len(parts)
    hl = parts[0].shape[1] // 2

    def body(*refs):
        ins, outs = refs[:n], refs[n:2 * n]
        send, recv = refs[2 * n:]
        x, y, c, _ = _mesh_pos()
        cps = [_remote(ins[a].at[:, pl.ds((1 - c) * hl, hl)], outs[a], send.at[a], recv.at[a], (x, y, 1 - c)) for a in range(n)]
        for cp in cps:
            cp.start()
        for cp in cps:
            cp.wait()

    return pl.pallas_call(
        body, name="grad_swap_halves", in_specs=[HBM] * n, out_specs=[HBM] * n,
        out_shape=[jax.ShapeDtypeStruct((NCHIP, hl) + p.shape[2:], p.dtype) for p in parts],
        scratch_shapes=[pltpu.SemaphoreType.DMA((n,)), pltpu.SemaphoreType.DMA((n,))],
    )(*parts)


def _exchange_shards(parts):
    n = len(parts)

    def body(*refs):
        ins, outs = refs[:n], refs[n:2 * n]
        send, recv = refs[2 * n:]
        x, y, c, chips = _mesh_pos()
        k = 2 * x + y
        work = []
        for a in range(n):
            for j, (px, py) in enumerate(chips):
                cp = _remote(ins[a].at[2 * px + py], outs[a].at[k], send.at[a, j], recv.at[a, j], (px, py, c))
                cp.start()
                work.append(cp)
        for w in work:
            w.wait()

    return pl.pallas_call(
        body, name="grad_exchange", in_specs=[HBM] * n, out_specs=[HBM] * n,
        out_shape=[jax.ShapeDtypeStruct(p.shape, p.dtype) for p in parts],
        scratch_shapes=[pltpu.SemaphoreType.DMA((n, 3)), pltpu.SemaphoreType.DMA((n, 3))],
    )(*parts)


def _swap_reduced(halves):
    n = len(halves)

    def body(*refs):
        ins, outs = refs[:n], refs[n:2 * n]
        send, recv = refs[2 * n:]
        x, y, c, _ = _mesh_pos()
        work = [_remote(ins[a], outs[a], send.at[a], recv.at[a], (x, y, 1 - c)) for a in range(n)]
        for w in work:
            w.start()
        for w in work:
            w.wait()

    return pl.pallas_call(
        body, name="grad_swap_reduced", in_specs=[HBM] * n, out_specs=[HBM] * n,
        out_shape=[jax.ShapeDtypeStruct(h.shape, h.dtype) for h in halves],
        scratch_shapes=[pltpu.SemaphoreType.DMA((n,)), pltpu.SemaphoreType.DMA((n,))],
    )(*halves)


def _all_sum_small(vec, name):
    r = vec.shape[0]

    def body(v_ref, o_ref, buf, send, recv):
        x, y, c, _ = _mesh_pos()
        me = 4 * x + 2 * y + c
        buf[me] = v_ref[...]
        cps = []
        for mask in range(1, NDEV):
            fx, fy, fc = (mask >> 2) & 1, (mask >> 1) & 1, mask & 1
            peer = (1 - x if fx else x, 1 - y if fy else y, 1 - c if fc else c)
            cp = _remote(v_ref, buf.at[me], send.at[mask - 1], recv.at[mask - 1], peer)
            cp.start()
            cps.append(cp)
        for cp in cps:
            cp.wait()
        total = buf[0]
        for dev in range(1, NDEV):
            total = total + buf[dev]
        o_ref[...] = total

    vm = pl.BlockSpec(memory_space=pltpu.VMEM)
    return pl.pallas_call(
        body, name=name, in_specs=[vm], out_specs=vm, out_shape=jax.ShapeDtypeStruct((r, LANES), F32),
        scratch_shapes=[pltpu.VMEM((NDEV, r, LANES), F32), pltpu.SemaphoreType.DMA((NDEV - 1,)), pltpu.SemaphoreType.DMA((NDEV - 1,))],
    )(vec)


def _add_pairs(a, b, out_dtype, name):
    n0, n1, r, c = a.shape
    tr = _pick(r, max(8, (1 << 19) // c // 8 * 8), 8)

    def body(a_ref, b_ref, o_ref):
        o_ref[...] = (a_ref[...] + b_ref[...]).astype(out_dtype)

    blk = pl.BlockSpec((1, 1, tr, c), lambda i, j, t: (i, j, t, 0))
    return pl.pallas_call(
        body, name=name, grid=(n0, n1, r // tr), in_specs=[blk, blk], out_specs=blk,
        out_shape=jax.ShapeDtypeStruct(a.shape, out_dtype), compiler_params=_params("parallel", "parallel", "parallel"),
    )(a, b)


def _sum_chips(p, name):
    _, hl, r, c = p.shape
    tr = _pick(r, max(16, (1 << 19) // c // 16 * 16), 16)

    def body(p_ref, o_ref):
        total = p_ref[0].astype(F32)
        for j in range(1, NCHIP):
            total = total + p_ref[j].astype(F32)
        o_ref[...] = total

    return pl.pallas_call(
        body, name=name, grid=(hl, r // tr),
        in_specs=[pl.BlockSpec((NCHIP, 1, tr, c), lambda i, t: (0, i, t, 0))],
        out_specs=pl.BlockSpec((1, tr, c), lambda i, t: (i, t, 0)),
        out_shape=jax.ShapeDtypeStruct((hl, r, c), F32), compiler_params=_params("parallel", "parallel"),
    )(p)


def _reduce_scatter(parts):
    c = lax.axis_index("c")
    k = 2 * lax.axis_index("x") + lax.axis_index("y")
    hl = parts[0].shape[1] // 2
    theirs = _swap_halves(parts)
    mine = [lax.dynamic_slice_in_dim(p, c * hl, hl, axis=1) for p in parts]
    chip_sum = [_add_pairs(m, t, BF16, f"grad_pair_sum{i}") for i, (m, t) in enumerate(zip(mine, theirs))]
    gathered = _exchange_shards(chip_sum)
    gathered = [lax.dynamic_update_slice_in_dim(g, lax.dynamic_slice_in_dim(s, k, 1, axis=0), k, axis=0)
                for g, s in zip(gathered, chip_sum)]
    halves = [_sum_chips(g, f"grad_chip_sum{i}") for i, g in enumerate(gathered)]
    sibling = _swap_reduced(halves)
    south = c == 0
    return [jnp.concatenate([jnp.where(south, h, s), jnp.where(south, s, h)], axis=0) for h, s in zip(halves, sibling)]


SMALL = ("norm_w", "conv_w", "conv_b", "dt_bias", "a_log", "d_skip", "ssd_norm_w", "final_norm_w")


def _pack(arrays):
    flat = jnp.concatenate([a.reshape(-1).astype(F32) for a in arrays])
    rows = -(-flat.shape[0] // (8 * LANES)) * 8
    return jnp.pad(flat, (0, rows * LANES - flat.shape[0])).reshape(rows, LANES)


def _unpack(vec, shapes):
    flat = vec.reshape(-1)
    out, pos = [], 0
    for s in shapes:
        n = math.prod(s)
        out.append(flat[pos:pos + n].reshape(s))
        pos += n
    return out


def kernel(x, norm_w, w_in, conv_w, conv_b, dt_bias, a_log, d_skip, ssd_norm_w, w_out, final_norm_w, loss_target, m_norm_w, m_w_in, m_conv_w, m_conv_b, m_dt_bias, m_a_log, m_d_skip, m_ssd_norm_w, m_w_out, m_final_norm_w, v_norm_w, v_w_in, v_conv_w, v_conv_b, v_dt_bias, v_a_log, v_d_skip, v_ssd_norm_w, v_w_out, v_final_norm_w):
    depth, d, ics = w_in.shape
    cs = conv_w.shape[2]
    xi, yi, ci = lax.axis_index("x"), lax.axis_index("y"), lax.axis_index("c")
    k = 2 * xi + yi

    placed = lax.dynamic_update_slice(jnp.zeros((depth, KCONV, NCHIP, cs), F32), conv_w[:, :, None, :], (0, 0, k, 0))
    placed = jnp.where(ci == 0, placed, 0.0)
    conv_full = _unpack(_all_sum_small(_pack([placed]), "gather_conv_w"), [(depth, KCONV, NCHIP * cs)])[0]

    w_in16, w_out16 = w_in.astype(BF16), w_out.astype(BF16)
    wi_all, wo_all = _gather_weights([w_in16, w_out16])
    wi_all = lax.dynamic_update_slice_in_dim(wi_all, w_in16[None], k, axis=0)
    wo_all = lax.dynamic_update_slice_in_dim(wo_all, w_out16[None], k, axis=0)
    w_in_full = wi_all.transpose(1, 2, 0, 3).reshape(depth, d, NCHIP * ics)
    w_out_full = wo_all.transpose(1, 0, 2, 3).reshape(depth, 2 * d, d)

    small = dict(norm_w=norm_w, conv_b=conv_b, dt_bias=dt_bias, a_log=a_log, d_skip=d_skip, ssd_norm_w=ssd_norm_w, final_norm_w=final_norm_w)
    loss_local, gx, g = _local_step(x[0], loss_target[0], w_in_full, w_out_full, conv_full, small)

    g_in = g["w_in"].reshape(depth, d, NCHIP, ics).transpose(2, 0, 1, 3)
    g_out = g["w_out"].reshape(depth, NCHIP, 2 * d // NCHIP, d).transpose(1, 0, 2, 3)
    grad_w_in, grad_w_out = _reduce_scatter([g_in, g_out])

    names = list(SMALL)
    total = _all_sum_small(_pack([g[n] for n in names] + [loss_local]), "sum_small_grads")
    parts = _unpack(total, [g[n].shape for n in names] + [()])
    grads = dict(zip(names, parts[:-1]))
    loss = parts[-1]
    grads["conv_w"] = lax.dynamic_index_in_dim(grads["conv_w"].reshape(depth, KCONV, NCHIP, cs), k, axis=2, keepdims=False)
    grads["w_in"], grads["w_out"] = grad_w_in, grad_w_out

    w = dict(norm_w=norm_w, w_in=w_in, conv_w=conv_w, conv_b=conv_b, dt_bias=dt_bias, a_log=a_log, d_skip=d_skip,
             ssd_norm_w=ssd_norm_w, w_out=w_out, final_norm_w=final_norm_w)
    m = dict(norm_w=m_norm_w, w_in=m_w_in, conv_w=m_conv_w, conv_b=m_conv_b, dt_bias=m_dt_bias, a_log=m_a_log, d_skip=m_d_skip,
             ssd_norm_w=m_ssd_norm_w, w_out=m_w_out, final_norm_w=m_final_norm_w)
    v = dict(norm_w=v_norm_w, w_in=v_w_in, conv_w=v_conv_w, conv_b=v_conv_b, dt_bias=v_dt_bias, a_log=v_a_log, d_skip=v_d_skip,
             ssd_norm_w=v_ssd_norm_w, w_out=v_w_out, final_norm_w=v_final_norm_w)
    delta, new_m, new_v = {}, {}, {}
    for n in ("w_in", "w_out"):
        delta[n], new_m[n], new_v[n] = _adamw(w[n], grads[n], m[n], v[n], f"adamw_{n}")
    shapes = [w[n].shape for n in names]
    packed = [_pack([t[n] for n in names])[None] for t in (w, grads, m, v)]
    for res, out in zip(_adamw(*packed, "adamw_small"), (delta, new_m, new_v)):
        out.update(zip(names, _unpack(res[0], shapes)))

    order = ("norm_w", "w_in", "conv_w", "conv_b", "dt_bias", "a_log", "d_skip", "ssd_norm_w", "w_out", "final_norm_w")
    return (loss, gx[None], *[grads[n] for n in order], *[delta[n] for n in order], *[new_m[n] for n in order], *[new_v[n] for n in order])
```

```python
import functools
import math

import jax
import jax.numpy as jnp
from jax import lax
from jax.experimental import pallas as pl
from jax.experimental.pallas import tpu as pltpu

F32, BF16 = jnp.float32, jnp.bfloat16
EPS = 1e-6
CHUNK = 64
NGROUPS = 4
NSTATE = 128
KCONV = 4
SSD_HD = 64
SSD_HD_SHIFT = SSD_HD.bit_length() - 1
SBA_HD = 128
LANES = 128
VMEM_LIMIT = 56 * 1024 * 1024

ADAM_LR, ADAM_B1, ADAM_B2, ADAM_EPS, ADAM_WD, ADAM_STEP = 0.001, 0.9, 0.999, 1e-08, 0.01, 10

NN = ((1,), (0,))
NT = ((1,), (1,))
TN = ((0,), (0,))
MESH = pl.DeviceIdType.MESH


def _dot(a, b, dims):
    return lax.dot_general(a, b, (dims, ((), ())), preferred_element_type=F32)


def _params(*sem):
    return pltpu.CompilerParams(dimension_semantics=sem, vmem_limit_bytes=VMEM_LIMIT)


def _pick(n, target, mult):
    best = None
    for d in range(mult, min(n, target) + 1, mult):
        if n % d == 0:
            best = d
    return n if best is None else best


def _split3(x):
    x1 = x.astype(BF16)
    r1 = x - x1.astype(F32)
    x2 = r1.astype(BF16)
    x3 = (r1 - x2.astype(F32)).astype(BF16)
    return x1, x2, x3


def _split2(x):
    x1 = x.astype(BF16)
    return x1, (x - x1.astype(F32)).astype(BF16)


def _sigmoid(x):
    return 1.0 / (1.0 + jnp.exp(-x))


def _softplus(x):
    return jnp.maximum(x, 0.0) + jnp.log1p(jnp.exp(-jnp.abs(x)))


def _matmul(a, b, mode, out_dtype, name, add=None, tm=1024, tn=1024, tk=2048):
    if mode == "nn":
        (m, k), n = a.shape, b.shape[1]
    elif mode == "nt":
        (m, k), n = a.shape, b.shape[0]
    else:
        (k, m), n = a.shape, b.shape[1]
    tm, tn, tk = _pick(m, tm, LANES), _pick(n, tn, LANES), _pick(k, tk, LANES)
    nk = k // tk
    dims = {"nn": NN, "nt": NT, "tn": TN}[mode]
    a_spec = pl.BlockSpec((tk, tm), lambda i, j, kk: (kk, i)) if mode == "tn" else pl.BlockSpec((tm, tk), lambda i, j, kk: (i, kk))
    b_spec = pl.BlockSpec((tn, tk), lambda i, j, kk: (j, kk)) if mode == "nt" else pl.BlockSpec((tk, tn), lambda i, j, kk: (kk, j))
    o_spec = pl.BlockSpec((tm, tn), lambda i, j, kk: (i, j))
    has_add = add is not None

    def body(*refs):
        a_ref, b_ref = refs[0], refs[1]
        add_ref = refs[2] if has_add else None
        o_ref, acc_ref = refs[-2], refs[-1]
        kk = pl.program_id(2)
        part = _dot(a_ref[...], b_ref[...], dims)

        def finish(total):
            if has_add:
                total = total + add_ref[...].astype(F32)
            o_ref[...] = total.astype(out_dtype)

        if nk == 1:
            finish(part)
        else:
            @pl.when(kk == 0)
            def _():
                acc_ref[...] = part

            @pl.when(jnp.logical_and(kk > 0, kk < nk - 1))
            def _():
                acc_ref[...] += part

            @pl.when(kk == nk - 1)
            def _():
                finish(acc_ref[...] + part)

    in_specs = [a_spec, b_spec] + ([o_spec] if has_add else [])
    args = (a, b) + ((add,) if has_add else ())
    return pl.pallas_call(
        body, name=name, grid=(m // tm, n // tn, nk), in_specs=in_specs, out_specs=o_spec,
        out_shape=jax.ShapeDtypeStruct((m, n), out_dtype),
        scratch_shapes=[pltpu.VMEM((tm, tn), F32)],
        compiler_params=_params("parallel", "parallel", "arbitrary"),
    )(*args)


def _rmsnorm_fwd(x, w, name):
    l, d = x.shape
    tr = _pick(l, 512, 8)

    def body(x_ref, w_ref, h_ref):
        xv = x_ref[...]
        r = lax.rsqrt(jnp.mean(xv * xv, axis=-1, keepdims=True) + EPS)
        h_ref[...] = (xv * r * w_ref[...]).astype(BF16)

    return pl.pallas_call(
        body, name=name, grid=(l // tr,),
        in_specs=[pl.BlockSpec((tr, d), lambda i: (i, 0)), pl.BlockSpec((1, d), lambda i: (0, 0))],
        out_specs=pl.BlockSpec((tr, d), lambda i: (i, 0)),
        out_shape=jax.ShapeDtypeStruct((l, d), BF16), compiler_params=_params("parallel"),
    )(x, w)


def _rmsnorm_bwd(dh, x, w, dres, name):
    l, d = x.shape
    tr = _pick(l, 256, 8)

    def body(dh_ref, x_ref, w_ref, dres_ref, dx_ref, dw_ref):
        xv = x_ref[...]
        r = lax.rsqrt(jnp.mean(xv * xv, axis=-1, keepdims=True) + EPS)
        xh = xv * r
        dhv = dh_ref[...]
        dxh = dhv * w_ref[...]
        dx_ref[...] = dres_ref[...] + r * (dxh - xh * jnp.mean(dxh * xh, axis=-1, keepdims=True))
        part = jnp.sum(dhv * xh, axis=0, keepdims=True)

        @pl.when(pl.program_id(0) == 0)
        def _():
            dw_ref[...] = part

        @pl.when(pl.program_id(0) > 0)
        def _():
            dw_ref[...] += part

    row = pl.BlockSpec((tr, d), lambda i: (i, 0))
    vec = pl.BlockSpec((1, d), lambda i: (0, 0))
    return pl.pallas_call(
        body, name=name, grid=(l // tr,), in_specs=[row, row, vec, row], out_specs=[row, vec],
        out_shape=[jax.ShapeDtypeStruct((l, d), F32), jax.ShapeDtypeStruct((1, d), F32)],
        compiler_params=_params("arbitrary"),
    )(dh, x, w, dres)


def _final_loss(h, w, target, name):
    l, d = h.shape
    tr = _pick(l, 256, 8)

    def body(h_ref, w_ref, t_ref, dh_ref, dw_ref, loss_ref):
        xv = h_ref[...]
        r = lax.rsqrt(jnp.mean(xv * xv, axis=-1, keepdims=True) + EPS)
        xh = xv * r
        err = xh * w_ref[...] - t_ref[...]
        dy = err * (1.0 / d)
        dxh = dy * w_ref[...]
        dh_ref[...] = r * (dxh - xh * jnp.mean(dxh * xh, axis=-1, keepdims=True))
        part = jnp.sum(dy * xh, axis=0, keepdims=True)
        lpart = jnp.zeros((8, LANES), F32) + 0.5 * jnp.sum(jnp.mean(err * err, axis=-1, keepdims=True))

        @pl.when(pl.program_id(0) == 0)
        def _():
            dw_ref[...] = part
            loss_ref[...] = lpart

        @pl.when(pl.program_id(0) > 0)
        def _():
            dw_ref[...] += part
            loss_ref[...] += lpart

    row = pl.BlockSpec((tr, d), lambda i: (i, 0))
    vec = pl.BlockSpec((1, d), lambda i: (0, 0))
    return pl.pallas_call(
        body, name=name, grid=(l // tr,), in_specs=[row, vec, row],
        out_specs=[row, vec, pl.BlockSpec((8, LANES), lambda i: (0, 0))],
        out_shape=[jax.ShapeDtypeStruct((l, d), F32), jax.ShapeDtypeStruct((1, d), F32), jax.ShapeDtypeStruct((8, LANES), F32)],
        compiler_params=_params("arbitrary"),
    )(h, w, target)


def _gatenorm_fwd(y, proj_a, w, name):
    l, d = y.shape
    dg = d // NGROUPS
    tr = _pick(l, 256, 8)

    def body(y_ref, z_ref, w_ref, o_ref):
        for g in range(NGROUPS):
            sl = slice(g * dg, (g + 1) * dg)
            zv = z_ref[:, sl]
            u = y_ref[:, sl] * (zv * _sigmoid(zv))
            r = lax.rsqrt(jnp.mean(u * u, axis=-1, keepdims=True) + EPS)
            o_ref[:, sl] = (u * r * w_ref[:, sl]).astype(BF16)

    row = pl.BlockSpec((tr, d), lambda i: (i, 0))
    return pl.pallas_call(
        body, name=name, grid=(l // tr,), in_specs=[row, row, pl.BlockSpec((1, d), lambda i: (0, 0))],
        out_specs=row, out_shape=jax.ShapeDtypeStruct((l, d), BF16), compiler_params=_params("parallel"),
    )(y, proj_a, w)


def _gatenorm_bwd(dout, y, proj_a, w, name):
    l, d = y.shape
    dg = d // NGROUPS
    tr = _pick(l, 256, 8)

    def body(do_ref, y_ref, z_ref, w_ref, dy_ref, dz_ref, dw_ref):
        first = pl.program_id(0) == 0
        for g in range(NGROUPS):
            sl = slice(g * dg, (g + 1) * dg)
            zv = z_ref[:, sl]
            sg = _sigmoid(zv)
            sz = zv * sg
            yv = y_ref[:, sl]
            u = yv * sz
            r = lax.rsqrt(jnp.mean(u * u, axis=-1, keepdims=True) + EPS)
            nh = u * r
            dov = do_ref[:, sl]
            dn = dov * w_ref[:, sl]
            du = r * (dn - nh * jnp.mean(dn * nh, axis=-1, keepdims=True))
            dy_ref[:, sl] = du * sz
            dz_ref[:, sl] = (du * yv * (sg * (1.0 + zv * (1.0 - sg)))).astype(BF16)
            part = jnp.sum(dov * nh, axis=0, keepdims=True)

            @pl.when(first)
            def _():
                dw_ref[:, sl] = part

            @pl.when(jnp.logical_not(first))
            def _():
                dw_ref[:, sl] += part

    row = pl.BlockSpec((tr, d), lambda i: (i, 0))
    vec = pl.BlockSpec((1, d), lambda i: (0, 0))
    return pl.pallas_call(
        body, name=name, grid=(l // tr,), in_specs=[row, row, row, vec], out_specs=[row, row, vec],
        out_shape=[jax.ShapeDtypeStruct((l, d), F32), jax.ShapeDtypeStruct((l, d), BF16), jax.ShapeDtypeStruct((1, d), F32)],
        compiler_params=_params("arbitrary"),
    )(dout, y, proj_a, w)


def _conv_cols(d):
    return _pick(math.gcd(d, d + 2 * NGROUPS * NSTATE), 512, LANES)


def _conv_fwd(proj_a, conv_w, conv_b, d, name):
    l = proj_a.shape[0]
    cdim = d + 2 * NGROUPS * NSTATE
    cw = _conv_cols(d)
    off = d // cw
    tl = _pick(l, 512, 8)
    hb = tl // 8

    def body(u_ref, up_ref, w_ref, b_ref, acc_ref, xbc_ref, ext):
        i = pl.program_id(1)
        ext[0:8, :] = jnp.where(i > 0, up_ref[...], 0.0)
        ext[8:8 + tl, :] = u_ref[...]
        acc = jnp.zeros((tl, cw), F32) + b_ref[...]
        for j in range(KCONV):
            acc = acc + w_ref[j:j + 1, :] * ext[pl.ds(8 - (KCONV - 1) + j, tl), :]
        acc_ref[...] = acc
        xbc_ref[...] = acc * _sigmoid(acc)

    blk = pl.BlockSpec((tl, cw), lambda c, i: (i, c))
    return pl.pallas_call(
        body, name=name, grid=(cdim // cw, l // tl),
        in_specs=[pl.BlockSpec((tl, cw), lambda c, i: (i, c + off)),
                  pl.BlockSpec((8, cw), lambda c, i: (jnp.maximum(i * hb - 1, 0), c + off)),
                  pl.BlockSpec((KCONV, cw), lambda c, i: (0, c)), pl.BlockSpec((1, cw), lambda c, i: (0, c))],
        out_specs=[blk, blk],
        out_shape=[jax.ShapeDtypeStruct((l, cdim), F32), jax.ShapeDtypeStruct((l, cdim), F32)],
        scratch_shapes=[pltpu.VMEM((tl + 8, cw), F32)], compiler_params=_params("parallel", "parallel"),
    )(proj_a, proj_a, conv_w, conv_b)


def _conv_bwd(dxbc, acc, proj_a, conv_w, d, name):
    l, cdim = acc.shape
    cw = _conv_cols(d)
    off = d // cw
    tl = _pick(l, 512, 8)
    hb = tl // 8
    nb = l // tl

    def dsilu(g, a):
        s = _sigmoid(a)
        return g * (s * (1.0 + a * (1.0 - s)))

    def body(g_ref, gn_ref, a_ref, an_ref, u_ref, up_ref, w_ref, du_ref, dw_ref, ext, dext):
        i = pl.program_id(1)
        da = dsilu(g_ref[...], a_ref[...])
        dext[0:tl, :] = da
        dext[tl:tl + 8, :] = jnp.where(i < nb - 1, dsilu(gn_ref[...], an_ref[...]), 0.0)
        ext[0:8, :] = jnp.where(i > 0, up_ref[...], 0.0)
        ext[8:8 + tl, :] = u_ref[...]
        du = jnp.zeros((tl, cw), F32)
        rows = []
        for j in range(KCONV):
            du = du + w_ref[j:j + 1, :] * dext[pl.ds(KCONV - 1 - j, tl), :]
            rows.append(jnp.sum(da * ext[pl.ds(8 - (KCONV - 1) + j, tl), :], axis=0, keepdims=True))
        rows.append(jnp.sum(da, axis=0, keepdims=True))
        du_ref[...] = du.astype(BF16)

        @pl.when(i == 0)
        def _():
            dw_ref[...] = jnp.zeros_like(dw_ref)

        for j, rv in enumerate(rows):
            dw_ref[j:j + 1, :] += rv

    blk = pl.BlockSpec((tl, cw), lambda c, i: (i, c))
    nxt = pl.BlockSpec((8, cw), lambda c, i: (jnp.minimum((i + 1) * hb, l // 8 - 1), c))
    return pl.pallas_call(
        body, name=name, grid=(cdim // cw, nb),
        in_specs=[blk, nxt, blk, nxt,
                  pl.BlockSpec((tl, cw), lambda c, i: (i, c + off)),
                  pl.BlockSpec((8, cw), lambda c, i: (jnp.maximum(i * hb - 1, 0), c + off)),
                  pl.BlockSpec((KCONV, cw), lambda c, i: (0, c))],
        out_specs=[blk, pl.BlockSpec((8, cw), lambda c, i: (0, c))],
        out_shape=[jax.ShapeDtypeStruct((l, cdim), BF16), jax.ShapeDtypeStruct((8, cdim), F32)],
        scratch_shapes=[pltpu.VMEM((tl + 8, cw), F32), pltpu.VMEM((tl + 8, cw), F32)],
        compiler_params=_params("parallel", "arbitrary"),
    )(dxbc, dxbc, acc, acc, proj_a, proj_a, conv_w)


def _ssd_common(dt_ref, bias_ref, alog_ref):
    li = lax.broadcasted_iota(jnp.int32, (CHUNK, CHUNK), 0)
    si = lax.broadcasted_iota(jnp.int32, (CHUNK, CHUNK), 1)
    tri = si <= li
    dtv = _softplus(dt_ref[...] + bias_ref[0])
    a_neg = -jnp.exp(alog_ref[0])
    da = dtv * a_neg
    cs_col = sum(_dot(tri.astype(BF16), p, NN) for p in _split3(da))
    cs_row = sum(_dot(p, (li <= si).astype(BF16), TN) for p in _split3(da))
    return tri, dtv, a_neg, cs_col, cs_row


def _dot2(x, t16, dims):
    hi, lo = _split2(x)
    if dims == NN:
        return _dot(jnp.concatenate([hi, lo], axis=1), jnp.concatenate([t16, t16], axis=0), NN)
    return _dot(hi, t16, dims) + _dot(lo, t16, dims)


def _ssd_expand(dtv, cs_col, dsk, pg):
    expm = (lax.shift_right_logical(lax.broadcasted_iota(jnp.int32, (LANES, pg), 1), SSD_HD_SHIFT)
            == lax.broadcasted_iota(jnp.int32, (LANES, pg), 0)).astype(BF16)
    tot = cs_col[CHUNK - 1:CHUNK, :]
    stack = jnp.concatenate([dtv, jnp.exp(cs_col), jnp.exp(tot - cs_col), jnp.broadcast_to(dsk, (CHUNK, LANES)),
                             jnp.broadcast_to(jnp.exp(tot), (CHUNK, LANES))], axis=0)
    ex = _dot2(stack, expm, NN)
    return tuple(ex[CHUNK * a:CHUNK * (a + 1)] for a in range(5))


def _ssd_specs(d, r_heads):
    pg = r_heads * SSD_HD
    nb = d // LANES
    dt_blk = (2 * d + 2 * NGROUPS * NSTATE) // LANES
    x_spec = lambda cmap: pl.BlockSpec((CHUNK, pg), lambda g, c: (cmap(c), g))
    b_spec = lambda cmap: pl.BlockSpec((CHUNK, NSTATE), lambda g, c: (cmap(c), nb + g))
    c_spec = lambda cmap: pl.BlockSpec((CHUNK, NSTATE), lambda g, c: (cmap(c), nb + NGROUPS + g))
    dt_spec = lambda cmap: pl.BlockSpec((CHUNK, LANES), lambda g, c: (cmap(c), dt_blk + g))
    const = pl.BlockSpec((1, 1, LANES), lambda g, c: (g, 0, 0))
    return pg, x_spec, b_spec, c_spec, dt_spec, const


def _ssd_fwd(xbc, proj_a, bias_g, alog_g, dsk_g, d, name):
    l = xbc.shape[0]
    nc = l // CHUNK
    r_heads = d // SSD_HD // NGROUPS
    pg, x_spec, b_spec, c_spec, dt_spec, const = _ssd_specs(d, r_heads)
    ident = lambda c: c

    def body(x_ref, b_ref, c_ref, dt_ref, bias_ref, alog_ref, dsk_ref, y_ref, hs_ref, h_scr):
        @pl.when(pl.program_id(1) == 0)
        def _():
            h_scr[...] = jnp.zeros_like(h_scr)

        tri, dtv, _, cs_col, cs_row = _ssd_common(dt_ref, bias_ref, alog_ref)
        hin = h_scr[...]
        hs_ref[0, 0] = hin
        bm = b_ref[...].astype(BF16)
        cm = c_ref[...].astype(BF16)
        xv = x_ref[...]
        e_dt, e_ecs, e_decs, e_dsk, e_etot = _ssd_expand(dtv, cs_col, dsk_ref[0], pg)
        xd = xv * e_dt
        xd16 = xd.astype(BF16)
        gmat = _dot(cm, bm, NT)
        for r in range(r_heads):
            sl = slice(SSD_HD * r, SSD_HD * (r + 1))
            lm = jnp.exp(jnp.where(tri, cs_col[:, r:r + 1] - cs_row[r:r + 1, :], -jnp.inf))
            y_ref[:, sl] = _dot((gmat * lm).astype(BF16), xd16[:, sl], NN)
        y_ref[...] += e_ecs * _dot(cm, hin.astype(BF16), NN) + e_dsk * xv
        h_scr[...] = hin * e_etot[0:1] + _dot(bm, (xd * e_decs).astype(BF16), TN)

    return pl.pallas_call(
        body, name=name, grid=(NGROUPS, nc),
        in_specs=[x_spec(ident), b_spec(ident), c_spec(ident), dt_spec(ident), const, const, const],
        out_specs=[x_spec(ident), pl.BlockSpec((1, 1, NSTATE, pg), lambda g, c: (c, g, 0, 0))],
        out_shape=[jax.ShapeDtypeStruct((l, d), F32), jax.ShapeDtypeStruct((nc, NGROUPS, NSTATE, pg), F32)],
        scratch_shapes=[pltpu.VMEM((NSTATE, pg), F32)], compiler_params=_params("parallel", "arbitrary"),
    )(xbc, xbc, xbc, proj_a, bias_g, alog_g, dsk_g)


def _ssd_bwd(dy, xbc, proj_a, hs, bias_g, alog_g, dsk_g, d, name):
    l = xbc.shape[0]
    nc = l // CHUNK
    r_heads = d // SSD_HD // NGROUPS
    pg, x_spec, b_spec, c_spec, dt_spec, const = _ssd_specs(d, r_heads)
    rev = lambda c: nc - 1 - c
    cdim = d + 2 * NGROUPS * NSTATE

    def body(dy_ref, x_ref, b_ref, c_ref, dt_ref, hs_ref, bias_ref, alog_ref, dsk_ref,
             dx_ref, db_ref, dc_ref, ddt_ref, sums_ref, dh_scr, p_scr, pt_scr, dxd_scr):
        first = pl.program_id(1) == 0

        @pl.when(first)
        def _():
            dh_scr[...] = jnp.zeros_like(dh_scr)
            sums_ref[...] = jnp.zeros_like(sums_ref)

        tri, dtv, a_neg, cs_col, cs_row = _ssd_common(dt_ref, bias_ref, alog_ref)
        li = lax.broadcasted_iota(jnp.int32, (CHUNK, CHUNK), 0)
        si = lax.broadcasted_iota(jnp.int32, (CHUNK, CHUNK), 1)
        tri_t = li <= si
        lastrow = lax.broadcasted_iota(jnp.int32, (CHUNK, 1), 0) == CHUNK - 1
        indm = (lax.shift_right_logical(lax.broadcasted_iota(jnp.int32, (pg, LANES), 0), SSD_HD_SHIFT)
                == lax.broadcasted_iota(jnp.int32, (pg, LANES), 1)).astype(BF16)
        hin = hs_ref[0, 0]
        dhout = dh_scr[...]
        hin16 = hin.astype(BF16)
        dhout16 = dhout.astype(BF16)
        bm = b_ref[...].astype(BF16)
        cm = c_ref[...].astype(BF16)
        xv = x_ref[...]
        dyv = dy_ref[...]
        e_dt, e_ecs, e_decs, e_dsk, e_etot = _ssd_expand(dtv, cs_col, dsk_ref[0], pg)
        xd = xv * e_dt
        xd16 = xd.astype(BF16)
        dy16 = dyv.astype(BF16)
        dye = dyv * e_ecs
        dye16 = dye.astype(BF16)
        gmat = _dot(cm, bm, NT)
        gmat_t = _dot(bm, cm, NT)
        dg = jnp.zeros((CHUNK, CHUNK), F32)
        dg_t = jnp.zeros((CHUNK, CHUNK), F32)
        for r in range(r_heads):
            sl = slice(SSD_HD * r, SSD_HD * (r + 1))
            col = cs_col[:, r:r + 1]
            row = cs_row[r:r + 1, :]
            lm = jnp.exp(jnp.where(tri, col - row, -jnp.inf))
            lm_t = jnp.exp(jnp.where(tri_t, row - col, -jnp.inf))
            dm = _dot(dy16[:, sl], xd16[:, sl], NT)
            dm_t = _dot(xd16[:, sl], dy16[:, sl], NT)
            dg = dg + dm * lm
            dg_t = dg_t + dm_t * lm_t
            m_t = gmat_t * lm_t
            p_scr[:, sl] = dm * (gmat * lm)
            pt_scr[:, sl] = dm_t * m_t
            dxd_scr[:, sl] = _dot(m_t.astype(BF16), dy16[:, sl], NN)
        yoff = _dot(cm, hin16, NN)
        qall = _dot(bm, dhout16, NN)
        dxd = dxd_scr[...] + qall * e_decs
        hh = jnp.broadcast_to(jnp.sum(dhout * hin, axis=0, keepdims=True), (8, pg))
        red = _dot2(jnp.concatenate([dye * yoff, qall * xd, dxd * xv, dyv * xv, p_scr[...], pt_scr[...], hh], axis=0), indm, NN)
        r_yoff, r_q, r_dt, r_dsk, r_p, r_pt = (red[CHUNK * a:CHUNK * (a + 1)] for a in range(6))
        tot = cs_col[CHUNK - 1:CHUNK, :]
        ddec = r_q * jnp.exp(tot - cs_col)
        dtot = jnp.sum(ddec, axis=0, keepdims=True) + jnp.exp(tot) * red[6 * CHUNK:6 * CHUNK + 1]
        dcs = r_p - r_pt + r_yoff - ddec + jnp.where(lastrow, dtot, 0.0)
        dda = sum(_dot(tri_t.astype(BF16), p, NN) for p in _split3(dcs))
        draw = (r_dt + dda * a_neg) * _sigmoid(dt_ref[...] + bias_ref[0])
        ddt_ref[...] = draw
        sums_ref[0, 0:1, :] += jnp.sum(draw, axis=0, keepdims=True)
        sums_ref[0, 1:2, :] += jnp.sum(dda * dtv, axis=0, keepdims=True)
        sums_ref[0, 2:3, :] += jnp.sum(r_dsk, axis=0, keepdims=True)
        dx_ref[...] = dxd * e_dt + e_dsk * dyv
        dc_ref[...] = _dot(dg.astype(BF16), bm, NN) + _dot(dye16, hin16, NT)
        db_ref[...] = _dot(dg_t.astype(BF16), cm, NN) + _dot((xd * e_decs).astype(BF16), dhout16, NT)
        dh_scr[...] = dhout * e_etot[0:1] + _dot(cm, dye16, TN)

    grp = pl.BlockSpec((CHUNK, LANES), lambda g, c: (rev(c), g))
    return pl.pallas_call(
        body, name=name, grid=(NGROUPS, nc),
        in_specs=[x_spec(rev), x_spec(rev), b_spec(rev), c_spec(rev), dt_spec(rev),
                  pl.BlockSpec((1, 1, NSTATE, pg), lambda g, c: (rev(c), g, 0, 0)), const, const, const],
        out_specs=[x_spec(rev), grp, grp, grp, pl.BlockSpec((1, 8, LANES), lambda g, c: (g, 0, 0))],
        out_shape=[jax.ShapeDtypeStruct((l, d), F32), jax.ShapeDtypeStruct((l, NGROUPS * NSTATE), F32),
                   jax.ShapeDtypeStruct((l, NGROUPS * NSTATE), F32),
                   jax.ShapeDtypeStruct((l, NGROUPS * LANES), F32), jax.ShapeDtypeStruct((NGROUPS, 8, LANES), F32)],
        scratch_shapes=[pltpu.VMEM((NSTATE, pg), F32), pltpu.VMEM((CHUNK, pg), F32), pltpu.VMEM((CHUNK, pg), F32),
                        pltpu.VMEM((CHUNK, pg), F32)],
        compiler_params=_params("parallel", "arbitrary"),
    )(dy, xbc, xbc, xbc, proj_a, hs, bias_g, alog_g, dsk_g)


LOG2E = 1.4426950408889634
SBA_TK = 256
SBA_TQ_FWD = 1024
SBA_TQ_BWD = 1024
SBA_ROW_PARTS = 8


def _sba_tiles(l, tq_target):
    tk = _pick(l, SBA_TK, LANES)
    tq = _pick(l, tq_target, tk)
    assert l // tk <= LANES
    return tq, tk


def _sba_scores(qk, valid, scale):
    z2 = qk * (scale * LOG2E)
    t2 = jnp.log2(1.0 + jnp.exp2(-jnp.abs(z2)))
    la = jnp.minimum(z2, 0.0) - t2
    lk = la - z2
    if valid is not None:
        lk = jnp.where(valid, lk, 0.0)
    return lk, la


def _sba_fwd(proj_a, proj_c, d, g_off, name):
    l = proj_c.shape[0]
    nh = d // SBA_HD
    tq, tk = _sba_tiles(l, SBA_TQ_FWD)
    band = tq // tk
    scale = 1.0 / math.sqrt(SBA_HD)
    rq = tq // SBA_ROW_PARTS if tq % (SBA_ROW_PARTS * 16) == 0 else tq
    parts = [pl.ds(p * rq, rq) for p in range(tq // rq)]

    def body(q_ref, k_ref, v_ref, g_ref, o_ref, y_ref, rs_ref, rs_scr, qk_a, qk_b):
        i = pl.program_id(1)
        ki = lax.broadcasted_iota(jnp.int32, (tk, tk), 0)
        kj = lax.broadcasted_iota(jnp.int32, (tk, tk), 1)
        uex = (ki > kj).astype(BF16)
        lane = lax.broadcasted_iota(jnp.int32, (rq, LANES), 1)
        rs_scr[...] = jnp.zeros_like(rs_scr)
        qs = [q_ref[ps, :] for ps in parts]

        def qk_into(j, qk_scr):
            kb = k_ref[pl.ds(pl.multiple_of(j * tk, tk), tk), :]
            for p, ps in enumerate(parts):
                qk_scr[ps, :] = _dot(qs[p], kb, NT)

        def tile(j, carry, band_pos=None, qk_scr=None):
            start = pl.multiple_of(j * tk, tk)
            vb = v_ref[pl.ds(start, tk), :]
            live, valid = list(range(len(parts))), {}
            if band_pos is not None:
                k0, k1 = band_pos * tk, (band_pos + 1) * tk - 1
                live = [p for p in live if (p + 1) * rq - 1 > k0]
                rows = lax.broadcasted_iota(jnp.int32, (rq, tk), 0)
                cols = lax.broadcasted_iota(jnp.int32, (rq, tk), 1)
                valid = {p: cols - rows < p * rq - k0 for p in live if p * rq <= k1}
            if qk_scr is None:
                kb = k_ref[pl.ds(start, tk), :]
                qk = {p: _dot(qs[p], kb, NT) for p in live}
            else:
                qk = {p: qk_scr[parts[p], :] for p in live}
            sc = {p: _sba_scores(qk[p], valid.get(p), scale) for p in live}
            later = {p: _dot2(sc[p][0], uex, NN) for p in live}
            out = list(carry)
            for p in live:
                rsum, acc = carry[p]
                w = jnp.exp2(sc[p][1] + later[p] + rsum)
                if p in valid:
                    w = jnp.where(valid[p], w, 0.0)
                acc = acc + _dot(w.astype(BF16), vb, NN)
                rs_scr[parts[p], :] = jnp.where(lane == j, rsum, rs_scr[parts[p], :])
                out[p] = (rsum + later[p][:, 0:1] + sc[p][0][:, 0:1], acc)
            return tuple(out)

        carry = tuple((jnp.zeros((rq, 1), F32), jnp.zeros((rq, SBA_HD), F32)) for _ in parts)
        nfull = i * band
        for band_pos in reversed(range(band)):
            carry = tile(nfull + band_pos, carry, band_pos)
        if band % 2 == 0:
            @pl.when(nfull > 0)
            def _():
                qk_into(nfull - 1, qk_a)

            def pair(m, c):
                ja = nfull - 1 - 2 * m
                qk_into(ja - 1, qk_b)
                c = tile(ja, c, None, qk_a)
                qk_into(jnp.maximum(ja - 2, 0), qk_a)
                return tile(ja - 1, c, None, qk_b)

            carry = lax.fori_loop(0, nfull // 2, pair, carry)
        else:
            carry = lax.fori_loop(0, nfull, lambda jj, c: tile(nfull - 1 - jj, c), carry)
        rs_ref[...] = rs_scr[...]
        for p, ps in enumerate(parts):
            acc = carry[p][1]
            o_ref[ps, :] = acc
            gv = g_ref[ps, :]
            y_ref[ps, :] = (acc * (gv * _sigmoid(gv))).astype(BF16)

    goff = g_off // SBA_HD
    blk = lambda off: pl.BlockSpec((tq, SBA_HD), lambda h, i: (i, h + off))
    full = lambda off: pl.BlockSpec((l, SBA_HD), lambda h, i: (0, h + off))
    out = pl.BlockSpec((tq, SBA_HD), lambda h, i: (i, h))
    return pl.pallas_call(
        body, name=name, grid=(nh, l // tq), in_specs=[blk(0), full(nh), full(2 * nh), blk(goff)], out_specs=[out, out, out],
        scratch_shapes=[pltpu.VMEM((tq, LANES), F32), pltpu.VMEM((tq, tk), F32), pltpu.VMEM((tq, tk), F32)],
        out_shape=[jax.ShapeDtypeStruct((l, d), F32), jax.ShapeDtypeStruct((l, d), BF16), jax.ShapeDtypeStruct((l, d), F32)],
        compiler_params=_params("parallel", "arbitrary"),
    )(proj_c, proj_c, proj_c, proj_a)


def _sba_bwd(dys, o, rs, proj_a, proj_c, d, g_off, name):
    l = proj_c.shape[0]
    nh = d // SBA_HD
    tq, tk = _sba_tiles(l, SBA_TQ_BWD)
    band = tq // tk
    scale = 1.0 / math.sqrt(SBA_HD)

    def body(dy_ref, o_ref, rs_ref, q_ref, k_ref, v_ref, g_ref, dq_ref, dk_ref, dv_ref, dg_ref, dk_acc, dv_acc,
             qk_a, qk_b, dw_a, dw_b):
        i = pl.program_id(1)
        nq = pl.num_programs(1)

        @pl.when(i == 0)
        def _():
            dk_acc[...] = jnp.zeros_like(dk_acc)
            dv_acc[...] = jnp.zeros_like(dv_acc)

        ki = lax.broadcasted_iota(jnp.int32, (tk, tk), 0)
        kj = lax.broadcasted_iota(jnp.int32, (tk, tk), 1)
        uex = (ki > kj).astype(BF16)
        ulow = (ki < kj).astype(BF16)
        lane = lax.broadcasted_iota(jnp.int32, (tq, LANES), 1)
        q = q_ref[...]
        gv = g_ref[...]
        sg = _sigmoid(gv)
        dyv = dy_ref[...]
        dg_ref[...] = (dyv * o_ref[...] * (sg * (1.0 + gv * (1.0 - sg)))).astype(BF16)
        do16 = (dyv * (gv * sg)).astype(BF16)

        def ahead(j, qk_scr, dw_scr):
            start = pl.multiple_of(j * tk, tk)
            qk_scr[...] = _dot(q, k_ref[pl.ds(start, tk), :], NT)
            dw_scr[...] = _dot(do16, v_ref[pl.ds(start, tk), :], NT)

        def tile(j, carry, band_pos=None, qk_scr=None, dw_scr=None):
            r0 = 0 if band_pos is None else band_pos * tk
            epre, dq = carry[0][r0:], carry[1][r0:]
            qr, dor = q[r0:], do16[r0:]
            start = pl.multiple_of(j * tk, tk)
            kb = k_ref[pl.ds(start, tk), :]
            rsum = jnp.sum(jnp.where(lane[r0:] == j, rs_ref[r0:, :], 0.0), axis=1, keepdims=True)
            valid = None
            if band_pos is not None:
                valid = (lax.broadcasted_iota(jnp.int32, (tq - r0, tk), 1) < lax.broadcasted_iota(jnp.int32, (tq - r0, tk), 0))
            if qk_scr is None:
                qk = _dot(qr, kb, NT)
                dw = _dot(dor, v_ref[pl.ds(start, tk), :], NT)
            else:
                qk = qk_scr[...]
                dw = dw_scr[...]
            lk, la = _sba_scores(qk, valid, scale)
            later = _dot2(lk, uex, NN)
            w = jnp.exp2(la + later + rsum)
            if valid is not None:
                w = jnp.where(valid, w, 0.0)
            e = w * dw
            epx = _dot2(e, ulow, NN)
            dz = (e - jnp.exp2(la) * (e + epre + epx)) * scale
            if valid is not None:
                dz = jnp.where(valid, dz, 0.0)
            dz16 = dz.astype(BF16)
            dq = dq + _dot(dz16, kb, NN)
            dk_acc[pl.ds(start, tk), :] += _dot(dz16, qr, TN)
            dv_acc[pl.ds(start, tk), :] += _dot(w.astype(BF16), dor, TN)
            epre = epre + epx[:, tk - 1:tk] + e[:, tk - 1:tk]
            if r0:
                epre = jnp.concatenate([carry[0][:r0], epre], axis=0)
                dq = jnp.concatenate([carry[1][:r0], dq], axis=0)
            return epre, dq

        carry = (jnp.zeros((tq, 1), F32), jnp.zeros((tq, SBA_HD), F32))
        nfull = i * band
        if band % 2 == 0:
            @pl.when(nfull > 0)
            def _():
                ahead(0, qk_a, dw_a)

            def pair(m, c):
                ja = 2 * m
                ahead(ja + 1, qk_b, dw_b)
                c = tile(ja, c, None, qk_a, dw_a)
                ahead(jnp.minimum(ja + 2, nfull - 1), qk_a, dw_a)
                return tile(ja + 1, c, None, qk_b, dw_b)

            carry = lax.fori_loop(0, nfull // 2, pair, carry)
        else:
            carry = lax.fori_loop(0, nfull, lambda j, c: tile(j, c), carry)
        for band_pos in range(band):
            carry = tile(nfull + band_pos, carry, band_pos)
        dq_ref[...] = carry[1].astype(BF16)

        @pl.when(i == nq - 1)
        def _():
            dk_ref[...] = dk_acc[...].astype(BF16)
            dv_ref[...] = dv_acc[...].astype(BF16)

    goff = g_off // SBA_HD
    blk = lambda off: pl.BlockSpec((tq, SBA_HD), lambda h, i: (i, h + off))
    full = lambda off: pl.BlockSpec((l, SBA_HD), lambda h, i: (0, h + off))
    out = pl.BlockSpec((tq, SBA_HD), lambda h, i: (i, h))
    outfull = pl.BlockSpec((l, SBA_HD), lambda h, i: (0, h))
    sd = jax.ShapeDtypeStruct((l, d), BF16)
    return pl.pallas_call(
        body, name=name, grid=(nh, l // tq),
        in_specs=[blk(nh), out, out, blk(0), full(nh), full(2 * nh), blk(goff)],
        out_specs=[out, outfull, outfull, out], out_shape=[sd, sd, sd, sd],
        scratch_shapes=[pltpu.VMEM((l, SBA_HD), F32), pltpu.VMEM((l, SBA_HD), F32)] + [pltpu.VMEM((tq, tk), F32)] * 4,
        compiler_params=_params("parallel", "arbitrary"),
    )(dys, o, rs, proj_c, proj_c, proj_c, proj_a)


def _adamw_math(w, g, m, v):
    m = ADAM_B1 * m + (1.0 - ADAM_B1) * g
    v = ADAM_B2 * v + (1.0 - ADAM_B2) * (g * g)
    m_hat = m / (1.0 - ADAM_B1 ** ADAM_STEP)
    v_hat = v / (1.0 - ADAM_B2 ** ADAM_STEP)
    delta = -ADAM_LR * (m_hat / (jnp.sqrt(v_hat) + ADAM_EPS) + ADAM_WD * w)
    return delta, m, v


def _adamw(w, g, m, v, name):
    a, r, c = w.shape
    tr = _pick(r, max(8, (1 << 19) // c // 8 * 8), 8)

    def body(w_ref, g_ref, m_ref, v_ref, d_ref, nm_ref, nv_ref):
        dl, nm, nv = _adamw_math(w_ref[...], g_ref[...], m_ref[...], v_ref[...])
        d_ref[...] = dl
        nm_ref[...] = nm
        nv_ref[...] = nv

    blk = pl.BlockSpec((1, tr, c), lambda i, j: (i, j, 0))
    sd = jax.ShapeDtypeStruct(w.shape, F32)
    return pl.pallas_call(
        body, name=name, grid=(a, r // tr), in_specs=[blk] * 4, out_specs=[blk] * 3, out_shape=[sd] * 3,
        compiler_params=_params("parallel", "parallel"),
    )(w, g, m, v)


def _dims(d):
    cdim = d + 2 * NGROUPS * NSTATE
    heads = d // SSD_HD
    r_heads = heads // NGROUPS
    g_off = d + cdim + NGROUPS * LANES
    na = g_off + d
    nc = 3 * d
    return cdim, heads, r_heads, na, nc, g_off


def _pack_w_in(w_in, d):
    cdim, heads, r_heads, na, nc, g_off = _dims(d)
    o = d + cdim
    w_dt = w_in[:, o:o + heads].reshape(d, NGROUPS, r_heads)
    w_dt = jnp.pad(w_dt, ((0, 0), (0, 0), (0, LANES - r_heads))).reshape(d, NGROUPS * LANES)
    return jnp.concatenate([w_in[:, :o], w_dt, w_in[:, o + heads + nc:]], axis=1), w_in[:, o + heads:o + heads + nc]


def _unpack_w_in(ga, gc, d):
    cdim, heads, r_heads, na, nc, g_off = _dims(d)
    o = d + cdim
    g_dt = ga[:, o:g_off].reshape(d, NGROUPS, LANES)[:, :, :r_heads].reshape(d, heads)
    return jnp.concatenate([ga[:, :o], g_dt, gc, ga[:, g_off:]], axis=1)


def _group_vec(v, r_heads):
    return jnp.pad(v.reshape(NGROUPS, 1, r_heads), ((0, 0), (0, 0), (0, LANES - r_heads)))


def _layer_fwd(x, p, d, tag):
    cdim, heads, r_heads, na, nc, g_off = _dims(d)
    h = _rmsnorm_fwd(x, p["norm_w"], f"norm_f{tag}")
    proj_a = _matmul(h, p["wa"], "nn", F32, f"inproj_a{tag}")
    proj_c = _matmul(h, p["wc"], "nn", BF16, f"inproj_c{tag}")
    acc, xbc = _conv_fwd(proj_a, p["conv_w"], p["conv_b"], d, f"conv_f{tag}")
    y, hs = _ssd_fwd(xbc, proj_a, p["bias_g"], p["alog_g"], p["dsk_g"], d, f"ssd_f{tag}")
    y_ssd = _gatenorm_fwd(y, proj_a, p["ssd_norm_w"], f"gate_f{tag}")
    o, y_sba, rs = _sba_fwd(proj_a, proj_c, d, g_off, f"sba_f{tag}")
    mix = jnp.concatenate([y_ssd, y_sba], axis=1)
    x_next = _matmul(mix, p["w_out"], "nn", F32, f"outproj{tag}", add=x)
    return x_next, dict(x=x, h=h, proj_a=proj_a, proj_c=proj_c, acc=acc, xbc=xbc, y=y, hs=hs, o=o, rs=rs, mix=mix)


def _layer_bwd(dxn, s, p, d, tag):
    cdim, heads, r_heads, na, nc, g_off = _dims(d)
    dxn16 = dxn.astype(BF16)
    dmix = _matmul(dxn16, p["w_out"], "nt", F32, f"dmix{tag}")
    g_w_out = _matmul(s["mix"], dxn16, "tn", F32, f"dwout{tag}")
    dq, dk, dv, dg = _sba_bwd(dmix, s["o"], s["rs"], s["proj_a"], s["proj_c"], d, g_off, f"sba_b{tag}")
    dy, dz, g_ssd_norm = _gatenorm_bwd(dmix, s["y"], s["proj_a"], p["ssd_norm_w"], f"gate_b{tag}")
    dx_s, db_s, dc_s, ddt, sums = _ssd_bwd(dy, s["xbc"], s["proj_a"], s["hs"], p["bias_g"], p["alog_g"], p["dsk_g"], d, f"ssd_b{tag}")
    dxbc = jnp.concatenate([dx_s, db_s, dc_s], axis=1)
    du, g_conv = _conv_bwd(dxbc, s["acc"], s["proj_a"], p["conv_w"], d, f"conv_b{tag}")
    dproj_a = jnp.concatenate([dz, du, ddt.astype(BF16), dg], axis=1)
    dproj_c = jnp.concatenate([dq, dk, dv], axis=1)
    g_wa = _matmul(s["h"], dproj_a, "tn", F32, f"dwin_a{tag}")
    g_wc = _matmul(s["h"], dproj_c, "tn", F32, f"dwin_c{tag}")
    dh = _matmul(dproj_a, p["wa"], "nt", F32, f"dh_a{tag}")
    dh = _matmul(dproj_c, p["wc"], "nt", F32, f"dh_c{tag}", add=dh)
    dx, g_norm = _rmsnorm_bwd(dh, s["x"], p["norm_w"], dxn, f"norm_b{tag}")
    a_neg = -jnp.exp(p["alog_g"][:, 0, :r_heads].reshape(heads))
    grads = dict(
        norm_w=g_norm[0], w_in=_unpack_w_in(g_wa, g_wc, d), conv_w=g_conv[:KCONV], conv_b=g_conv[KCONV],
        dt_bias=sums[:, 0, :r_heads].reshape(heads), a_log=sums[:, 1, :r_heads].reshape(heads) * a_neg,
        d_skip=sums[:, 2, :r_heads].reshape(heads), ssd_norm_w=g_ssd_norm[0], w_out=g_w_out)
    return dx, grads


def _local_step(x, target, w_in16, w_out16, conv_w, small):
    l, d = x.shape
    depth = w_in16.shape[0]
    r_heads = _dims(d)[2]
    layers = []
    for i in range(depth):
        wa, wc = _pack_w_in(w_in16[i], d)
        layers.append(dict(
            norm_w=small["norm_w"][i][None], wa=wa, wc=wc, conv_w=conv_w[i], conv_b=small["conv_b"][i][None],
            bias_g=_group_vec(small["dt_bias"][i], r_heads), alog_g=_group_vec(small["a_log"][i], r_heads),
            dsk_g=_group_vec(small["d_skip"][i], r_heads), ssd_norm_w=small["ssd_norm_w"][i][None], w_out=w_out16[i]))
    saved = []
    hcur = x
    for i in range(depth):
        hcur, s = _layer_fwd(hcur, layers[i], d, str(i))
        saved.append(s)
    dh, g_final, loss = _final_loss(hcur, small["final_norm_w"][None], target, "final_loss")
    grads = [None] * depth
    for i in reversed(range(depth)):
        dh, grads[i] = _layer_bwd(dh, saved[i], layers[i], d, str(i))
    stacked = {k: jnp.stack([g[k] for g in grads]) for k in grads[0]}
    stacked["final_norm_w"] = g_final[0]
    return loss[0, 0], dh, stacked


HBM = pl.BlockSpec(memory_space=pl.ANY)
NCHIP = 4
NDEV = 8


def _mesh_pos():
    x, y, c = lax.axis_index("x"), lax.axis_index("y"), lax.axis_index("c")
    chips = [(1 - x, y), (x, 1 - y), (1 - x, 1 - y)]
    return x, y, c, chips


def _remote(src, dst, send_sem, recv_sem, dev):
    return pltpu.make_async_remote_copy(src_ref=src, dst_ref=dst, send_sem=send_sem, recv_sem=recv_sem,
                                        device_id=dev, device_id_type=MESH)


def _gather_weights(shards):
    n = len(shards)
    hl = shards[0].shape[0] // 2

    def body(*refs):
        ins, outs = refs[:n], refs[n:2 * n]
        send, recv = refs[2 * n:]
        x, y, c, chips = _mesh_pos()
        k = 2 * x + y
        half = pl.ds(c * hl, hl)
        other = pl.ds((1 - c) * hl, hl)
        for a in range(n):
            for j, (px, py) in enumerate(chips):
                _remote(ins[a].at[half], outs[a].at[k, half], send.at[a, j], recv.at[a, j], (px, py, c)).start()
        for a in range(n):
            for j, (px, py) in enumerate(chips):
                kj = 2 * px + py
                got = outs[a].at[kj, half]
                _remote(got, got, send.at[a, j], recv.at[a, j], (px, py, c)).wait_recv()
                _remote(got, got, send.at[a, 3 + j], recv.at[a, 3 + j], (x, y, 1 - c)).start()
        for a in range(n):
            for j, (px, py) in enumerate(chips):
                kj = 2 * px + py
                _remote(outs[a].at[kj, other], outs[a].at[kj, other], send.at[a, 3 + j], recv.at[a, 3 + j], (x, y, 1 - c)).wait_recv()
            for j, (px, py) in enumerate(chips):
                kj = 2 * px + py
                _remote(ins[a].at[half], outs[a].at[k, half], send.at[a, j], recv.at[a, j], (px, py, c)).wait_send()
                _remote(outs[a].at[kj, half], outs[a].at[kj, half], send.at[a, 3 + j], recv.at[a, 3 + j], (x, y, 1 - c)).wait_send()

    return pl.pallas_call(
        body, name="gather_weights", in_specs=[HBM] * n, out_specs=[HBM] * n,
        out_shape=[jax.ShapeDtypeStruct((NCHIP,) + s.shape, s.dtype) for s in shards],
        scratch_shapes=[pltpu.SemaphoreType.DMA((n, 6)), pltpu.SemaphoreType.DMA((n, 6))],
    )(*shards)


def _swap_halves(parts):
    n = len(parts)
    hl = parts[0].shape[1] // 2

    def body(*refs):
        ins, outs = refs[:n], refs[n:2 * n]
        send, recv = refs[2 * n:]
        x, y, c, _ = _mesh_pos()
        cps = [_remote(ins[a].at[:, pl.ds((1 - c) * hl, hl)], outs[a], send.at[a], recv.at[a], (x, y, 1 - c)) for a in range(n)]
        for cp in cps:
            cp.start()
        for cp in cps:
            cp.wait()

    return pl.pallas_call(
        body, name="grad_swap_halves", in_specs=[HBM] * n, out_specs=[HBM] * n,
        out_shape=[jax.ShapeDtypeStruct((NCHIP, hl) + p.shape[2:], p.dtype) for p in parts],
        scratch_shapes=[pltpu.SemaphoreType.DMA((n,)), pltpu.SemaphoreType.DMA((n,))],
    )(*parts)


def _exchange_shards(parts):
    n = len(parts)

    def body(*refs):
        ins, outs = refs[:n], refs[n:2 * n]
        send, recv = refs[2 * n:]
        x, y, c, chips = _mesh_pos()
        k = 2 * x + y
        work = []
        for a in range(n):
            for j, (px, py) in enumerate(chips):
                cp = _remote(ins[a].at[2 * px + py], outs[a].at[k], send.at[a, j], recv.at[a, j], (px, py, c))
                cp.start()
                work.append(cp)
        for w in work:
            w.wait()

    return pl.pallas_call(
        body, name="grad_exchange", in_specs=[HBM] * n, out_specs=[HBM] * n,
        out_shape=[jax.ShapeDtypeStruct(p.shape, p.dtype) for p in parts],
        scratch_shapes=[pltpu.SemaphoreType.DMA((n, 3)), pltpu.SemaphoreType.DMA((n, 3))],
    )(*parts)


def _swap_reduced(halves):
    n = len(halves)

    def body(*refs):
        ins, outs = refs[:n], refs[n:2 * n]
        send, recv = refs[2 * n:]
        x, y, c, _ = _mesh_pos()
        work = [_remote(ins[a], outs[a], send.at[a], recv.at[a], (x, y, 1 - c)) for a in range(n)]
        for w in work:
            w.start()
        for w in work:
            w.wait()

    return pl.pallas_call(
        body, name="grad_swap_reduced", in_specs=[HBM] * n, out_specs=[HBM] * n,
        out_shape=[jax.ShapeDtypeStruct(h.shape, h.dtype) for h in halves],
        scratch_shapes=[pltpu.SemaphoreType.DMA((n,)), pltpu.SemaphoreType.DMA((n,))],
    )(*halves)


def _all_sum_small(vec, name):
    r = vec.shape[0]

    def body(v_ref, o_ref, buf, send, recv):
        x, y, c, _ = _mesh_pos()
        me = 4 * x + 2 * y + c
        buf[me] = v_ref[...]
        cps = []
        for mask in range(1, NDEV):
            fx, fy, fc = (mask >> 2) & 1, (mask >> 1) & 1, mask & 1
            peer = (1 - x if fx else x, 1 - y if fy else y, 1 - c if fc else c)
            cp = _remote(v_ref, buf.at[me], send.at[mask - 1], recv.at[mask - 1], peer)
            cp.start()
            cps.append(cp)
        for cp in cps:
            cp.wait()
        total = buf[0]
        for dev in range(1, NDEV):
            total = total + buf[dev]
        o_ref[...] = total

    vm = pl.BlockSpec(memory_space=pltpu.VMEM)
    return pl.pallas_call(
        body, name=name, in_specs=[vm], out_specs=vm, out_shape=jax.ShapeDtypeStruct((r, LANES), F32),
        scratch_shapes=[pltpu.VMEM((NDEV, r, LANES), F32), pltpu.SemaphoreType.DMA((NDEV - 1,)), pltpu.SemaphoreType.DMA((NDEV - 1,))],
    )(vec)


def _add_pairs(a, b, out_dtype, name):
    n0, n1, r, c = a.shape
    tr = _pick(r, max(8, (1 << 19) // c // 8 * 8), 8)

    def body(a_ref, b_ref, o_ref):
        o_ref[...] = (a_ref[...] + b_ref[...]).astype(out_dtype)

    blk = pl.BlockSpec((1, 1, tr, c), lambda i, j, t: (i, j, t, 0))
    return pl.pallas_call(
        body, name=name, grid=(n0, n1, r // tr), in_specs=[blk, blk], out_specs=blk,
        out_shape=jax.ShapeDtypeStruct(a.shape, out_dtype), compiler_params=_params("parallel", "parallel", "parallel"),
    )(a, b)


def _sum_chips(p, name):
    _, hl, r, c = p.shape
    tr = _pick(r, max(16, (1 << 19) // c // 16 * 16), 16)

    def body(p_ref, o_ref):
        total = p_ref[0].astype(F32)
        for j in range(1, NCHIP):
            total = total + p_ref[j].astype(F32)
        o_ref[...] = total

    return pl.pallas_call(
        body, name=name, grid=(hl, r // tr),
        in_specs=[pl.BlockSpec((NCHIP, 1, tr, c), lambda i, t: (0, i, t, 0))],
        out_specs=pl.BlockSpec((1, tr, c), lambda i, t: (i, t, 0)),
        out_shape=jax.ShapeDtypeStruct((hl, r, c), F32), compiler_params=_params("parallel", "parallel"),
    )(p)


def _reduce_scatter(parts):
    c = lax.axis_index("c")
    k = 2 * lax.axis_index("x") + lax.axis_index("y")
    hl = parts[0].shape[1] // 2
    theirs = _swap_halves(parts)
    mine = [lax.dynamic_slice_in_dim(p, c * hl, hl, axis=1) for p in parts]
    chip_sum = [_add_pairs(m, t, BF16, f"grad_pair_sum{i}") for i, (m, t) in enumerate(zip(mine, theirs))]
    gathered = _exchange_shards(chip_sum)
    gathered = [lax.dynamic_update_slice_in_dim(g, lax.dynamic_slice_in_dim(s, k, 1, axis=0), k, axis=0)
                for g, s in zip(gathered, chip_sum)]
    halves = [_sum_chips(g, f"grad_chip_sum{i}") for i, g in enumerate(gathered)]
    sibling = _swap_reduced(halves)
    south = c == 0
    return [jnp.concatenate([jnp.where(south, h, s), jnp.where(south, s, h)], axis=0) for h, s in zip(halves, sibling)]


SMALL = ("norm_w", "conv_w", "conv_b", "dt_bias", "a_log", "d_skip", "ssd_norm_w", "final_norm_w")


def _pack(arrays):
    flat = jnp.concatenate([a.reshape(-1).astype(F32) for a in arrays])
    rows = -(-flat.shape[0] // (8 * LANES)) * 8
    return jnp.pad(flat, (0, rows * LANES - flat.shape[0])).reshape(rows, LANES)


def _unpack(vec, shapes):
    flat = vec.reshape(-1)
    out, pos = [], 0
    for s in shapes:
        n = math.prod(s)
        out.append(flat[pos:pos + n].reshape(s))
        pos += n
    return out


def kernel(x, norm_w, w_in, conv_w, conv_b, dt_bias, a_log, d_skip, ssd_norm_w, w_out, final_norm_w, loss_target, m_norm_w, m_w_in, m_conv_w, m_conv_b, m_dt_bias, m_a_log, m_d_skip, m_ssd_norm_w, m_w_out, m_final_norm_w, v_norm_w, v_w_in, v_conv_w, v_conv_b, v_dt_bias, v_a_log, v_d_skip, v_ssd_norm_w, v_w_out, v_final_norm_w):
    depth, d, ics = w_in.shape
    cs = conv_w.shape[2]
    xi, yi, ci = lax.axis_index("x"), lax.axis_index("y"), lax.axis_index("c")
    k = 2 * xi + yi

    placed = lax.dynamic_update_slice(jnp.zeros((depth, KCONV, NCHIP, cs), F32), conv_w[:, :, None, :], (0, 0, k, 0))
    placed = jnp.where(ci == 0, placed, 0.0)
    conv_full = _unpack(_all_sum_small(_pack([placed]), "gather_conv_w"), [(depth, KCONV, NCHIP * cs)])[0]

    w_in16, w_out16 = w_in.astype(BF16), w_out.astype(BF16)
    wi_all, wo_all = _gather_weights([w_in16, w_out16])
    wi_all = lax.dynamic_update_slice_in_dim(wi_all, w_in16[None], k, axis=0)
    wo_all = lax.dynamic_update_slice_in_dim(wo_all, w_out16[None], k, axis=0)
    w_in_full = wi_all.transpose(1, 2, 0, 3).reshape(depth, d, NCHIP * ics)
    w_out_full = wo_all.transpose(1, 0, 2, 3).reshape(depth, 2 * d, d)

    small = dict(norm_w=norm_w, conv_b=conv_b, dt_bias=dt_bias, a_log=a_log, d_skip=d_skip, ssd_norm_w=ssd_norm_w, final_norm_w=final_norm_w)
    loss_local, gx, g = _local_step(x[0], loss_target[0], w_in_full, w_out_full, conv_full, small)

    g_in = g["w_in"].reshape(depth, d, NCHIP, ics).transpose(2, 0, 1, 3)
    g_out = g["w_out"].reshape(depth, NCHIP, 2 * d // NCHIP, d).transpose(1, 0, 2, 3)
    grad_w_in, grad_w_out = _reduce_scatter([g_in, g_out])

    names = list(SMALL)
    total = _all_sum_small(_pack([g[n] for n in names] + [loss_local]), "sum_small_grads")
    parts = _unpack(total, [g[n].shape for n in names] + [()])
    grads = dict(zip(names, parts[:-1]))
    loss = parts[-1]
    grads["conv_w"] = lax.dynamic_index_in_dim(grads["conv_w"].reshape(depth, KCONV, NCHIP, cs), k, axis=2, keepdims=False)
    grads["w_in"], grads["w_out"] = grad_w_in, grad_w_out

    w = dict(norm_w=norm_w, w_in=w_in, conv_w=conv_w, conv_b=conv_b, dt_bias=dt_bias, a_log=a_log, d_skip=d_skip,
             ssd_norm_w=ssd_norm_w, w_out=w_out, final_norm_w=final_norm_w)
    m = dict(norm_w=m_norm_w, w_in=m_w_in, conv_w=m_conv_w, conv_b=m_conv_b, dt_bias=m_dt_bias, a_log=m_a_log, d_skip=m_d_skip,
             ssd_norm_w=m_ssd_norm_w, w_out=m_w_out, final_norm_w=m_final_norm_w)
    v = dict(norm_w=v_norm_w, w_in=v_w_in, conv_w=v_conv_w, conv_b=v_conv_b, dt_bias=v_dt_bias, a_log=v_a_log, d_skip=v_d_skip,
             ssd_norm_w=v_ssd_norm_w, w_out=v_w_out, final_norm_w=v_final_norm_w)
    delta, new_m, new_v = {}, {}, {}
    for n in ("w_in", "w_out"):
        delta[n], new_m[n], new_v[n] = _adamw(w[n], grads[n], m[n], v[n], f"adamw_{n}")
    shapes = [w[n].shape for n in names]
    packed = [_pack([t[n] for n in names])[None] for t in (w, grads, m, v)]
    for res, out in zip(_adamw(*packed, "adamw_small"), (delta, new_m, new_v)):
        out.update(zip(names, _unpack(res[0], shapes)))

    order = ("norm_w", "w_in", "conv_w", "conv_b", "dt_bias", "a_log", "d_skip", "ssd_norm_w", "w_out", "final_norm_w")
    return (loss, gx[None], *[grads[n] for n in order], *[delta[n] for n in order], *[new_m[n] for n in order], *[new_v[n] for n in order])
```

```python
import functools
import math

import jax
import jax.numpy as jnp
from jax import lax
from jax.experimental import pallas as pl
from jax.experimental.pallas import tpu as pltpu

F32, BF16 = jnp.float32, jnp.bfloat16
EPS = 1e-6
CHUNK = 64
NGROUPS = 4
NSTATE = 128
KCONV = 4
SSD_HD = 64
SSD_HD_SHIFT = SSD_HD.bit_length() - 1
SBA_HD = 128
LANES = 128
VMEM_LIMIT = 56 * 1024 * 1024

ADAM_LR, ADAM_B1, ADAM_B2, ADAM_EPS, ADAM_WD, ADAM_STEP = 0.001, 0.9, 0.999, 1e-08, 0.01, 10

NN = ((1,), (0,))
NT = ((1,), (1,))
TN = ((0,), (0,))
MESH = pl.DeviceIdType.MESH


def _dot(a, b, dims):
    return lax.dot_general(a, b, (dims, ((), ())), preferred_element_type=F32)


def _params(*sem):
    return pltpu.CompilerParams(dimension_semantics=sem, vmem_limit_bytes=VMEM_LIMIT)


def _pick(n, target, mult):
    best = None
    for d in range(mult, min(n, target) + 1, mult):
        if n % d == 0:
            best = d
    return n if best is None else best


def _split3(x):
    x1 = x.astype(BF16)
    r1 = x - x1.astype(F32)
    x2 = r1.astype(BF16)
    x3 = (r1 - x2.astype(F32)).astype(BF16)
    return x1, x2, x3


def _split2(x):
    x1 = x.astype(BF16)
    return x1, (x - x1.astype(F32)).astype(BF16)


def _sigmoid(x):
    return 1.0 / (1.0 + jnp.exp(-x))


def _softplus(x):
    return jnp.maximum(x, 0.0) + jnp.log1p(jnp.exp(-jnp.abs(x)))


def _matmul(a, b, mode, out_dtype, name, add=None, tm=1024, tn=1024, tk=2048):
    if mode == "nn":
        (m, k), n = a.shape, b.shape[1]
    elif mode == "nt":
        (m, k), n = a.shape, b.shape[0]
    else:
        (k, m), n = a.shape, b.shape[1]
    tm, tn, tk = _pick(m, tm, LANES), _pick(n, tn, LANES), _pick(k, tk, LANES)
    nk = k // tk
    dims = {"nn": NN, "nt": NT, "tn": TN}[mode]
    a_spec = pl.BlockSpec((tk, tm), lambda i, j, kk: (kk, i)) if mode == "tn" else pl.BlockSpec((tm, tk), lambda i, j, kk: (i, kk))
    b_spec = pl.BlockSpec((tn, tk), lambda i, j, kk: (j, kk)) if mode == "nt" else pl.BlockSpec((tk, tn), lambda i, j, kk: (kk, j))
    o_spec = pl.BlockSpec((tm, tn), lambda i, j, kk: (i, j))
    has_add = add is not None

    def body(*refs):
        a_ref, b_ref = refs[0], refs[1]
        add_ref = refs[2] if has_add else None
        o_ref, acc_ref = refs[-2], refs[-1]
        kk = pl.program_id(2)
        part = _dot(a_ref[...], b_ref[...], dims)

        def finish(total):
            if has_add:
                total = total + add_ref[...].astype(F32)
            o_ref[...] = total.astype(out_dtype)

        if nk == 1:
            finish(part)
        else:
            @pl.when(kk == 0)
            def _():
                acc_ref[...] = part

            @pl.when(jnp.logical_and(kk > 0, kk < nk - 1))
            def _():
                acc_ref[...] += part

            @pl.when(kk == nk - 1)
            def _():
                finish(acc_ref[...] + part)

    in_specs = [a_spec, b_spec] + ([o_spec] if has_add else [])
    args = (a, b) + ((add,) if has_add else ())
    return pl.pallas_call(
        body, name=name, grid=(m // tm, n // tn, nk), in_specs=in_specs, out_specs=o_spec,
        out_shape=jax.ShapeDtypeStruct((m, n), out_dtype),
        scratch_shapes=[pltpu.VMEM((tm, tn), F32)],
        compiler_params=_params("parallel", "parallel", "arbitrary"),
    )(*args)


def _rmsnorm_fwd(x, w, name):
    l, d = x.shape
    tr = _pick(l, 512, 8)

    def body(x_ref, w_ref, h_ref):
        xv = x_ref[...]
        r = lax.rsqrt(jnp.mean(xv * xv, axis=-1, keepdims=True) + EPS)
        h_ref[...] = (xv * r * w_ref[...]).astype(BF16)

    return pl.pallas_call(
        body, name=name, grid=(l // tr,),
        in_specs=[pl.BlockSpec((tr, d), lambda i: (i, 0)), pl.BlockSpec((1, d), lambda i: (0, 0))],
        out_specs=pl.BlockSpec((tr, d), lambda i: (i, 0)),
        out_shape=jax.ShapeDtypeStruct((l, d), BF16), compiler_params=_params("parallel"),
    )(x, w)


def _rmsnorm_bwd(dh, x, w, dres, name):
    l, d = x.shape
    tr = _pick(l, 256, 8)

    def body(dh_ref, x_ref, w_ref, dres_ref, dx_ref, dw_ref):
        xv = x_ref[...]
        r = lax.rsqrt(jnp.mean(xv * xv, axis=-1, keepdims=True) + EPS)
        xh = xv * r
        dhv = dh_ref[...]
        dxh = dhv * w_ref[...]
        dx_ref[...] = dres_ref[...] + r * (dxh - xh * jnp.mean(dxh * xh, axis=-1, keepdims=True))
        part = jnp.sum(dhv * xh, axis=0, keepdims=True)

        @pl.when(pl.program_id(0) == 0)
        def _():
            dw_ref[...] = part

        @pl.when(pl.program_id(0) > 0)
        def _():
            dw_ref[...] += part

    row = pl.BlockSpec((tr, d), lambda i: (i, 0))
    vec = pl.BlockSpec((1, d), lambda i: (0, 0))
    return pl.pallas_call(
        body, name=name, grid=(l // tr,), in_specs=[row, row, vec, row], out_specs=[row, vec],
        out_shape=[jax.ShapeDtypeStruct((l, d), F32), jax.ShapeDtypeStruct((1, d), F32)],
        compiler_params=_params("arbitrary"),
    )(dh, x, w, dres)


def _final_loss(h, w, target, name):
    l, d = h.shape
    tr = _pick(l, 256, 8)

    def body(h_ref, w_ref, t_ref, dh_ref, dw_ref, loss_ref):
        xv = h_ref[...]
        r = lax.rsqrt(jnp.mean(xv * xv, axis=-1, keepdims=True) + EPS)
        xh = xv * r
        err = xh * w_ref[...] - t_ref[...]
        dy = err * (1.0 / d)
        dxh = dy * w_ref[...]
        dh_ref[...] = r * (dxh - xh * jnp.mean(dxh * xh, axis=-1, keepdims=True))
        part = jnp.sum(dy * xh, axis=0, keepdims=True)
        lpart = jnp.zeros((8, LANES), F32) + 0.5 * jnp.sum(jnp.mean(err * err, axis=-1, keepdims=True))

        @pl.when(pl.program_id(0) == 0)
        def _():
            dw_ref[...] = part
            loss_ref[...] = lpart

        @pl.when(pl.program_id(0) > 0)
        def _():
            dw_ref[...] += part
            loss_ref[...] += lpart

    row = pl.BlockSpec((tr, d), lambda i: (i, 0))
    vec = pl.BlockSpec((1, d), lambda i: (0, 0))
    return pl.pallas_call(
        body, name=name, grid=(l // tr,), in_specs=[row, vec, row],
        out_specs=[row, vec, pl.BlockSpec((8, LANES), lambda i: (0, 0))],
        out_shape=[jax.ShapeDtypeStruct((l, d), F32), jax.ShapeDtypeStruct((1, d), F32), jax.ShapeDtypeStruct((8, LANES), F32)],
        compiler_params=_params("arbitrary"),
    )(h, w, target)


def _gatenorm_fwd(y, proj_a, w, name):
    l, d = y.shape
    dg = d // NGROUPS
    tr = _pick(l, 256, 8)

    def body(y_ref, z_ref, w_ref, o_ref):
        for g in range(NGROUPS):
            sl = slice(g * dg, (g + 1) * dg)
            zv = z_ref[:, sl]
            u = y_ref[:, sl] * (zv * _sigmoid(zv))
            r = lax.rsqrt(jnp.mean(u * u, axis=-1, keepdims=True) + EPS)
            o_ref[:, sl] = (u * r * w_ref[:, sl]).astype(BF16)

    row = pl.BlockSpec((tr, d), lambda i: (i, 0))
    return pl.pallas_call(
        body, name=name, grid=(l // tr,), in_specs=[row, row, pl.BlockSpec((1, d), lambda i: (0, 0))],
        out_specs=row, out_shape=jax.ShapeDtypeStruct((l, d), BF16), compiler_params=_params("parallel"),
    )(y, proj_a, w)


def _gatenorm_bwd(dout, y, proj_a, w, name):
    l, d = y.shape
    dg = d // NGROUPS
    tr = _pick(l, 256, 8)

    def body(do_ref, y_ref, z_ref, w_ref, dy_ref, dz_ref, dw_ref):
        first = pl.program_id(0) == 0
        for g in range(NGROUPS):
            sl = slice(g * dg, (g + 1) * dg)
            zv = z_ref[:, sl]
            sg = _sigmoid(zv)
            sz = zv * sg
            yv = y_ref[:, sl]
            u = yv * sz
            r = lax.rsqrt(jnp.mean(u * u, axis=-1, keepdims=True) + EPS)
            nh = u * r
            dov = do_ref[:, sl]
            dn = dov * w_ref[:, sl]
            du = r * (dn - nh * jnp.mean(dn * nh, axis=-1, keepdims=True))
            dy_ref[:, sl] = du * sz
            dz_ref[:, sl] = (du * yv * (sg * (1.0 + zv * (1.0 - sg)))).astype(BF16)
            part = jnp.sum(dov * nh, axis=0, keepdims=True)

            @pl.when(first)
            def _():
                dw_ref[:, sl] = part

            @pl.when(jnp.logical_not(first))
            def _():
                dw_ref[:, sl] += part

    row = pl.BlockSpec((tr, d), lambda i: (i, 0))
    vec = pl.BlockSpec((1, d), lambda i: (0, 0))
    return pl.pallas_call(
        body, name=name, grid=(l // tr,), in_specs=[row, row, row, vec], out_specs=[row, row, vec],
        out_shape=[jax.ShapeDtypeStruct((l, d), F32), jax.ShapeDtypeStruct((l, d), BF16), jax.ShapeDtypeStruct((1, d), F32)],
        compiler_params=_params("arbitrary"),
    )(dout, y, proj_a, w)


def _conv_cols(d):
    return _pick(math.gcd(d, d + 2 * NGROUPS * NSTATE), 512, LANES)


def _conv_fwd(proj_a, conv_w, conv_b, d, name):
    l = proj_a.shape[0]
    cdim = d + 2 * NGROUPS * NSTATE
    cw = _conv_cols(d)
    off = d // cw
    tl = _pick(l, 512, 8)
    hb = tl // 8

    def body(u_ref, up_ref, w_ref, b_ref, acc_ref, xbc_ref, ext):
        i = pl.program_id(1)
        ext[0:8, :] = jnp.where(i > 0, up_ref[...], 0.0)
        ext[8:8 + tl, :] = u_ref[...]
        acc = jnp.zeros((tl, cw), F32) + b_ref[...]
        for j in range(KCONV):
            acc = acc + w_ref[j:j + 1, :] * ext[pl.ds(8 - (KCONV - 1) + j, tl), :]
        acc_ref[...] = acc
        xbc_ref[...] = acc * _sigmoid(acc)

    blk = pl.BlockSpec((tl, cw), lambda c, i: (i, c))
    return pl.pallas_call(
        body, name=name, grid=(cdim // cw, l // tl),
        in_specs=[pl.BlockSpec((tl, cw), lambda c, i: (i, c + off)),
                  pl.BlockSpec((8, cw), lambda c, i: (jnp.maximum(i * hb - 1, 0), c + off)),
                  pl.BlockSpec((KCONV, cw), lambda c, i: (0, c)), pl.BlockSpec((1, cw), lambda c, i: (0, c))],
        out_specs=[blk, blk],
        out_shape=[jax.ShapeDtypeStruct((l, cdim), F32), jax.ShapeDtypeStruct((l, cdim), F32)],
        scratch_shapes=[pltpu.VMEM((tl + 8, cw), F32)], compiler_params=_params("parallel", "parallel"),
    )(proj_a, proj_a, conv_w, conv_b)


def _conv_bwd(dxbc, acc, proj_a, conv_w, d, name):
    l, cdim = acc.shape
    cw = _conv_cols(d)
    off = d // cw
    tl = _pick(l, 512, 8)
    hb = tl // 8
    nb = l // tl

    def dsilu(g, a):
        s = _sigmoid(a)
        return g * (s * (1.0 + a * (1.0 - s)))

    def body(g_ref, gn_ref, a_ref, an_ref, u_ref, up_ref, w_ref, du_ref, dw_ref, ext, dext):
        i = pl.program_id(1)
        da = dsilu(g_ref[...], a_ref[...])
        dext[0:tl, :] = da
        dext[tl:tl + 8, :] = jnp.where(i < nb - 1, dsilu(gn_ref[...], an_ref[...]), 0.0)
        ext[0:8, :] = jnp.where(i > 0, up_ref[...], 0.0)
        ext[8:8 + tl, :] = u_ref[...]
        du = jnp.zeros((tl, cw), F32)
        rows = []
        for j in range(KCONV):
            du = du + w_ref[j:j + 1, :] * dext[pl.ds(KCONV - 1 - j, tl), :]
            rows.append(jnp.sum(da * ext[pl.ds(8 - (KCONV - 1) + j, tl), :], axis=0, keepdims=True))
        rows.append(jnp.sum(da, axis=0, keepdims=True))
        du_ref[...] = du.astype(BF16)

        @pl.when(i == 0)
        def _():
            dw_ref[...] = jnp.zeros_like(dw_ref)

        for j, rv in enumerate(rows):
            dw_ref[j:j + 1, :] += rv

    blk = pl.BlockSpec((tl, cw), lambda c, i: (i, c))
    nxt = pl.BlockSpec((8, cw), lambda c, i: (jnp.minimum((i + 1) * hb, l // 8 - 1), c))
    return pl.pallas_call(
        body, name=name, grid=(cdim // cw, nb),
        in_specs=[blk, nxt, blk, nxt,
                  pl.BlockSpec((tl, cw), lambda c, i: (i, c + off)),
                  pl.BlockSpec((8, cw), lambda c, i: (jnp.maximum(i * hb - 1, 0), c + off)),
                  pl.BlockSpec((KCONV, cw), lambda c, i: (0, c))],
        out_specs=[blk, pl.BlockSpec((8, cw), lambda c, i: (0, c))],
        out_shape=[jax.ShapeDtypeStruct((l, cdim), BF16), jax.ShapeDtypeStruct((8, cdim), F32)],
        scratch_shapes=[pltpu.VMEM((tl + 8, cw), F32), pltpu.VMEM((tl + 8, cw), F32)],
        compiler_params=_params("parallel", "arbitrary"),
    )(dxbc, dxbc, acc, acc, proj_a, proj_a, conv_w)


def _ssd_common(dt_ref, bias_ref, alog_ref):
    li = lax.broadcasted_iota(jnp.int32, (CHUNK, CHUNK), 0)
    si = lax.broadcasted_iota(jnp.int32, (CHUNK, CHUNK), 1)
    tri = si <= li
    dtv = _softplus(dt_ref[...] + bias_ref[0])
    a_neg = -jnp.exp(alog_ref[0])
    da = dtv * a_neg
    cs_col = sum(_dot(tri.astype(BF16), p, NN) for p in _split3(da))
    cs_row = sum(_dot(p, (li <= si).astype(BF16), TN) for p in _split3(da))
    return tri, dtv, a_neg, cs_col, cs_row


def _dot2(x, t16, dims):
    hi, lo = _split2(x)
    if dims == NN:
        return _dot(jnp.concatenate([hi, lo], axis=1), jnp.concatenate([t16, t16], axis=0), NN)
    return _dot(hi, t16, dims) + _dot(lo, t16, dims)


def _ssd_expand(dtv, cs_col, dsk, pg):
    expm = (lax.shift_right_logical(lax.broadcasted_iota(jnp.int32, (LANES, pg), 1), SSD_HD_SHIFT)
            == lax.broadcasted_iota(jnp.int32, (LANES, pg), 0)).astype(BF16)
    tot = cs_col[CHUNK - 1:CHUNK, :]
    stack = jnp.concatenate([dtv, jnp.exp(cs_col), jnp.exp(tot - cs_col), jnp.broadcast_to(dsk, (CHUNK, LANES)),
                             jnp.broadcast_to(jnp.exp(tot), (CHUNK, LANES))], axis=0)
    ex = _dot2(stack, expm, NN)
    return tuple(ex[CHUNK * a:CHUNK * (a + 1)] for a in range(5))


SSD_GPS = 2


def _ssd_specs(d, r_heads):
    pg = r_heads * SSD_HD
    gw = SSD_GPS * LANES
    nb = d // gw
    dt_blk = (2 * d + 2 * NGROUPS * NSTATE) // gw
    x_spec = lambda cmap: pl.BlockSpec((CHUNK, SSD_GPS * pg), lambda g, c: (cmap(c), g))
    b_spec = lambda cmap: pl.BlockSpec((CHUNK, gw), lambda g, c: (cmap(c), nb + g))
    c_spec = lambda cmap: pl.BlockSpec((CHUNK, gw), lambda g, c: (cmap(c), nb + NGROUPS // SSD_GPS + g))
    dt_spec = lambda cmap: pl.BlockSpec((CHUNK, gw), lambda g, c: (cmap(c), dt_blk + g))
    const = pl.BlockSpec((SSD_GPS, 1, LANES), lambda g, c: (g, 0, 0))
    return pg, x_spec, b_spec, c_spec, dt_spec, const


def _ssd_group_views(refs, kinds, gi, pg):
    out = []
    for r, kind in zip(refs, kinds):
        if kind == "x":
            out.append(r.at[:, pl.ds(gi * pg, pg)])
        elif kind == "g":
            out.append(r.at[:, pl.ds(gi * LANES, LANES)])
        elif kind == "c":
            out.append(r.at[pl.ds(gi, 1)])
        elif kind == "h":
            out.append(r.at[:, pl.ds(gi, 1)])
        else:
            out.append(r.at[gi])
    return out


def _ssd_fwd(xbc, proj_a, bias_g, alog_g, dsk_g, d, name):
    l = xbc.shape[0]
    nc = l // CHUNK
    r_heads = d // SSD_HD // NGROUPS
    pg, x_spec, b_spec, c_spec, dt_spec, const = _ssd_specs(d, r_heads)
    ident = lambda c: c

    def body(*refs):
        for gi in range(SSD_GPS):
            group(*_ssd_group_views(refs, "xgggcccxhs", gi, pg))

    def group(x_ref, b_ref, c_ref, dt_ref, bias_ref, alog_ref, dsk_ref, y_ref, hs_ref, h_scr):
        @pl.when(pl.program_id(1) == 0)
        def _():
            h_scr[...] = jnp.zeros_like(h_scr)

        tri, dtv, _, cs_col, cs_row = _ssd_common(dt_ref, bias_ref, alog_ref)
        hin = h_scr[...]
        hs_ref[0, 0] = hin
        bm = b_ref[...].astype(BF16)
        cm = c_ref[...].astype(BF16)
        xv = x_ref[...]
        e_dt, e_ecs, e_decs, e_dsk, e_etot = _ssd_expand(dtv, cs_col, dsk_ref[0], pg)
        xd = xv * e_dt
        xd16 = xd.astype(BF16)
        gmat = _dot(cm, bm, NT)
        for r in range(r_heads):
            sl = slice(SSD_HD * r, SSD_HD * (r + 1))
            lm = jnp.exp(jnp.where(tri, cs_col[:, r:r + 1] - cs_row[r:r + 1, :], -jnp.inf))
            y_ref[:, sl] = _dot((gmat * lm).astype(BF16), xd16[:, sl], NN)
        y_ref[...] += e_ecs * _dot(cm, hin.astype(BF16), NN) + e_dsk * xv
        h_scr[...] = hin * e_etot[0:1] + _dot(bm, (xd * e_decs).astype(BF16), TN)

    return pl.pallas_call(
        body, name=name, grid=(NGROUPS // SSD_GPS, nc),
        in_specs=[x_spec(ident), b_spec(ident), c_spec(ident), dt_spec(ident), const, const, const],
        out_specs=[x_spec(ident), pl.BlockSpec((1, SSD_GPS, NSTATE, pg), lambda g, c: (c, g, 0, 0))],
        out_shape=[jax.ShapeDtypeStruct((l, d), F32), jax.ShapeDtypeStruct((nc, NGROUPS, NSTATE, pg), F32)],
        scratch_shapes=[pltpu.VMEM((SSD_GPS, NSTATE, pg), F32)], compiler_params=_params("parallel", "arbitrary"),
    )(xbc, xbc, xbc, proj_a, bias_g, alog_g, dsk_g)


def _ssd_bwd(dy, xbc, proj_a, hs, bias_g, alog_g, dsk_g, d, name):
    l = xbc.shape[0]
    nc = l // CHUNK
    r_heads = d // SSD_HD // NGROUPS
    pg, x_spec, b_spec, c_spec, dt_spec, const = _ssd_specs(d, r_heads)
    rev = lambda c: nc - 1 - c
    cdim = d + 2 * NGROUPS * NSTATE

    def body(*refs):
        for gi in range(SSD_GPS):
            group(*_ssd_group_views(refs, "xxggghcccxgggcssss", gi, pg))

    def group(dy_ref, x_ref, b_ref, c_ref, dt_ref, hs_ref, bias_ref, alog_ref, dsk_ref,
              dx_ref, db_ref, dc_ref, ddt_ref, sums_ref, dh_scr, p_scr, pt_scr, dxd_scr):
        first = pl.program_id(1) == 0

        @pl.when(first)
        def _():
            dh_scr[...] = jnp.zeros_like(dh_scr)
            sums_ref[...] = jnp.zeros_like(sums_ref)

        tri, dtv, a_neg, cs_col, cs_row = _ssd_common(dt_ref, bias_ref, alog_ref)
        li = lax.broadcasted_iota(jnp.int32, (CHUNK, CHUNK), 0)
        si = lax.broadcasted_iota(jnp.int32, (CHUNK, CHUNK), 1)
        tri_t = li <= si
        lastrow = lax.broadcasted_iota(jnp.int32, (CHUNK, 1), 0) == CHUNK - 1
        indm = (lax.shift_right_logical(lax.broadcasted_iota(jnp.int32, (pg, LANES), 0), SSD_HD_SHIFT)
                == lax.broadcasted_iota(jnp.int32, (pg, LANES), 1)).astype(BF16)
        hin = hs_ref[0, 0]
        dhout = dh_scr[...]
        hin16 = hin.astype(BF16)
        dhout16 = dhout.astype(BF16)
        bm = b_ref[...].astype(BF16)
        cm = c_ref[...].astype(BF16)
        xv = x_ref[...]
        dyv = dy_ref[...]
        e_dt, e_ecs, e_decs, e_dsk, e_etot = _ssd_expand(dtv, cs_col, dsk_ref[0], pg)
        xd = xv * e_dt
        xd16 = xd.astype(BF16)
        dy16 = dyv.astype(BF16)
        dye = dyv * e_ecs
        dye16 = dye.astype(BF16)
        gmat = _dot(cm, bm, NT)
        gmat_t = _dot(bm, cm, NT)
        dg = jnp.zeros((CHUNK, CHUNK), F32)
        dg_t = jnp.zeros((CHUNK, CHUNK), F32)
        for r in range(r_heads):
            sl = slice(SSD_HD * r, SSD_HD * (r + 1))
            col = cs_col[:, r:r + 1]
            row = cs_row[r:r + 1, :]
            lm = jnp.exp(jnp.where(tri, col - row, -jnp.inf))
            lm_t = jnp.exp(jnp.where(tri_t, row - col, -jnp.inf))
            dm = _dot(dy16[:, sl], xd16[:, sl], NT)
            dm_t = _dot(xd16[:, sl], dy16[:, sl], NT)
            dg = dg + dm * lm
            dg_t = dg_t + dm_t * lm_t
            m_t = gmat_t * lm_t
            p_scr[:, sl] = dm * (gmat * lm)
            pt_scr[:, sl] = dm_t * m_t
            dxd_scr[:, sl] = _dot(m_t.astype(BF16), dy16[:, sl], NN)
        yoff = _dot(cm, hin16, NN)
        qall = _dot(bm, dhout16, NN)
        dxd = dxd_scr[...] + qall * e_decs
        hh = jnp.broadcast_to(jnp.sum(dhout * hin, axis=0, keepdims=True), (8, pg))
        red = _dot2(jnp.concatenate([dye * yoff, qall * xd, dxd * xv, dyv * xv, p_scr[...], pt_scr[...], hh], axis=0), indm, NN)
        r_yoff, r_q, r_dt, r_dsk, r_p, r_pt = (red[CHUNK * a:CHUNK * (a + 1)] for a in range(6))
        tot = cs_col[CHUNK - 1:CHUNK, :]
        ddec = r_q * jnp.exp(tot - cs_col)
        dtot = jnp.sum(ddec, axis=0, keepdims=True) + jnp.exp(tot) * red[6 * CHUNK:6 * CHUNK + 1]
        dcs = r_p - r_pt + r_yoff - ddec + jnp.where(lastrow, dtot, 0.0)
        dda = sum(_dot(tri_t.astype(BF16), p, NN) for p in _split3(dcs))
        draw = (r_dt + dda * a_neg) * _sigmoid(dt_ref[...] + bias_ref[0])
        ddt_ref[...] = draw
        sums_ref[0, 0:1, :] += jnp.sum(draw, axis=0, keepdims=True)
        sums_ref[0, 1:2, :] += jnp.sum(dda * dtv, axis=0, keepdims=True)
        sums_ref[0, 2:3, :] += jnp.sum(r_dsk, axis=0, keepdims=True)
        dx_ref[...] = dxd * e_dt + e_dsk * dyv
        dc_ref[...] = _dot(dg.astype(BF16), bm, NN) + _dot(dye16, hin16, NT)
        db_ref[...] = _dot(dg_t.astype(BF16), cm, NN) + _dot((xd * e_decs).astype(BF16), dhout16, NT)
        dh_scr[...] = dhout * e_etot[0:1] + _dot(cm, dye16, TN)

    grp = pl.BlockSpec((CHUNK, SSD_GPS * LANES), lambda g, c: (rev(c), g))
    return pl.pallas_call(
        body, name=name, grid=(NGROUPS // SSD_GPS, nc),
        in_specs=[x_spec(rev), x_spec(rev), b_spec(rev), c_spec(rev), dt_spec(rev),
                  pl.BlockSpec((1, SSD_GPS, NSTATE, pg), lambda g, c: (rev(c), g, 0, 0)), const, const, const],
        out_specs=[x_spec(rev), grp, grp, grp, pl.BlockSpec((SSD_GPS, 8, LANES), lambda g, c: (g, 0, 0))],
        out_shape=[jax.ShapeDtypeStruct((l, d), F32), jax.ShapeDtypeStruct((l, NGROUPS * NSTATE), F32),
                   jax.ShapeDtypeStruct((l, NGROUPS * NSTATE), F32),
                   jax.ShapeDtypeStruct((l, NGROUPS * LANES), F32), jax.ShapeDtypeStruct((NGROUPS, 8, LANES), F32)],
        scratch_shapes=[pltpu.VMEM((SSD_GPS, NSTATE, pg), F32)] + [pltpu.VMEM((SSD_GPS, CHUNK, pg), F32)] * 3,
        compiler_params=_params("parallel", "arbitrary"),
    )(dy, xbc, xbc, xbc, proj_a, hs, bias_g, alog_g, dsk_g)


SBA_TK = 256
SBA_TQ_FWD = 1024
SBA_TQ_BWD = 1024
SBA_ROW_PARTS = 8


def _sba_tiles(l, tq_target):
    tk = _pick(l, SBA_TK, LANES)
    tq = _pick(l, tq_target, tk)
    assert l // tk <= LANES
    return tq, tk


def _sba_scores(qk, valid, scale):
    z = qk * scale
    nz = -z
    lk = jnp.minimum(nz, 0.0) - jnp.log(1.0 + jnp.exp(jnp.minimum(z, nz)))
    if valid is not None:
        lk = jnp.where(valid, lk, 0.0)
    return z, lk


def _sba_fwd(proj_a, proj_c, d, g_off, name):
    l = proj_c.shape[0]
    nh = d // SBA_HD
    tq, tk = _sba_tiles(l, SBA_TQ_FWD)
    band = tq // tk
    scale = 1.0 / math.sqrt(SBA_HD)
    rq = tq // SBA_ROW_PARTS if tq % (SBA_ROW_PARTS * 16) == 0 else tq
    parts = [pl.ds(p * rq, rq) for p in range(tq // rq)]

    def body(q_ref, k_ref, v_ref, g_ref, o_ref, y_ref, rs_ref, rs_scr, qk_a, qk_b):
        i = pl.program_id(1)
        ki = lax.broadcasted_iota(jnp.int32, (tk, tk), 0)
        kj = lax.broadcasted_iota(jnp.int32, (tk, tk), 1)
        uinc = (ki >= kj).astype(BF16)
        lane = lax.broadcasted_iota(jnp.int32, (rq, LANES), 1)
        rs_scr[...] = jnp.zeros_like(rs_scr)
        qs = [q_ref[ps, :] for ps in parts]

        def qk_into(j, qk_scr):
            kb = k_ref[pl.ds(pl.multiple_of(j * tk, tk), tk), :]
            for p, ps in enumerate(parts):
                qk_scr[ps, :] = _dot(qs[p], kb, NT)

        def tile(j, carry, band_pos=None, qk_scr=None):
            start = pl.multiple_of(j * tk, tk)
            vb = v_ref[pl.ds(start, tk), :]
            live, valid = list(range(len(parts))), {}
            if band_pos is not None:
                k0, k1 = band_pos * tk, (band_pos + 1) * tk - 1
                live = [p for p in live if (p + 1) * rq - 1 > k0]
                rows = lax.broadcasted_iota(jnp.int32, (rq, tk), 0)
                cols = lax.broadcasted_iota(jnp.int32, (rq, tk), 1)
                valid = {p: cols - rows < p * rq - k0 for p in live if p * rq <= k1}
            if qk_scr is None:
                kb = k_ref[pl.ds(start, tk), :]
                qk = {p: _dot(qs[p], kb, NT) for p in live}
            else:
                qk = {p: qk_scr[parts[p], :] for p in live}
            sc = {p: _sba_scores(qk[p], valid.get(p), scale) for p in live}
            tail = {p: _dot2(sc[p][1], uinc, NN) for p in live}
            out = list(carry)
            for p in live:
                rsum, acc = carry[p]
                w = jnp.exp(sc[p][0] + tail[p] + rsum)
                if p in valid:
                    w = jnp.where(valid[p], w, 0.0)
                acc = acc + _dot(w.astype(BF16), vb, NN)
                rs_scr[parts[p], :] = jnp.where(lane == j, rsum, rs_scr[parts[p], :])
                out[p] = (rsum + tail[p][:, 0:1], acc)
            return tuple(out)

        carry = tuple((jnp.zeros((rq, 1), F32), jnp.zeros((rq, SBA_HD), F32)) for _ in parts)
        nfull = i * band
        for band_pos in reversed(range(band)):
            carry = tile(nfull + band_pos, carry, band_pos)
        if band % 2 == 0:
            @pl.when(nfull > 0)
            def _():
                qk_into(nfull - 1, qk_a)

            def pair(m, c):
                ja = nfull - 1 - 2 * m
                qk_into(ja - 1, qk_b)
                c = tile(ja, c, None, qk_a)
                qk_into(jnp.maximum(ja - 2, 0), qk_a)
                return tile(ja - 1, c, None, qk_b)

            carry = lax.fori_loop(0, nfull // 2, pair, carry)
        else:
            carry = lax.fori_loop(0, nfull, lambda jj, c: tile(nfull - 1 - jj, c), carry)
        rs_ref[...] = rs_scr[...]
        for p, ps in enumerate(parts):
            acc = carry[p][1]
            o_ref[ps, :] = acc
            gv = g_ref[ps, :]
            y_ref[ps, :] = (acc * (gv * _sigmoid(gv))).astype(BF16)

    goff = g_off // SBA_HD
    blk = lambda off: pl.BlockSpec((tq, SBA_HD), lambda h, i: (i, h + off))
    full = lambda off: pl.BlockSpec((l, SBA_HD), lambda h, i: (0, h + off))
    out = pl.BlockSpec((tq, SBA_HD), lambda h, i: (i, h))
    return pl.pallas_call(
        body, name=name, grid=(nh, l // tq), in_specs=[blk(0), full(nh), full(2 * nh), blk(goff)], out_specs=[out, out, out],
        scratch_shapes=[pltpu.VMEM((tq, LANES), F32), pltpu.VMEM((tq, tk), F32), pltpu.VMEM((tq, tk), F32)],
        out_shape=[jax.ShapeDtypeStruct((l, d), F32), jax.ShapeDtypeStruct((l, d), BF16), jax.ShapeDtypeStruct((l, d), F32)],
        compiler_params=_params("parallel", "arbitrary"),
    )(proj_c, proj_c, proj_c, proj_a)


def _sba_bwd(dys, o, rs, proj_a, proj_c, d, g_off, name):
    l = proj_c.shape[0]
    nh = d // SBA_HD
    tq, tk = _sba_tiles(l, SBA_TQ_BWD)
    band = tq // tk
    scale = 1.0 / math.sqrt(SBA_HD)

    def body(dy_ref, o_ref, rs_ref, q_ref, k_ref, v_ref, g_ref, dq_ref, dk_ref, dv_ref, dg_ref, dk_acc, dv_acc,
             qk_a, qk_b, dw_a, dw_b):
        i = pl.program_id(1)
        nq = pl.num_programs(1)

        @pl.when(i == 0)
        def _():
            dk_acc[...] = jnp.zeros_like(dk_acc)
            dv_acc[...] = jnp.zeros_like(dv_acc)

        ki = lax.broadcasted_iota(jnp.int32, (tk, tk), 0)
        kj = lax.broadcasted_iota(jnp.int32, (tk, tk), 1)
        uinc = (ki >= kj).astype(BF16)
        ulow = (ki < kj).astype(BF16)
        lane = lax.broadcasted_iota(jnp.int32, (tq, LANES), 1)
        q = q_ref[...]
        gv = g_ref[...]
        sg = _sigmoid(gv)
        dyv = dy_ref[...]
        dg_ref[...] = (dyv * o_ref[...] * (sg * (1.0 + gv * (1.0 - sg)))).astype(BF16)
        do = dyv * (gv * sg)
        do16 = do.astype(BF16)
        q_t = q.astype(F32).T.astype(BF16)
        do_t = do.T.astype(BF16)

        def ahead(j, qk_scr, dw_scr):
            start = pl.multiple_of(j * tk, tk)
            qk_scr[...] = _dot(q, k_ref[pl.ds(start, tk), :], NT)
            dw_scr[...] = _dot(do16, v_ref[pl.ds(start, tk), :], NT)

        def tile(j, carry, band_pos=None, qk_scr=None, dw_scr=None):
            r0 = 0 if band_pos is None else band_pos * tk
            epre, dq = carry[0][r0:], carry[1][r0:]
            qr, dor = q[r0:], do16[r0:]
            start = pl.multiple_of(j * tk, tk)
            kb = k_ref[pl.ds(start, tk), :]
            rsum = jnp.sum(jnp.where(lane[r0:] == j, rs_ref[r0:, :], 0.0), axis=1, keepdims=True)
            valid = None
            if band_pos is not None:
                valid = (lax.broadcasted_iota(jnp.int32, (tq - r0, tk), 1) < lax.broadcasted_iota(jnp.int32, (tq - r0, tk), 0))
            if qk_scr is None:
                qk = _dot(qr, kb, NT)
                dw = _dot(dor, v_ref[pl.ds(start, tk), :], NT)
            else:
                qk = qk_scr[...]
                dw = dw_scr[...]
            z, lk = _sba_scores(qk, valid, scale)
            w = jnp.exp(z + _dot2(lk, uinc, NN) + rsum)
            if valid is not None:
                w = jnp.where(valid, w, 0.0)
            e = w * dw
            epx = _dot2(e, ulow, NN)
            dz = (e - jnp.exp(z + lk) * (e + epre + epx)) * scale
            if valid is not None:
                dz = jnp.where(valid, dz, 0.0)
            dz16 = dz.astype(BF16)
            dq = dq + _dot(dz16, kb, NN)
            dk_acc[j] += _dot(q_t[:, r0:], dz16, NN)
            dv_acc[j] += _dot(do_t[:, r0:], w.astype(BF16), NN)
            epre = epre + epx[:, tk - 1:tk] + e[:, tk - 1:tk]
            if r0:
                epre = jnp.concatenate([carry[0][:r0], epre], axis=0)
                dq = jnp.concatenate([carry[1][:r0], dq], axis=0)
            return epre, dq

        carry = (jnp.zeros((tq, 1), F32), jnp.zeros((tq, SBA_HD), F32))
        nfull = i * band
        if band % 2 == 0:
            @pl.when(nfull > 0)
            def _():
                ahead(0, qk_a, dw_a)

            def pair(m, c):
                ja = 2 * m
                ahead(ja + 1, qk_b, dw_b)
                c = tile(ja, c, None, qk_a, dw_a)
                ahead(jnp.minimum(ja + 2, nfull - 1), qk_a, dw_a)
                return tile(ja + 1, c, None, qk_b, dw_b)

            carry = lax.fori_loop(0, nfull // 2, pair, carry)
        else:
            carry = lax.fori_loop(0, nfull, lambda j, c: tile(j, c), carry)
        for band_pos in range(band):
            carry = tile(nfull + band_pos, carry, band_pos)
        dq_ref[...] = carry[1].astype(BF16)

        @pl.when(i == nq - 1)
        def _():
            for jj in range(l // tk):
                dk_ref[jj * tk:(jj + 1) * tk, :] = dk_acc[jj].T.astype(BF16)
                dv_ref[jj * tk:(jj + 1) * tk, :] = dv_acc[jj].T.astype(BF16)

    goff = g_off // SBA_HD
    blk = lambda off: pl.BlockSpec((tq, SBA_HD), lambda h, i: (i, h + off))
    full = lambda off: pl.BlockSpec((l, SBA_HD), lambda h, i: (0, h + off))
    out = pl.BlockSpec((tq, SBA_HD), lambda h, i: (i, h))
    outfull = pl.BlockSpec((l, SBA_HD), lambda h, i: (0, h))
    sd = jax.ShapeDtypeStruct((l, d), BF16)
    return pl.pallas_call(
        body, name=name, grid=(nh, l // tq),
        in_specs=[blk(nh), out, out, blk(0), full(nh), full(2 * nh), blk(goff)],
        out_specs=[out, outfull, outfull, out], out_shape=[sd, sd, sd, sd],
        scratch_shapes=[pltpu.VMEM((l // tk, SBA_HD, tk), F32)] * 2 + [pltpu.VMEM((tq, tk), F32)] * 4,
        compiler_params=_params("parallel", "arbitrary"),
    )(dys, o, rs, proj_c, proj_c, proj_c, proj_a)


def _adamw_math(w, g, m, v):
    m = ADAM_B1 * m + (1.0 - ADAM_B1) * g
    v = ADAM_B2 * v + (1.0 - ADAM_B2) * (g * g)
    m_hat = m / (1.0 - ADAM_B1 ** ADAM_STEP)
    v_hat = v / (1.0 - ADAM_B2 ** ADAM_STEP)
    delta = -ADAM_LR * (m_hat / (jnp.sqrt(v_hat) + ADAM_EPS) + ADAM_WD * w)
    return delta, m, v


def _adamw(w, g, m, v, name):
    a, r, c = w.shape
    tr = _pick(r, max(8, (1 << 19) // c // 8 * 8), 8)

    def body(w_ref, g_ref, m_ref, v_ref, d_ref, nm_ref, nv_ref):
        dl, nm, nv = _adamw_math(w_ref[...], g_ref[...], m_ref[...], v_ref[...])
        d_ref[...] = dl
        nm_ref[...] = nm
        nv_ref[...] = nv

    blk = pl.BlockSpec((1, tr, c), lambda i, j: (i, j, 0))
    sd = jax.ShapeDtypeStruct(w.shape, F32)
    return pl.pallas_call(
        body, name=name, grid=(a, r // tr), in_specs=[blk] * 4, out_specs=[blk] * 3, out_shape=[sd] * 3,
        compiler_params=_params("parallel", "parallel"),
    )(w, g, m, v)


def _dims(d):
    cdim = d + 2 * NGROUPS * NSTATE
    heads = d // SSD_HD
    r_heads = heads // NGROUPS
    g_off = d + cdim + NGROUPS * LANES
    na = g_off + d
    nc = 3 * d
    return cdim, heads, r_heads, na, nc, g_off


def _pack_w_in(w_in, d):
    cdim, heads, r_heads, na, nc, g_off = _dims(d)
    o = d + cdim
    w_dt = w_in[:, o:o + heads].reshape(d, NGROUPS, r_heads)
    w_dt = jnp.pad(w_dt, ((0, 0), (0, 0), (0, LANES - r_heads))).reshape(d, NGROUPS * LANES)
    return jnp.concatenate([w_in[:, :o], w_dt, w_in[:, o + heads + nc:]], axis=1), w_in[:, o + heads:o + heads + nc]


def _unpack_w_in(ga, gc, d):
    cdim, heads, r_heads, na, nc, g_off = _dims(d)
    o = d + cdim
    g_dt = ga[:, o:g_off].reshape(d, NGROUPS, LANES)[:, :, :r_heads].reshape(d, heads)
    return jnp.concatenate([ga[:, :o], g_dt, gc, ga[:, g_off:]], axis=1)


def _group_vec(v, r_heads):
    return jnp.pad(v.reshape(NGROUPS, 1, r_heads), ((0, 0), (0, 0), (0, LANES - r_heads)))


def _layer_fwd(x, p, d, tag):
    cdim, heads, r_heads, na, nc, g_off = _dims(d)
    h = _rmsnorm_fwd(x, p["norm_w"], f"norm_f{tag}")
    proj_a = _matmul(h, p["wa"], "nn", F32, f"inproj_a{tag}")
    proj_c = _matmul(h, p["wc"], "nn", BF16, f"inproj_c{tag}")
    acc, xbc = _conv_fwd(proj_a, p["conv_w"], p["conv_b"], d, f"conv_f{tag}")
    y, hs = _ssd_fwd(xbc, proj_a, p["bias_g"], p["alog_g"], p["dsk_g"], d, f"ssd_f{tag}")
    y_ssd = _gatenorm_fwd(y, proj_a, p["ssd_norm_w"], f"gate_f{tag}")
    o, y_sba, rs = _sba_fwd(proj_a, proj_c, d, g_off, f"sba_f{tag}")
    mix = jnp.concatenate([y_ssd, y_sba], axis=1)
    x_next = _matmul(mix, p["w_out"], "nn", F32, f"outproj{tag}", add=x)
    return x_next, dict(x=x, h=h, proj_a=proj_a, proj_c=proj_c, acc=acc, xbc=xbc, y=y, hs=hs, o=o, rs=rs, mix=mix)


def _layer_bwd(dxn, s, p, d, tag):
    cdim, heads, r_heads, na, nc, g_off = _dims(d)
    dxn16 = dxn.astype(BF16)
    dmix = _matmul(dxn16, p["w_out"], "nt", F32, f"dmix{tag}")
    g_w_out = _matmul(s["mix"], dxn16, "tn", F32, f"dwout{tag}")
    dq, dk, dv, dg = _sba_bwd(dmix, s["o"], s["rs"], s["proj_a"], s["proj_c"], d, g_off, f"sba_b{tag}")
    dy, dz, g_ssd_norm = _gatenorm_bwd(dmix, s["y"], s["proj_a"], p["ssd_norm_w"], f"gate_b{tag}")
    dx_s, db_s, dc_s, ddt, sums = _ssd_bwd(dy, s["xbc"], s["proj_a"], s["hs"], p["bias_g"], p["alog_g"], p["dsk_g"], d, f"ssd_b{tag}")
    dxbc = jnp.concatenate([dx_s, db_s, dc_s], axis=1)
    du, g_conv = _conv_bwd(dxbc, s["acc"], s["proj_a"], p["conv_w"], d, f"conv_b{tag}")
    dproj_a = jnp.concatenate([dz, du, ddt.astype(BF16), dg], axis=1)
    dproj_c = jnp.concatenate([dq, dk, dv], axis=1)
    g_wa = _matmul(s["h"], dproj_a, "tn", F32, f"dwin_a{tag}")
    g_wc = _matmul(s["h"], dproj_c, "tn", F32, f"dwin_c{tag}")
    dh = _matmul(dproj_a, p["wa"], "nt", F32, f"dh_a{tag}")
    dh = _matmul(dproj_c, p["wc"], "nt", F32, f"dh_c{tag}", add=dh)
    dx, g_norm = _rmsnorm_bwd(dh, s["x"], p["norm_w"], dxn, f"norm_b{tag}")
    a_neg = -jnp.exp(p["alog_g"][:, 0, :r_heads].reshape(heads))
    grads = dict(
        norm_w=g_norm[0], w_in=_unpack_w_in(g_wa, g_wc, d), conv_w=g_conv[:KCONV], conv_b=g_conv[KCONV],
        dt_bias=sums[:, 0, :r_heads].reshape(heads), a_log=sums[:, 1, :r_heads].reshape(heads) * a_neg,
        d_skip=sums[:, 2, :r_heads].reshape(heads), ssd_norm_w=g_ssd_norm[0], w_out=g_w_out)
    return dx, grads


def _local_step(x, target, w_in16, w_out16, conv_w, small):
    l, d = x.shape
    depth = w_in16.shape[0]
    r_heads = _dims(d)[2]
    layers = []
    for i in range(depth):
        wa, wc = _pack_w_in(w_in16[i], d)
        layers.append(dict(
            norm_w=small["norm_w"][i][None], wa=wa, wc=wc, conv_w=conv_w[i], conv_b=small["conv_b"][i][None],
            bias_g=_group_vec(small["dt_bias"][i], r_heads), alog_g=_group_vec(small["a_log"][i], r_heads),
            dsk_g=_group_vec(small["d_skip"][i], r_heads), ssd_norm_w=small["ssd_norm_w"][i][None], w_out=w_out16[i]))
    saved = []
    hcur = x
    for i in range(depth):
        hcur, s = _layer_fwd(hcur, layers[i], d, str(i))
        saved.append(s)
    dh, g_final, loss = _final_loss(hcur, small["final_norm_w"][None], target, "final_loss")
    grads = [None] * depth
    for i in reversed(range(depth)):
        dh, grads[i] = _layer_bwd(dh, saved[i], layers[i], d, str(i))
    stacked = {k: jnp.stack([g[k] for g in grads]) for k in grads[0]}
    stacked["final_norm_w"] = g_final[0]
    return loss[0, 0], dh, stacked


HBM = pl.BlockSpec(memory_space=pl.ANY)
NCHIP = 4
NDEV = 8


def _mesh_pos():
    x, y, c = lax.axis_index("x"), lax.axis_index("y"), lax.axis_index("c")
    chips = [(1 - x, y), (x, 1 - y), (1 - x, 1 - y)]
    return x, y, c, chips


def _remote(src, dst, send_sem, recv_sem, dev):
    return pltpu.make_async_remote_copy(src_ref=src, dst_ref=dst, send_sem=send_sem, recv_sem=recv_sem,
                                        device_id=dev, device_id_type=MESH)


def _gather_weights(shards):
    n = len(shards)
    hl = shards[0].shape[0] // 2

    def body(*refs):
        ins, outs = refs[:n], refs[n:2 * n]
        send, recv = refs[2 * n:]
        x, y, c, chips = _mesh_pos()
        k = 2 * x + y
        half = pl.ds(c * hl, hl)
        other = pl.ds((1 - c) * hl, hl)
        for a in range(n):
            for j, (px, py) in enumerate(chips):
                _remote(ins[a].at[half], outs[a].at[k, half], send.at[a, j], recv.at[a, j], (px, py, c)).start()
        for a in range(n):
            for j, (px, py) in enumerate(chips):
                kj = 2 * px + py
                got = outs[a].at[kj, half]
                _remote(got, got, send.at[a, j], recv.at[a, j], (px, py, c)).wait_recv()
                _remote(got, got, send.at[a, 3 + j], recv.at[a, 3 + j], (x, y, 1 - c)).start()
        for a in range(n):
            for j, (px, py) in enumerate(chips):
                kj = 2 * px + py
                _remote(outs[a].at[kj, other], outs[a].at[kj, other], send.at[a, 3 + j], recv.at[a, 3 + j], (x, y, 1 - c)).wait_recv()
            for j, (px, py) in enumerate(chips):
                kj = 2 * px + py
                _remote(ins[a].at[half], outs[a].at[k, half], send.at[a, j], recv.at[a, j], (px, py, c)).wait_send()
                _remote(outs[a].at[kj, half], outs[a].at[kj, half], send.at[a, 3 + j], recv.at[a, 3 + j], (x, y, 1 - c)).wait_send()

    return pl.pallas_call(
        body, name="gather_weights", in_specs=[HBM] * n, out_specs=[HBM] * n,
        out_shape=[jax.ShapeDtypeStruct((NCHIP,) + s.shape, s.dtype) for s in shards],
        scratch_shapes=[pltpu.SemaphoreType.DMA((n, 6)), pltpu.SemaphoreType.DMA((n, 6))],
    )(*shards)


def _swap_halves(parts):
    n = len(parts)
    hl = parts[0].shape[1] // 2

    def body(*refs):
        ins, outs = refs[:n], refs[n:2 * n]
        send, recv = refs[2 * n:]
        x, y, c, _ = _mesh_pos()
        cps = [_remote(ins[a].at[:, pl.ds((1 - c) * hl, hl)], outs[a], send.at[a], recv.at[a], (x, y, 1 - c)) for a in range(n)]
        for cp in cps:
            cp.start()
        for cp in cps:
            cp.wait()

    return pl.pallas_call(
        body, name="grad_swap_halves", in_specs=[HBM] * n, out_specs=[HBM] * n,
        out_shape=[jax.ShapeDtypeStruct((NCHIP, hl) + p.shape[2:], p.dtype) for p in parts],
        scratch_shapes=[pltpu.SemaphoreType.DMA((n,)), pltpu.SemaphoreType.DMA((n,))],
    )(*parts)


def _exchange_shards(parts):
    n = len(parts)

    def body(*refs):
        ins, outs = refs[:n], refs[n:2 * n]
        send, recv = refs[2 * n:]
        x, y, c, chips = _mesh_pos()
        k = 2 * x + y
        work = []
        for a in range(n):
            for j, (px, py) in enumerate(chips):
                cp = _remote(ins[a].at[2 * px + py], outs[a].at[k], send.at[a, j], recv.at[a, j], (px, py, c))
                cp.start()
                work.append(cp)
        for w in work:
            w.wait()

    return pl.pallas_call(
        body, name="grad_exchange", in_specs=[HBM] * n, out_specs=[HBM] * n,
        out_shape=[jax.ShapeDtypeStruct(p.shape, p.dtype) for p in parts],
        scratch_shapes=[pltpu.SemaphoreType.DMA((n, 3)), pltpu.SemaphoreType.DMA((n, 3))],
    )(*parts)


def _swap_reduced(halves):
    n = len(halves)

    def body(*refs):
        ins, outs = refs[:n], refs[n:2 * n]
        send, recv = refs[2 * n:]
        x, y, c, _ = _mesh_pos()
        work = [_remote(ins[a], outs[a], send.at[a], recv.at[a], (x, y, 1 - c)) for a in range(n)]
        for w in work:
            w.start()
        for w in work:
            w.wait()

    return pl.pallas_call(
        body, name="grad_swap_reduced", in_specs=[HBM] * n, out_specs=[HBM] * n,
        out_shape=[jax.ShapeDtypeStruct(h.shape, h.dtype) for h in halves],
        scratch_shapes=[pltpu.SemaphoreType.DMA((n,)), pltpu.SemaphoreType.DMA((n,))],
    )(*halves)


def _all_sum_small(vec, name):
    r = vec.shape[0]

    def body(v_ref, o_ref, buf, send, recv):
        x, y, c, _ = _mesh_pos()
        me = 4 * x + 2 * y + c
        buf[me] = v_ref[...]
        cps = []
        for mask in range(1, NDEV):
            fx, fy, fc = (mask >> 2) & 1, (mask >> 1) & 1, mask & 1
            peer = (1 - x if fx else x, 1 - y if fy else y, 1 - c if fc else c)
            cp = _remote(v_ref, buf.at[me], send.at[mask - 1], recv.at[mask - 1], peer)
            cp.start()
            cps.append(cp)
        for cp in cps:
            cp.wait()
        total = buf[0]
        for dev in range(1, NDEV):
            total = total + buf[dev]
        o_ref[...] = total

    vm = pl.BlockSpec(memory_space=pltpu.VMEM)
    return pl.pallas_call(
        body, name=name, in_specs=[vm], out_specs=vm, out_shape=jax.ShapeDtypeStruct((r, LANES), F32),
        scratch_shapes=[pltpu.VMEM((NDEV, r, LANES), F32), pltpu.SemaphoreType.DMA((NDEV - 1,)), pltpu.SemaphoreType.DMA((NDEV - 1,))],
    )(vec)


def _add_pairs(a, b, out_dtype, name):
    n0, n1, r, c = a.shape
    tr = _pick(r, max(8, (1 << 19) // c // 8 * 8), 8)

    def body(a_ref, b_ref, o_ref):
        o_ref[...] = (a_ref[...] + b_ref[...]).astype(out_dtype)

    blk = pl.BlockSpec((1, 1, tr, c), lambda i, j, t: (i, j, t, 0))
    return pl.pallas_call(
        body, name=name, grid=(n0, n1, r // tr), in_specs=[blk, blk], out_specs=blk,
        out_shape=jax.ShapeDtypeStruct(a.shape, out_dtype), compiler_params=_params("parallel", "parallel", "parallel"),
    )(a, b)


def _sum_chips(p, name):
    _, hl, r, c = p.shape
    tr = _pick(r, max(16, (1 << 19) // c // 16 * 16), 16)

    def body(p_ref, o_ref):
        total = p_ref[0].astype(F32)
        for j in range(1, NCHIP):
            total = total + p_ref[j].astype(F32)
        o_ref[...] = total

    return pl.pallas_call(
        body, name=name, grid=(hl, r // tr),
        in_specs=[pl.BlockSpec((NCHIP, 1, tr, c), lambda i, t: (0, i, t, 0))],
        out_specs=pl.BlockSpec((1, tr, c), lambda i, t: (i, t, 0)),
        out_shape=jax.ShapeDtypeStruct((hl, r, c), F32), compiler_params=_params("parallel", "parallel"),
    )(p)


def _reduce_scatter(parts):
    c = lax.axis_index("c")
    k = 2 * lax.axis_index("x") + lax.axis_index("y")
    hl = parts[0].shape[1] // 2
    theirs = _swap_halves(parts)
    mine = [lax.dynamic_slice_in_dim(p, c * hl, hl, axis=1) for p in parts]
    chip_sum = [_add_pairs(m, t, BF16, f"grad_pair_sum{i}") for i, (m, t) in enumerate(zip(mine, theirs))]
    gathered = _exchange_shards(chip_sum)
    gathered = [lax.dynamic_update_slice_in_dim(g, lax.dynamic_slice_in_dim(s, k, 1, axis=0), k, axis=0)
                for g, s in zip(gathered, chip_sum)]
    halves = [_sum_chips(g, f"grad_chip_sum{i}") for i, g in enumerate(gathered)]
    sibling = _swap_reduced(halves)
    south = c == 0
    return [jnp.concatenate([jnp.where(south, h, s), jnp.where(south, s, h)], axis=0) for h, s in zip(halves, sibling)]


SMALL = ("norm_w", "conv_w", "conv_b", "dt_bias", "a_log", "d_skip", "ssd_norm_w", "final_norm_w")


def _pack(arrays):
    flat = jnp.concatenate([a.reshape(-1).astype(F32) for a in arrays])
    rows = -(-flat.shape[0] // (8 * LANES)) * 8
    return jnp.pad(flat, (0, rows * LANES - flat.shape[0])).reshape(rows, LANES)


def _unpack(vec, shapes):
    flat = vec.reshape(-1)
    out, pos = [], 0
    for s in shapes:
        n = math.prod(s)
        out.append(flat[pos:pos + n].reshape(s))
        pos += n
    return out


def kernel(x, norm_w, w_in, conv_w, conv_b, dt_bias, a_log, d_skip, ssd_norm_w, w_out, final_norm_w, loss_target, m_norm_w, m_w_in, m_conv_w, m_conv_b, m_dt_bias, m_a_log, m_d_skip, m_ssd_norm_w, m_w_out, m_final_norm_w, v_norm_w, v_w_in, v_conv_w, v_conv_b, v_dt_bias, v_a_log, v_d_skip, v_ssd_norm_w, v_w_out, v_final_norm_w):
    depth, d, ics = w_in.shape
    cs = conv_w.shape[2]
    xi, yi, ci = lax.axis_index("x"), lax.axis_index("y"), lax.axis_index("c")
    k = 2 * xi + yi

    placed = lax.dynamic_update_slice(jnp.zeros((depth, KCONV, NCHIP, cs), F32), conv_w[:, :, None, :], (0, 0, k, 0))
    placed = jnp.where(ci == 0, placed, 0.0)
    conv_full = _unpack(_all_sum_small(_pack([placed]), "gather_conv_w"), [(depth, KCONV, NCHIP * cs)])[0]

    w_in16, w_out16 = w_in.astype(BF16), w_out.astype(BF16)
    wi_all, wo_all = _gather_weights([w_in16, w_out16])
    wi_all = lax.dynamic_update_slice_in_dim(wi_all, w_in16[None], k, axis=0)
    wo_all = lax.dynamic_update_slice_in_dim(wo_all, w_out16[None], k, axis=0)
    w_in_full = wi_all.transpose(1, 2, 0, 3).reshape(depth, d, NCHIP * ics)
    w_out_full = wo_all.transpose(1, 0, 2, 3).reshape(depth, 2 * d, d)

    small = dict(norm_w=norm_w, conv_b=conv_b, dt_bias=dt_bias, a_log=a_log, d_skip=d_skip, ssd_norm_w=ssd_norm_w, final_norm_w=final_norm_w)
    loss_local, gx, g = _local_step(x[0], loss_target[0], w_in_full, w_out_full, conv_full, small)

    g_in = g["w_in"].reshape(depth, d, NCHIP, ics).transpose(2, 0, 1, 3)
    g_out = g["w_out"].reshape(depth, NCHIP, 2 * d // NCHIP, d).transpose(1, 0, 2, 3)
    grad_w_in, grad_w_out = _reduce_scatter([g_in, g_out])

    names = list(SMALL)
    total = _all_sum_small(_pack([g[n] for n in names] + [loss_local]), "sum_small_grads")
    parts = _unpack(total, [g[n].shape for n in names] + [()])
    grads = dict(zip(names, parts[:-1]))
    loss = parts[-1]
    grads["conv_w"] = lax.dynamic_index_in_dim(grads["conv_w"].reshape(depth, KCONV, NCHIP, cs), k, axis=2, keepdims=False)
    grads["w_in"], grads["w_out"] = grad_w_in, grad_w_out

    w = dict(norm_w=norm_w, w_in=w_in, conv_w=conv_w, conv_b=conv_b, dt_bias=dt_bias, a_log=a_log, d_skip=d_skip,
             ssd_norm_w=ssd_norm_w, w_out=w_out, final_norm_w=final_norm_w)
    m = dict(norm_w=m_norm_w, w_in=m_w_in, conv_w=m_conv_w, conv_b=m_conv_b, dt_bias=m_dt_bias, a_log=m_a_log, d_skip=m_d_skip,
             ssd_norm_w=m_ssd_norm_w, w_out=m_w_out, final_norm_w=m_final_norm_w)
    v = dict(norm_w=v_norm_w, w_in=v_w_in, conv_w=v_conv_w, conv_b=v_conv_b, dt_bias=v_dt_bias, a_log=v_a_log, d_skip=v_d_skip,
             ssd_norm_w=v_ssd_norm_w, w_out=v_w_out, final_norm_w=v_final_norm_w)
    delta, new_m, new_v = {}, {}, {}
    for n in ("w_in", "w_out"):
        delta[n], new_m[n], new_v[n] = _adamw(w[n], grads[n], m[n], v[n], f"adamw_{n}")
    shapes = [w[n].shape for n in names]
    packed = [_pack([t[n] for n in names])[None] for t in (w, grads, m, v)]
    for res, out in zip(_adamw(*packed, "adamw_small"), (delta, new_m, new_v)):
        out.update(zip(names, _unpack(res[0], shapes)))

    order = ("norm_w", "w_in", "conv_w", "conv_b", "dt_bias", "a_log", "d_skip", "ssd_norm_w", "w_out", "final_norm_w")
    return (loss, gx[None], *[grads[n] for n in order], *[delta[n] for n in order], *[new_m[n] for n in order], *[new_v[n] for n in order])
```

```python
import functools
import math

import jax
import jax.numpy as jnp
from jax import lax
from jax.experimental import pallas as pl
from jax.experimental.pallas import tpu as pltpu

F32, BF16 = jnp.float32, jnp.bfloat16
EPS = 1e-6
CHUNK = 64
NGROUPS = 4
NSTATE = 128
KCONV = 4
SSD_HD = 64
SSD_HD_SHIFT = SSD_HD.bit_length() - 1
SBA_HD = 128
LANES = 128
VMEM_LIMIT = 56 * 1024 * 1024

ADAM_LR, ADAM_B1, ADAM_B2, ADAM_EPS, ADAM_WD, ADAM_STEP = 0.001, 0.9, 0.999, 1e-08, 0.01, 10

NN = ((1,), (0,))
NT = ((1,), (1,))
TN = ((0,), (0,))
MESH = pl.DeviceIdType.MESH


def _dot(a, b, dims):
    return lax.dot_general(a, b, (dims, ((), ())), preferred_element_type=F32)


def _params(*sem):
    return pltpu.CompilerParams(dimension_semantics=sem, vmem_limit_bytes=VMEM_LIMIT)


def _pick(n, target, mult):
    best = None
    for d in range(mult, min(n, target) + 1, mult):
        if n % d == 0:
            best = d
    return n if best is None else best


def _split3(x):
    x1 = x.astype(BF16)
    r1 = x - x1.astype(F32)
    x2 = r1.astype(BF16)
    x3 = (r1 - x2.astype(F32)).astype(BF16)
    return x1, x2, x3


def _split2(x):
    x1 = x.astype(BF16)
    return x1, (x - x1.astype(F32)).astype(BF16)


def _sigmoid(x):
    return 1.0 / (1.0 + jnp.exp(-x))


def _softplus(x):
    return jnp.maximum(x, 0.0) + jnp.log1p(jnp.exp(-jnp.abs(x)))


def _matmul(a, b, mode, out_dtype, name, add=None, tm=1024, tn=1024, tk=2048):
    if mode == "nn":
        (m, k), n = a.shape, b.shape[1]
    elif mode == "nt":
        (m, k), n = a.shape, b.shape[0]
    else:
        (k, m), n = a.shape, b.shape[1]
    tm, tn, tk = _pick(m, tm, LANES), _pick(n, tn, LANES), _pick(k, tk, LANES)
    nk = k // tk
    dims = {"nn": NN, "nt": NT, "tn": TN}[mode]
    a_spec = pl.BlockSpec((tk, tm), lambda i, j, kk: (kk, i)) if mode == "tn" else pl.BlockSpec((tm, tk), lambda i, j, kk: (i, kk))
    b_spec = pl.BlockSpec((tn, tk), lambda i, j, kk: (j, kk)) if mode == "nt" else pl.BlockSpec((tk, tn), lambda i, j, kk: (kk, j))
    o_spec = pl.BlockSpec((tm, tn), lambda i, j, kk: (i, j))
    has_add = add is not None

    def body(*refs):
        a_ref, b_ref = refs[0], refs[1]
        add_ref = refs[2] if has_add else None
        o_ref, acc_ref = refs[-2], refs[-1]
        kk = pl.program_id(2)
        part = _dot(a_ref[...], b_ref[...], dims)

        def finish(total):
            if has_add:
                total = total + add_ref[...].astype(F32)
            o_ref[...] = total.astype(out_dtype)

        if nk == 1:
            finish(part)
        else:
            @pl.when(kk == 0)
            def _():
                acc_ref[...] = part

            @pl.when(jnp.logical_and(kk > 0, kk < nk - 1))
            def _():
                acc_ref[...] += part

            @pl.when(kk == nk - 1)
            def _():
                finish(acc_ref[...] + part)

    in_specs = [a_spec, b_spec] + ([o_spec] if has_add else [])
    args = (a, b) + ((add,) if has_add else ())
    return pl.pallas_call(
        body, name=name, grid=(m // tm, n // tn, nk), in_specs=in_specs, out_specs=o_spec,
        out_shape=jax.ShapeDtypeStruct((m, n), out_dtype),
        scratch_shapes=[pltpu.VMEM((tm, tn), F32)],
        compiler_params=_params("parallel", "parallel", "arbitrary"),
    )(*args)


def _rmsnorm_fwd(x, w, name):
    l, d = x.shape
    tr = _pick(l, 512, 8)

    def body(x_ref, w_ref, h_ref):
        xv = x_ref[...]
        r = lax.rsqrt(jnp.mean(xv * xv, axis=-1, keepdims=True) + EPS)
        h_ref[...] = (xv * r * w_ref[...]).astype(BF16)

    return pl.pallas_call(
        body, name=name, grid=(l // tr,),
        in_specs=[pl.BlockSpec((tr, d), lambda i: (i, 0)), pl.BlockSpec((1, d), lambda i: (0, 0))],
        out_specs=pl.BlockSpec((tr, d), lambda i: (i, 0)),
        out_shape=jax.ShapeDtypeStruct((l, d), BF16), compiler_params=_params("parallel"),
    )(x, w)


def _rmsnorm_bwd(dh, x, w, dres, name):
    l, d = x.shape
    tr = _pick(l, 256, 8)

    def body(dh_ref, x_ref, w_ref, dres_ref, dx_ref, dw_ref):
        xv = x_ref[...]
        r = lax.rsqrt(jnp.mean(xv * xv, axis=-1, keepdims=True) + EPS)
        xh = xv * r
        dhv = dh_ref[...]
        dxh = dhv * w_ref[...]
        dx_ref[...] = dres_ref[...] + r * (dxh - xh * jnp.mean(dxh * xh, axis=-1, keepdims=True))
        part = jnp.sum(dhv * xh, axis=0, keepdims=True)

        @pl.when(pl.program_id(0) == 0)
        def _():
            dw_ref[...] = part

        @pl.when(pl.program_id(0) > 0)
        def _():
            dw_ref[...] += part

    row = pl.BlockSpec((tr, d), lambda i: (i, 0))
    vec = pl.BlockSpec((1, d), lambda i: (0, 0))
    return pl.pallas_call(
        body, name=name, grid=(l // tr,), in_specs=[row, row, vec, row], out_specs=[row, vec],
        out_shape=[jax.ShapeDtypeStruct((l, d), F32), jax.ShapeDtypeStruct((1, d), F32)],
        compiler_params=_params("arbitrary"),
    )(dh, x, w, dres)


def _final_loss(h, w, target, name):
    l, d = h.shape
    tr = _pick(l, 256, 8)

    def body(h_ref, w_ref, t_ref, dh_ref, dw_ref, loss_ref):
        xv = h_ref[...]
        r = lax.rsqrt(jnp.mean(xv * xv, axis=-1, keepdims=True) + EPS)
        xh = xv * r
        err = xh * w_ref[...] - t_ref[...]
        dy = err * (1.0 / d)
        dxh = dy * w_ref[...]
        dh_ref[...] = r * (dxh - xh * jnp.mean(dxh * xh, axis=-1, keepdims=True))
        part = jnp.sum(dy * xh, axis=0, keepdims=True)
        lpart = jnp.zeros((8, LANES), F32) + 0.5 * jnp.sum(jnp.mean(err * err, axis=-1, keepdims=True))

        @pl.when(pl.program_id(0) == 0)
        def _():
            dw_ref[...] = part
            loss_ref[...] = lpart

        @pl.when(pl.program_id(0) > 0)
        def _():
            dw_ref[...] += part
            loss_ref[...] += lpart

    row = pl.BlockSpec((tr, d), lambda i: (i, 0))
    vec = pl.BlockSpec((1, d), lambda i: (0, 0))
    return pl.pallas_call(
        body, name=name, grid=(l // tr,), in_specs=[row, vec, row],
        out_specs=[row, vec, pl.BlockSpec((8, LANES), lambda i: (0, 0))],
        out_shape=[jax.ShapeDtypeStruct((l, d), F32), jax.ShapeDtypeStruct((1, d), F32), jax.ShapeDtypeStruct((8, LANES), F32)],
        compiler_params=_params("arbitrary"),
    )(h, w, target)


def _gatenorm_fwd(y, proj_a, w, name):
    l, d = y.shape
    dg = d // NGROUPS
    tr = _pick(l, 256, 8)

    def body(y_ref, z_ref, w_ref, o_ref):
        for g in range(NGROUPS):
            sl = slice(g * dg, (g + 1) * dg)
            zv = z_ref[:, sl]
            u = y_ref[:, sl] * (zv * _sigmoid(zv))
            r = lax.rsqrt(jnp.mean(u * u, axis=-1, keepdims=True) + EPS)
            o_ref[:, sl] = (u * r * w_ref[:, sl]).astype(BF16)

    row = pl.BlockSpec((tr, d), lambda i: (i, 0))
    return pl.pallas_call(
        body, name=name, grid=(l // tr,), in_specs=[row, row, pl.BlockSpec((1, d), lambda i: (0, 0))],
        out_specs=row, out_shape=jax.ShapeDtypeStruct((l, 2 * d), BF16), compiler_params=_params("parallel"),
    )(y, proj_a, w)


def _gatenorm_bwd(dout, y, proj_a, w, name):
    l, d = y.shape
    dg = d // NGROUPS
    tr = _pick(l, 256, 8)

    def body(do_ref, y_ref, z_ref, w_ref, dy_ref, dz_ref, dw_ref):
        first = pl.program_id(0) == 0
        for g in range(NGROUPS):
            sl = slice(g * dg, (g + 1) * dg)
            zv = z_ref[:, sl]
            sg = _sigmoid(zv)
            sz = zv * sg
            yv = y_ref[:, sl]
            u = yv * sz
            r = lax.rsqrt(jnp.mean(u * u, axis=-1, keepdims=True) + EPS)
            nh = u * r
            dov = do_ref[:, sl]
            dn = dov * w_ref[:, sl]
            du = r * (dn - nh * jnp.mean(dn * nh, axis=-1, keepdims=True))
            dy_ref[:, sl] = du * sz
            dz_ref[:, sl] = (du * yv * (sg * (1.0 + zv * (1.0 - sg)))).astype(BF16)
            part = jnp.sum(dov * nh, axis=0, keepdims=True)

            @pl.when(first)
            def _():
                dw_ref[:, sl] = part

            @pl.when(jnp.logical_not(first))
            def _():
                dw_ref[:, sl] += part

    row = pl.BlockSpec((tr, d), lambda i: (i, 0))
    vec = pl.BlockSpec((1, d), lambda i: (0, 0))
    return pl.pallas_call(
        body, name=name, grid=(l // tr,), in_specs=[row, row, row, vec], out_specs=[row, row, vec],
        out_shape=[jax.ShapeDtypeStruct((l, d), F32), jax.ShapeDtypeStruct((l, d), BF16), jax.ShapeDtypeStruct((1, d), F32)],
        compiler_params=_params("arbitrary"),
    )(dout, y, proj_a, w)


def _conv_cols(d):
    return _pick(math.gcd(d, d + 2 * NGROUPS * NSTATE), 512, LANES)


def _conv_fwd(proj_a, conv_w, conv_b, d, name):
    l = proj_a.shape[0]
    cdim = d + 2 * NGROUPS * NSTATE
    cw = _conv_cols(d)
    off = d // cw
    tl = _pick(l, 512, 8)
    hb = tl // 8

    def body(u_ref, up_ref, w_ref, b_ref, acc_ref, xbc_ref, ext):
        i = pl.program_id(1)
        ext[0:8, :] = jnp.where(i > 0, up_ref[...], 0.0)
        ext[8:8 + tl, :] = u_ref[...]
        acc = jnp.zeros((tl, cw), F32) + b_ref[...]
        for j in range(KCONV):
            acc = acc + w_ref[j:j + 1, :] * ext[pl.ds(8 - (KCONV - 1) + j, tl), :]
        acc_ref[...] = acc
        xbc_ref[...] = acc * _sigmoid(acc)

    blk = pl.BlockSpec((tl, cw), lambda c, i: (i, c))
    return pl.pallas_call(
        body, name=name, grid=(cdim // cw, l // tl),
        in_specs=[pl.BlockSpec((tl, cw), lambda c, i: (i, c + off)),
                  pl.BlockSpec((8, cw), lambda c, i: (jnp.maximum(i * hb - 1, 0), c + off)),
                  pl.BlockSpec((KCONV, cw), lambda c, i: (0, c)), pl.BlockSpec((1, cw), lambda c, i: (0, c))],
        out_specs=[blk, blk],
        out_shape=[jax.ShapeDtypeStruct((l, cdim), F32), jax.ShapeDtypeStruct((l, cdim), F32)],
        scratch_shapes=[pltpu.VMEM((tl + 8, cw), F32)], compiler_params=_params("parallel", "parallel"),
    )(proj_a, proj_a, conv_w, conv_b)


def _conv_bwd(dxbc, acc, proj_a, conv_w, d, name):
    l, cdim = acc.shape
    cw = _conv_cols(d)
    off = d // cw
    tl = _pick(l, 512, 8)
    hb = tl // 8
    nb = l // tl

    def dsilu(g, a):
        s = _sigmoid(a)
        return g * (s * (1.0 + a * (1.0 - s)))

    def body(g_ref, gn_ref, a_ref, an_ref, u_ref, up_ref, w_ref, du_ref, dw_ref, ext, dext):
        i = pl.program_id(1)
        da = dsilu(g_ref[...], a_ref[...])
        dext[0:tl, :] = da
        dext[tl:tl + 8, :] = jnp.where(i < nb - 1, dsilu(gn_ref[...], an_ref[...]), 0.0)
        ext[0:8, :] = jnp.where(i > 0, up_ref[...], 0.0)
        ext[8:8 + tl, :] = u_ref[...]
        du = jnp.zeros((tl, cw), F32)
        rows = []
        for j in range(KCONV):
            du = du + w_ref[j:j + 1, :] * dext[pl.ds(KCONV - 1 - j, tl), :]
            rows.append(jnp.sum(da * ext[pl.ds(8 - (KCONV - 1) + j, tl), :], axis=0, keepdims=True))
        rows.append(jnp.sum(da, axis=0, keepdims=True))
        du_ref[...] = du.astype(BF16)

        @pl.when(i == 0)
        def _():
            dw_ref[...] = jnp.zeros_like(dw_ref)

        for j, rv in enumerate(rows):
            dw_ref[j:j + 1, :] += rv

    blk = pl.BlockSpec((tl, cw), lambda c, i: (i, c))
    nxt = pl.BlockSpec((8, cw), lambda c, i: (jnp.minimum((i + 1) * hb, l // 8 - 1), c))
    return pl.pallas_call(
        body, name=name, grid=(cdim // cw, nb),
        in_specs=[blk, nxt, blk, nxt,
                  pl.BlockSpec((tl, cw), lambda c, i: (i, c + off)),
                  pl.BlockSpec((8, cw), lambda c, i: (jnp.maximum(i * hb - 1, 0), c + off)),
                  pl.BlockSpec((KCONV, cw), lambda c, i: (0, c))],
        out_specs=[blk, pl.BlockSpec((8, cw), lambda c, i: (0, c))],
        out_shape=[jax.ShapeDtypeStruct((l, cdim), BF16), jax.ShapeDtypeStruct((8, cdim), F32)],
        scratch_shapes=[pltpu.VMEM((tl + 8, cw), F32), pltpu.VMEM((tl + 8, cw), F32)],
        compiler_params=_params("parallel", "arbitrary"),
    )(dxbc, dxbc, acc, acc, proj_a, proj_a, conv_w)


def _ssd_common(dt_ref, bias_ref, alog_ref):
    li = lax.broadcasted_iota(jnp.int32, (CHUNK, CHUNK), 0)
    si = lax.broadcasted_iota(jnp.int32, (CHUNK, CHUNK), 1)
    tri = si <= li
    dtv = _softplus(dt_ref[...] + bias_ref[0])
    a_neg = -jnp.exp(alog_ref[0])
    da = dtv * a_neg
    cs_col = sum(_dot(tri.astype(BF16), p, NN) for p in _split3(da))
    cs_row = sum(_dot(p, (li <= si).astype(BF16), TN) for p in _split3(da))
    return tri, dtv, a_neg, cs_col, cs_row


def _dot2(x, t16, dims):
    hi, lo = _split2(x)
    if dims == NN:
        return _dot(jnp.concatenate([hi, lo], axis=1), jnp.concatenate([t16, t16], axis=0), NN)
    return _dot(hi, t16, dims) + _dot(lo, t16, dims)


def _ssd_expand(dtv, cs_col, dsk, pg):
    expm = (lax.shift_right_logical(lax.broadcasted_iota(jnp.int32, (LANES, pg), 1), SSD_HD_SHIFT)
            == lax.broadcasted_iota(jnp.int32, (LANES, pg), 0)).astype(BF16)
    tot = cs_col[CHUNK - 1:CHUNK, :]
    stack = jnp.concatenate([dtv, jnp.exp(cs_col), jnp.exp(tot - cs_col), jnp.broadcast_to(dsk, (CHUNK, LANES)),
                             jnp.broadcast_to(jnp.exp(tot), (CHUNK, LANES))], axis=0)
    ex = _dot2(stack, expm, NN)
    return tuple(ex[CHUNK * a:CHUNK * (a + 1)] for a in range(5))


SSD_GPS = 2


def _ssd_specs(d, r_heads):
    pg = r_heads * SSD_HD
    gw = SSD_GPS * LANES
    nb = d // gw
    dt_blk = (2 * d + 2 * NGROUPS * NSTATE) // gw
    x_spec = lambda cmap: pl.BlockSpec((CHUNK, SSD_GPS * pg), lambda g, c: (cmap(c), g))
    b_spec = lambda cmap: pl.BlockSpec((CHUNK, gw), lambda g, c: (cmap(c), nb + g))
    c_spec = lambda cmap: pl.BlockSpec((CHUNK, gw), lambda g, c: (cmap(c), nb + NGROUPS // SSD_GPS + g))
    dt_spec = lambda cmap: pl.BlockSpec((CHUNK, gw), lambda g, c: (cmap(c), dt_blk + g))
    const = pl.BlockSpec((SSD_GPS, 1, LANES), lambda g, c: (g, 0, 0))
    return pg, x_spec, b_spec, c_spec, dt_spec, const


def _ssd_group_views(refs, kinds, gi, pg):
    out = []
    for r, kind in zip(refs, kinds):
        if kind == "x":
            out.append(r.at[:, pl.ds(gi * pg, pg)])
        elif kind == "g":
            out.append(r.at[:, pl.ds(gi * LANES, LANES)])
        elif kind == "c":
            out.append(r.at[pl.ds(gi, 1)])
        elif kind == "h":
            out.append(r.at[:, pl.ds(gi, 1)])
        else:
            out.append(r.at[gi])
    return out


def _ssd_fwd(xbc, proj_a, bias_g, alog_g, dsk_g, d, name):
    l = xbc.shape[0]
    nc = l // CHUNK
    r_heads = d // SSD_HD // NGROUPS
    pg, x_spec, b_spec, c_spec, dt_spec, const = _ssd_specs(d, r_heads)
    ident = lambda c: c

    def body(*refs):
        for gi in range(SSD_GPS):
            group(*_ssd_group_views(refs, "xgggcccxhs", gi, pg))

    def group(x_ref, b_ref, c_ref, dt_ref, bias_ref, alog_ref, dsk_ref, y_ref, hs_ref, h_scr):
        @pl.when(pl.program_id(1) == 0)
        def _():
            h_scr[...] = jnp.zeros_like(h_scr)

        tri, dtv, _, cs_col, cs_row = _ssd_common(dt_ref, bias_ref, alog_ref)
        hin = h_scr[...]
        hs_ref[0, 0] = hin
        bm = b_ref[...].astype(BF16)
        cm = c_ref[...].astype(BF16)
        xv = x_ref[...]
        e_dt, e_ecs, e_decs, e_dsk, e_etot = _ssd_expand(dtv, cs_col, dsk_ref[0], pg)
        xd = xv * e_dt
        xd16 = xd.astype(BF16)
        gmat = _dot(cm, bm, NT)
        for r in range(r_heads):
            sl = slice(SSD_HD * r, SSD_HD * (r + 1))
            lm = jnp.exp(jnp.where(tri, cs_col[:, r:r + 1] - cs_row[r:r + 1, :], -jnp.inf))
            y_ref[:, sl] = _dot((gmat * lm).astype(BF16), xd16[:, sl], NN)
        y_ref[...] += e_ecs * _dot(cm, hin.astype(BF16), NN) + e_dsk * xv
        h_scr[...] = hin * e_etot[0:1] + _dot(bm, (xd * e_decs).astype(BF16), TN)

    return pl.pallas_call(
        body, name=name, grid=(NGROUPS // SSD_GPS, nc),
        in_specs=[x_spec(ident), b_spec(ident), c_spec(ident), dt_spec(ident), const, const, const],
        out_specs=[x_spec(ident), pl.BlockSpec((1, SSD_GPS, NSTATE, pg), lambda g, c: (c, g, 0, 0))],
        out_shape=[jax.ShapeDtypeStruct((l, d), F32), jax.ShapeDtypeStruct((nc, NGROUPS, NSTATE, pg), F32)],
        scratch_shapes=[pltpu.VMEM((SSD_GPS, NSTATE, pg), F32)], compiler_params=_params("parallel", "arbitrary"),
    )(xbc, xbc, xbc, proj_a, bias_g, alog_g, dsk_g)


def _ssd_bwd(dy, xbc, proj_a, hs, bias_g, alog_g, dsk_g, d, name):
    l = xbc.shape[0]
    nc = l // CHUNK
    r_heads = d // SSD_HD // NGROUPS
    pg, x_spec, b_spec, c_spec, dt_spec, const = _ssd_specs(d, r_heads)
    rev = lambda c: nc - 1 - c
    cdim = d + 2 * NGROUPS * NSTATE

    def body(*refs):
        for gi in range(SSD_GPS):
            group(*_ssd_group_views(refs, "xxggghcccxgggcssss", gi, pg))

    def group(dy_ref, x_ref, b_ref, c_ref, dt_ref, hs_ref, bias_ref, alog_ref, dsk_ref,
              dx_ref, db_ref, dc_ref, ddt_ref, sums_ref, dh_scr, p_scr, pt_scr, dxd_scr):
        first = pl.program_id(1) == 0

        @pl.when(first)
        def _():
            dh_scr[...] = jnp.zeros_like(dh_scr)
            sums_ref[...] = jnp.zeros_like(sums_ref)

        tri, dtv, a_neg, cs_col, cs_row = _ssd_common(dt_ref, bias_ref, alog_ref)
        li = lax.broadcasted_iota(jnp.int32, (CHUNK, CHUNK), 0)
        si = lax.broadcasted_iota(jnp.int32, (CHUNK, CHUNK), 1)
        tri_t = li <= si
        lastrow = lax.broadcasted_iota(jnp.int32, (CHUNK, 1), 0) == CHUNK - 1
        indm = (lax.shift_right_logical(lax.broadcasted_iota(jnp.int32, (pg, LANES), 0), SSD_HD_SHIFT)
                == lax.broadcasted_iota(jnp.int32, (pg, LANES), 1)).astype(BF16)
        hin = hs_ref[0, 0]
        dhout = dh_scr[...]
        hin16 = hin.astype(BF16)
        dhout16 = dhout.astype(BF16)
        bm = b_ref[...].astype(BF16)
        cm = c_ref[...].astype(BF16)
        xv = x_ref[...]
        dyv = dy_ref[...]
        e_dt, e_ecs, e_decs, e_dsk, e_etot = _ssd_expand(dtv, cs_col, dsk_ref[0], pg)
        xd = xv * e_dt
        xd16 = xd.astype(BF16)
        dy16 = dyv.astype(BF16)
        dye = dyv * e_ecs
        dye16 = dye.astype(BF16)
        gmat = _dot(cm, bm, NT)
        gmat_t = _dot(bm, cm, NT)
        dg = jnp.zeros((CHUNK, CHUNK), F32)
        dg_t = jnp.zeros((CHUNK, CHUNK), F32)
        for r in range(r_heads):
            sl = slice(SSD_HD * r, SSD_HD * (r + 1))
            col = cs_col[:, r:r + 1]
            row = cs_row[r:r + 1, :]
            lm = jnp.exp(jnp.where(tri, col - row, -jnp.inf))
            lm_t = jnp.exp(jnp.where(tri_t, row - col, -jnp.inf))
            dm = _dot(dy16[:, sl], xd16[:, sl], NT)
            dm_t = _dot(xd16[:, sl], dy16[:, sl], NT)
            dg = dg + dm * lm
            dg_t = dg_t + dm_t * lm_t
            m_t = gmat_t * lm_t
            p_scr[:, sl] = dm * (gmat * lm)
            pt_scr[:, sl] = dm_t * m_t
            dxd_scr[:, sl] = _dot(m_t.astype(BF16), dy16[:, sl], NN)
        yoff = _dot(cm, hin16, NN)
        qall = _dot(bm, dhout16, NN)
        dxd = dxd_scr[...] + qall * e_decs
        hh = jnp.broadcast_to(jnp.sum(dhout * hin, axis=0, keepdims=True), (8, pg))
        red = _dot2(jnp.concatenate([dye * yoff, qall * xd, dxd * xv, dyv * xv, p_scr[...], pt_scr[...], hh], axis=0), indm, NN)
        r_yoff, r_q, r_dt, r_dsk, r_p, r_pt = (red[CHUNK * a:CHUNK * (a + 1)] for a in range(6))
        tot = cs_col[CHUNK - 1:CHUNK, :]
        ddec = r_q * jnp.exp(tot - cs_col)
        dtot = jnp.sum(ddec, axis=0, keepdims=True) + jnp.exp(tot) * red[6 * CHUNK:6 * CHUNK + 1]
        dcs = r_p - r_pt + r_yoff - ddec + jnp.where(lastrow, dtot, 0.0)
        dda = sum(_dot(tri_t.astype(BF16), p, NN) for p in _split3(dcs))
        draw = (r_dt + dda * a_neg) * _sigmoid(dt_ref[...] + bias_ref[0])
        ddt_ref[...] = draw
        sums_ref[0, 0:1, :] += jnp.sum(draw, axis=0, keepdims=True)
        sums_ref[0, 1:2, :] += jnp.sum(dda * dtv, axis=0, keepdims=True)
        sums_ref[0, 2:3, :] += jnp.sum(r_dsk, axis=0, keepdims=True)
        dx_ref[...] = dxd * e_dt + e_dsk * dyv
        dc_ref[...] = _dot(dg.astype(BF16), bm, NN) + _dot(dye16, hin16, NT)
        db_ref[...] = _dot(dg_t.astype(BF16), cm, NN) + _dot((xd * e_decs).astype(BF16), dhout16, NT)
        dh_scr[...] = dhout * e_etot[0:1] + _dot(cm, dye16, TN)

    grp = pl.BlockSpec((CHUNK, SSD_GPS * LANES), lambda g, c: (rev(c), g))
    return pl.pallas_call(
        body, name=name, grid=(NGROUPS // SSD_GPS, nc),
        in_specs=[x_spec(rev), x_spec(rev), b_spec(rev), c_spec(rev), dt_spec(rev),
                  pl.BlockSpec((1, SSD_GPS, NSTATE, pg), lambda g, c: (rev(c), g, 0, 0)), const, const, const],
        out_specs=[x_spec(rev), grp, grp, grp, pl.BlockSpec((SSD_GPS, 8, LANES), lambda g, c: (g, 0, 0))],
        out_shape=[jax.ShapeDtypeStruct((l, d), F32), jax.ShapeDtypeStruct((l, NGROUPS * NSTATE), F32),
                   jax.ShapeDtypeStruct((l, NGROUPS * NSTATE), F32),
                   jax.ShapeDtypeStruct((l, NGROUPS * LANES), F32), jax.ShapeDtypeStruct((NGROUPS, 8, LANES), F32)],
        scratch_shapes=[pltpu.VMEM((SSD_GPS, NSTATE, pg), F32)] + [pltpu.VMEM((SSD_GPS, CHUNK, pg), F32)] * 3,
        compiler_params=_params("parallel", "arbitrary"),
    )(dy, xbc, xbc, xbc, proj_a, hs, bias_g, alog_g, dsk_g)


SBA_TK = 256
SBA_TQ_FWD = 1024
SBA_TQ_BWD = 1024
SBA_ROW_PARTS = 8


def _sba_tiles(l, tq_target):
    tk = _pick(l, SBA_TK, LANES)
    tq = _pick(l, tq_target, tk)
    assert l // tk <= LANES
    return tq, tk


def _sba_scores(qk, valid, scale):
    z = qk * scale
    nz = -z
    lk = jnp.minimum(nz, 0.0) - jnp.log(1.0 + jnp.exp(jnp.minimum(z, nz)))
    if valid is not None:
        lk = jnp.where(valid, lk, 0.0)
    return z, lk


def _sba_fwd(proj_a, proj_c, mix, d, g_off, name):
    l = proj_c.shape[0]
    nh = d // SBA_HD
    tq, tk = _sba_tiles(l, SBA_TQ_FWD)
    band = tq // tk
    scale = 1.0 / math.sqrt(SBA_HD)
    rq = tq // SBA_ROW_PARTS if tq % (SBA_ROW_PARTS * 16) == 0 else tq
    parts = [pl.ds(p * rq, rq) for p in range(tq // rq)]

    def body(q_ref, k_ref, v_ref, g_ref, mix_ref, o_ref, y_ref, rs_ref, rs_scr, qk_a, qk_b):
        del mix_ref
        i = pl.program_id(1)
        ki = lax.broadcasted_iota(jnp.int32, (tk, tk), 0)
        kj = lax.broadcasted_iota(jnp.int32, (tk, tk), 1)
        uinc = (ki >= kj).astype(BF16)
        lane = lax.broadcasted_iota(jnp.int32, (rq, LANES), 1)
        rs_scr[...] = jnp.zeros_like(rs_scr)
        qs = [q_ref[ps, :] for ps in parts]

        def qk_into(j, qk_scr):
            kb = k_ref[pl.ds(pl.multiple_of(j * tk, tk), tk), :]
            for p, ps in enumerate(parts):
                qk_scr[ps, :] = _dot(qs[p], kb, NT)

        def tile(j, carry, band_pos=None, qk_scr=None):
            start = pl.multiple_of(j * tk, tk)
            vb = v_ref[pl.ds(start, tk), :]
            live, valid = list(range(len(parts))), {}
            if band_pos is not None:
                k0, k1 = band_pos * tk, (band_pos + 1) * tk - 1
                live = [p for p in live if (p + 1) * rq - 1 > k0]
                rows = lax.broadcasted_iota(jnp.int32, (rq, tk), 0)
                cols = lax.broadcasted_iota(jnp.int32, (rq, tk), 1)
                valid = {p: cols - rows < p * rq - k0 for p in live if p * rq <= k1}
            if qk_scr is None:
                kb = k_ref[pl.ds(start, tk), :]
                qk = {p: _dot(qs[p], kb, NT) for p in live}
            else:
                qk = {p: qk_scr[parts[p], :] for p in live}
            sc = {p: _sba_scores(qk[p], valid.get(p), scale) for p in live}
            tail = {p: _dot2(sc[p][1], uinc, NN) for p in live}
            out = list(carry)
            for p in live:
                rsum, acc = carry[p]
                w = jnp.exp(sc[p][0] + tail[p] + rsum)
                if p in valid:
                    w = jnp.where(valid[p], w, 0.0)
                acc = acc + _dot(w.astype(BF16), vb, NN)
                rs_scr[parts[p], :] = jnp.where(lane == j, rsum, rs_scr[parts[p], :])
                out[p] = (rsum + tail[p][:, 0:1], acc)
            return tuple(out)

        carry = tuple((jnp.zeros((rq, 1), F32), jnp.zeros((rq, SBA_HD), F32)) for _ in parts)
        nfull = i * band
        for band_pos in reversed(range(band)):
            carry = tile(nfull + band_pos, carry, band_pos)
        if band % 2 == 0:
            @pl.when(nfull > 0)
            def _():
                qk_into(nfull - 1, qk_a)

            def pair(m, c):
                ja = nfull - 1 - 2 * m
                qk_into(ja - 1, qk_b)
                c = tile(ja, c, None, qk_a)
                qk_into(jnp.maximum(ja - 2, 0), qk_a)
                return tile(ja - 1, c, None, qk_b)

            carry = lax.fori_loop(0, nfull // 2, pair, carry)
        else:
            carry = lax.fori_loop(0, nfull, lambda jj, c: tile(nfull - 1 - jj, c), carry)
        rs_ref[...] = rs_scr[...]
        for p, ps in enumerate(parts):
            acc = carry[p][1]
            o_ref[ps, :] = acc
            gv = g_ref[ps, :]
            y_ref[ps, :] = (acc * (gv * _sigmoid(gv))).astype(BF16)

    goff = g_off // SBA_HD
    blk = lambda off: pl.BlockSpec((tq, SBA_HD), lambda h, i: (i, h + off))
    full = lambda off: pl.BlockSpec((l, SBA_HD), lambda h, i: (0, h + off))
    out = pl.BlockSpec((tq, SBA_HD), lambda h, i: (i, h))
    return pl.pallas_call(
        body, name=name, grid=(nh, l // tq), in_specs=[blk(0), full(nh), full(2 * nh), blk(goff), HBM],
        out_specs=[out, blk(nh), out],
        scratch_shapes=[pltpu.VMEM((tq, LANES), F32), pltpu.VMEM((tq, tk), F32), pltpu.VMEM((tq, tk), F32)],
        out_shape=[jax.ShapeDtypeStruct((l, d), F32), jax.ShapeDtypeStruct((l, 2 * d), BF16), jax.ShapeDtypeStruct((l, d), F32)],
        input_output_aliases={4: 1}, compiler_params=_params("parallel", "arbitrary"),
    )(proj_c, proj_c, proj_c, proj_a, mix)


def _sba_bwd(dys, o, rs, proj_a, proj_c, d, g_off, name):
    l = proj_c.shape[0]
    nh = d // SBA_HD
    tq, tk = _sba_tiles(l, SBA_TQ_BWD)
    band = tq // tk
    scale = 1.0 / math.sqrt(SBA_HD)

    def body(dy_ref, o_ref, rs_ref, q_ref, k_ref, v_ref, g_ref, dq_ref, dk_ref, dv_ref, dg_ref, dk_acc, dv_acc,
             qk_a, qk_b, dw_a, dw_b):
        i = pl.program_id(1)
        nq = pl.num_programs(1)

        @pl.when(i == 0)
        def _():
            dk_acc[...] = jnp.zeros_like(dk_acc)
            dv_acc[...] = jnp.zeros_like(dv_acc)

        ki = lax.broadcasted_iota(jnp.int32, (tk, tk), 0)
        kj = lax.broadcasted_iota(jnp.int32, (tk, tk), 1)
        uinc = (ki >= kj).astype(BF16)
        ulow = (ki < kj).astype(BF16)
        lane = lax.broadcasted_iota(jnp.int32, (tq, LANES), 1)
        q = q_ref[...]
        gv = g_ref[...]
        sg = _sigmoid(gv)
        dyv = dy_ref[...]
        dg_ref[...] = (dyv * o_ref[...] * (sg * (1.0 + gv * (1.0 - sg)))).astype(BF16)
        do = dyv * (gv * sg)
        do16 = do.astype(BF16)
        q_t = q.astype(F32).T.astype(BF16)
        do_t = do.T.astype(BF16)

        def ahead(j, qk_scr, dw_scr):
            start = pl.multiple_of(j * tk, tk)
            qk_scr[...] = _dot(q, k_ref[pl.ds(start, tk), :], NT)
            dw_scr[...] = _dot(do16, v_ref[pl.ds(start, tk), :], NT)

        def tile(j, carry, band_pos=None, qk_scr=None, dw_scr=None):
            r0 = 0 if band_pos is None else band_pos * tk
            epre, dq = carry[0][r0:], carry[1][r0:]
            qr, dor = q[r0:], do16[r0:]
            start = pl.multiple_of(j * tk, tk)
            kb = k_ref[pl.ds(start, tk), :]
            rsum = jnp.sum(jnp.where(lane[r0:] == j, rs_ref[r0:, :], 0.0), axis=1, keepdims=True)
            valid = None
            if band_pos is not None:
                valid = (lax.broadcasted_iota(jnp.int32, (tq - r0, tk), 1) < lax.broadcasted_iota(jnp.int32, (tq - r0, tk), 0))
            if qk_scr is None:
                qk = _dot(qr, kb, NT)
                dw = _dot(dor, v_ref[pl.ds(start, tk), :], NT)
            else:
                qk = qk_scr[...]
                dw = dw_scr[...]
            z, lk = _sba_scores(qk, valid, scale)
            w = jnp.exp(z + _dot2(lk, uinc, NN) + rsum)
            if valid is not None:
                w = jnp.where(valid, w, 0.0)
            e = w * dw
            epx = _dot2(e, ulow, NN)
            dz = (e - jnp.exp(z + lk) * (e + epre + epx)) * scale
            if valid is not None:
                dz = jnp.where(valid, dz, 0.0)
            dz16 = dz.astype(BF16)
            dq = dq + _dot(dz16, kb, NN)
            dk_acc[j] += _dot(q_t[:, r0:], dz16, NN)
            dv_acc[j] += _dot(do_t[:, r0:], w.astype(BF16), NN)
            epre = epre + epx[:, tk - 1:tk] + e[:, tk - 1:tk]
            if r0:
                epre = jnp.concatenate([carry[0][:r0], epre], axis=0)
                dq = jnp.concatenate([carry[1][:r0], dq], axis=0)
            return epre, dq

        carry = (jnp.zeros((tq, 1), F32), jnp.zeros((tq, SBA_HD), F32))
        nfull = i * band
        if band % 2 == 0:
            @pl.when(nfull > 0)
            def _():
                ahead(0, qk_a, dw_a)

            def pair(m, c):
                ja = 2 * m
                ahead(ja + 1, qk_b, dw_b)
                c = tile(ja, c, None, qk_a, dw_a)
                ahead(jnp.minimum(ja + 2, nfull - 1), qk_a, dw_a)
                return tile(ja + 1, c, None, qk_b, dw_b)

            carry = lax.fori_loop(0, nfull // 2, pair, carry)
        else:
            carry = lax.fori_loop(0, nfull, lambda j, c: tile(j, c), carry)
        for band_pos in range(band):
            carry = tile(nfull + band_pos, carry, band_pos)
        dq_ref[...] = carry[1].astype(BF16)

        @pl.when(i == nq - 1)
        def _():
            for jj in range(l // tk):
                dk_ref[jj * tk:(jj + 1) * tk, :] = dk_acc[jj].T.astype(BF16)
                dv_ref[jj * tk:(jj + 1) * tk, :] = dv_acc[jj].T.astype(BF16)

    goff = g_off // SBA_HD
    blk = lambda off: pl.BlockSpec((tq, SBA_HD), lambda h, i: (i, h + off))
    full = lambda off: pl.BlockSpec((l, SBA_HD), lambda h, i: (0, h + off))
    out = pl.BlockSpec((tq, SBA_HD), lambda h, i: (i, h))
    outfull = pl.BlockSpec((l, SBA_HD), lambda h, i: (0, h))
    sd = jax.ShapeDtypeStruct((l, d), BF16)
    return pl.pallas_call(
        body, name=name, grid=(nh, l // tq),
        in_specs=[blk(nh), out, out, blk(0), full(nh), full(2 * nh), blk(goff)],
        out_specs=[out, outfull, outfull, out], out_shape=[sd, sd, sd, sd],
        scratch_shapes=[pltpu.VMEM((l // tk, SBA_HD, tk), F32)] * 2 + [pltpu.VMEM((tq, tk), F32)] * 4,
        compiler_params=_params("parallel", "arbitrary"),
    )(dys, o, rs, proj_c, proj_c, proj_c, proj_a)


def _adamw_math(w, g, m, v):
    m = ADAM_B1 * m + (1.0 - ADAM_B1) * g
    v = ADAM_B2 * v + (1.0 - ADAM_B2) * (g * g)
    m_hat = m / (1.0 - ADAM_B1 ** ADAM_STEP)
    v_hat = v / (1.0 - ADAM_B2 ** ADAM_STEP)
    delta = -ADAM_LR * (m_hat / (jnp.sqrt(v_hat) + ADAM_EPS) + ADAM_WD * w)
    return delta, m, v


def _adamw(w, g, m, v, name):
    a, r, c = w.shape
    tc = _pick(c, 512, LANES)
    tr = _pick(r, max(8, (5 << 18) // tc // 8 * 8), 8)

    def body(w_ref, g_ref, m_ref, v_ref, d_ref, nm_ref, nv_ref):
        dl, nm, nv = _adamw_math(w_ref[...], g_ref[...], m_ref[...], v_ref[...])
        d_ref[...] = dl
        nm_ref[...] = nm
        nv_ref[...] = nv

    blk = pl.BlockSpec((1, tr, tc), lambda i, j, k: (i, j, k))
    sd = jax.ShapeDtypeStruct(w.shape, F32)
    return pl.pallas_call(
        body, name=name, grid=(a, r // tr, c // tc), in_specs=[blk] * 4, out_specs=[blk] * 3, out_shape=[sd] * 3,
        compiler_params=_params("parallel", "parallel", "parallel"),
    )(w, g, m, v)


def _dims(d):
    cdim = d + 2 * NGROUPS * NSTATE
    heads = d // SSD_HD
    r_heads = heads // NGROUPS
    g_off = d + cdim + NGROUPS * LANES
    na = g_off + d
    nc = 3 * d
    return cdim, heads, r_heads, na, nc, g_off


def _pack_w_in(w_in, d):
    cdim, heads, r_heads, na, nc, g_off = _dims(d)
    o = d + cdim
    w_dt = w_in[:, o:o + heads].reshape(d, NGROUPS, r_heads)
    w_dt = jnp.pad(w_dt, ((0, 0), (0, 0), (0, LANES - r_heads))).reshape(d, NGROUPS * LANES)
    return jnp.concatenate([w_in[:, :o], w_dt, w_in[:, o + heads + nc:]], axis=1), w_in[:, o + heads:o + heads + nc]


def _unpack_w_in(ga, gc, d):
    cdim, heads, r_heads, na, nc, g_off = _dims(d)
    o = d + cdim
    g_dt = ga[:, o:g_off].reshape(d, NGROUPS, LANES)[:, :, :r_heads].reshape(d, heads)
    return jnp.concatenate([ga[:, :o], g_dt, gc, ga[:, g_off:]], axis=1)


def _group_vec(v, r_heads):
    return jnp.pad(v.reshape(NGROUPS, 1, r_heads), ((0, 0), (0, 0), (0, LANES - r_heads)))


def _layer_fwd(x, p, d, tag):
    cdim, heads, r_heads, na, nc, g_off = _dims(d)
    h = _rmsnorm_fwd(x, p["norm_w"], f"norm_f{tag}")
    proj_a = _matmul(h, p["wa"], "nn", F32, f"inproj_a{tag}")
    proj_c = _matmul(h, p["wc"], "nn", BF16, f"inproj_c{tag}")
    acc, xbc = _conv_fwd(proj_a, p["conv_w"], p["conv_b"], d, f"conv_f{tag}")
    y, hs = _ssd_fwd(xbc, proj_a, p["bias_g"], p["alog_g"], p["dsk_g"], d, f"ssd_f{tag}")
    mix = _gatenorm_fwd(y, proj_a, p["ssd_norm_w"], f"gate_f{tag}")
    o, mix, rs = _sba_fwd(proj_a, proj_c, mix, d, g_off, f"sba_f{tag}")
    x_next = _matmul(mix, p["w_out"], "nn", F32, f"outproj{tag}", add=x)
    return x_next, dict(x=x, h=h, proj_a=proj_a, proj_c=proj_c, acc=acc, xbc=xbc, y=y, hs=hs, o=o, rs=rs, mix=mix)


def _layer_bwd(dxn, s, p, d, tag):
    cdim, heads, r_heads, na, nc, g_off = _dims(d)
    dxn16 = dxn.astype(BF16)
    dmix = _matmul(dxn16, p["w_out"], "nt", F32, f"dmix{tag}")
    g_w_out = _matmul(s["mix"], dxn16, "tn", F32, f"dwout{tag}")
    dq, dk, dv, dg = _sba_bwd(dmix, s["o"], s["rs"], s["proj_a"], s["proj_c"], d, g_off, f"sba_b{tag}")
    dy, dz, g_ssd_norm = _gatenorm_bwd(dmix, s["y"], s["proj_a"], p["ssd_norm_w"], f"gate_b{tag}")
    dx_s, db_s, dc_s, ddt, sums = _ssd_bwd(dy, s["xbc"], s["proj_a"], s["hs"], p["bias_g"], p["alog_g"], p["dsk_g"], d, f"ssd_b{tag}")
    dxbc = jnp.concatenate([dx_s, db_s, dc_s], axis=1)
    du, g_conv = _conv_bwd(dxbc, s["acc"], s["proj_a"], p["conv_w"], d, f"conv_b{tag}")
    dproj_a = jnp.concatenate([dz, du, ddt.astype(BF16), dg], axis=1)
    dproj_c = jnp.concatenate([dq, dk, dv], axis=1)
    g_wa = _matmul(s["h"], dproj_a, "tn", F32, f"dwin_a{tag}")
    g_wc = _matmul(s["h"], dproj_c, "tn", F32, f"dwin_c{tag}")
    dh = _matmul(dproj_a, p["wa"], "nt", F32, f"dh_a{tag}")
    dh = _matmul(dproj_c, p["wc"], "nt", F32, f"dh_c{tag}", add=dh)
    dx, g_norm = _rmsnorm_bwd(dh, s["x"], p["norm_w"], dxn, f"norm_b{tag}")
    a_neg = -jnp.exp(p["alog_g"][:, 0, :r_heads].reshape(heads))
    grads = dict(
        norm_w=g_norm[0], w_in=_unpack_w_in(g_wa, g_wc, d), conv_w=g_conv[:KCONV], conv_b=g_conv[KCONV],
        dt_bias=sums[:, 0, :r_heads].reshape(heads), a_log=sums[:, 1, :r_heads].reshape(heads) * a_neg,
        d_skip=sums[:, 2, :r_heads].reshape(heads), ssd_norm_w=g_ssd_norm[0], w_out=g_w_out)
    return dx, grads


def _local_step(x, target, w_in16, w_out16, conv_w, small):
    l, d = x.shape
    depth = w_in16.shape[0]
    r_heads = _dims(d)[2]
    layers = []
    for i in range(depth):
        wa, wc = _pack_w_in(w_in16[i], d)
        layers.append(dict(
            norm_w=small["norm_w"][i][None], wa=wa, wc=wc, conv_w=conv_w[i], conv_b=small["conv_b"][i][None],
            bias_g=_group_vec(small["dt_bias"][i], r_heads), alog_g=_group_vec(small["a_log"][i], r_heads),
            dsk_g=_group_vec(small["d_skip"][i], r_heads), ssd_norm_w=small["ssd_norm_w"][i][None], w_out=w_out16[i]))
    saved = []
    hcur = x
    for i in range(depth):
        hcur, s = _layer_fwd(hcur, layers[i], d, str(i))
        saved.append(s)
    dh, g_final, loss = _final_loss(hcur, small["final_norm_w"][None], target, "final_loss")
    grads = [None] * depth
    for i in reversed(range(depth)):
        dh, grads[i] = _layer_bwd(dh, saved[i], layers[i], d, str(i))
    stacked = {k: jnp.stack([g[k] for g in grads]) for k in grads[0]}
    stacked["final_norm_w"] = g_final[0]
    return loss[0, 0], dh, stacked


HBM = pl.BlockSpec(memory_space=pl.ANY)
NCHIP = 4
NDEV = 8


def _mesh_pos():
    x, y, c = lax.axis_index("x"), lax.axis_index("y"), lax.axis_index("c")
    chips = [(1 - x, y), (x, 1 - y), (1 - x, 1 - y)]
    return x, y, c, chips


def _remote(src, dst, send_sem, recv_sem, dev):
    return pltpu.make_async_remote_copy(src_ref=src, dst_ref=dst, send_sem=send_sem, recv_sem=recv_sem,
                                        device_id=dev, device_id_type=MESH)


def _gather_weights(shards):
    n = len(shards)
    hl = shards[0].shape[0] // 2

    def body(*refs):
        ins, outs = refs[:n], refs[n:2 * n]
        send, recv = refs[2 * n:]
        x, y, c, chips = _mesh_pos()
        k = 2 * x + y
        half = pl.ds(c * hl, hl)
        other = pl.ds((1 - c) * hl, hl)
        for a in range(n):
            for j, (px, py) in enumerate(chips):
                _remote(ins[a].at[half], outs[a].at[k, half], send.at[a, j], recv.at[a, j], (px, py, c)).start()
        for a in range(n):
            for j, (px, py) in enumerate(chips):
                kj = 2 * px + py
                got = outs[a].at[kj, half]
                _remote(got, got, send.at[a, j], recv.at[a, j], (px, py, c)).wait_recv()
                _remote(got, got, send.at[a, 3 + j], recv.at[a, 3 + j], (x, y, 1 - c)).start()
        for a in range(n):
            for j, (px, py) in enumerate(chips):
                kj = 2 * px + py
                _remote(outs[a].at[kj, other], outs[a].at[kj, other], send.at[a, 3 + j], recv.at[a, 3 + j], (x, y, 1 - c)).wait_recv()
            for j, (px, py) in enumerate(chips):
                kj = 2 * px + py
                _remote(ins[a].at[half], outs[a].at[k, half], send.at[a, j], recv.at[a, j], (px, py, c)).wait_send()
                _remote(outs[a].at[kj, half], outs[a].at[kj, half], send.at[a, 3 + j], recv.at[a, 3 + j], (x, y, 1 - c)).wait_send()

    return pl.pallas_call(
        body, name="gather_weights", in_specs=[HBM] * n, out_specs=[HBM] * n,
        out_shape=[jax.ShapeDtypeStruct((NCHIP,) + s.shape, s.dtype) for s in shards],
        scratch_shapes=[pltpu.SemaphoreType.DMA((n, 6)), pltpu.SemaphoreType.DMA((n, 6))],
    )(*shards)


def _swap_halves(parts):
    n = len(parts)
    hl = parts[0].shape[1] // 2

    def body(*refs):
        ins, outs = refs[:n], refs[n:2 * n]
        send, recv = refs[2 * n:]
        x, y, c, _ = _mesh_pos()
        cps = [_remote(ins[a].at[:, pl.ds((1 - c) * hl, hl)], outs[a], send.at[a], recv.at[a], (x, y, 1 - c)) for a in range(n)]
        for cp in cps:
            cp.start()
        for cp in cps:
            cp.wait()

    return pl.pallas_call(
        body, name="grad_swap_halves", in_specs=[HBM] * n, out_specs=[HBM] * n,
        out_shape=[jax.ShapeDtypeStruct((NCHIP, hl) + p.shape[2:], p.dtype) for p in parts],
        scratch_shapes=[pltpu.SemaphoreType.DMA((n,)), pltpu.SemaphoreType.DMA((n,))],
    )(*parts)


def _exchange_shards(parts):
    n = len(parts)

    def body(*refs):
        ins, outs = refs[:n], refs[n:2 * n]
        send, recv = refs[2 * n:]
        x, y, c, chips = _mesh_pos()
        k = 2 * x + y
        work = []
        for a in range(n):
            for j, (px, py) in enumerate(chips):
                cp = _remote(ins[a].at[2 * px + py], outs[a].at[k], send.at[a, j], recv.at[a, j], (px, py, c))
                cp.start()
                work.append(cp)
        for w in work:
            w.wait()

    return pl.pallas_call(
        body, name="grad_exchange", in_specs=[HBM] * n, out_specs=[HBM] * n,
        out_shape=[jax.ShapeDtypeStruct(p.shape, p.dtype) for p in parts],
        scratch_shapes=[pltpu.SemaphoreType.DMA((n, 3)), pltpu.SemaphoreType.DMA((n, 3))],
    )(*parts)


def _swap_reduced(halves):
    n = len(halves)

    def body(*refs):
        ins, outs = refs[:n], refs[n:2 * n]
        send, recv = refs[2 * n:]
        x, y, c, _ = _mesh_pos()
        work = [_remote(ins[a], outs[a], send.at[a], recv.at[a], (x, y, 1 - c)) for a in range(n)]
        for w in work:
            w.start()
        for w in work:
            w.wait()

    return pl.pallas_call(
        body, name="grad_swap_reduced", in_specs=[HBM] * n, out_specs=[HBM] * n,
        out_shape=[jax.ShapeDtypeStruct(h.shape, h.dtype) for h in halves],
        scratch_shapes=[pltpu.SemaphoreType.DMA((n,)), pltpu.SemaphoreType.DMA((n,))],
    )(*halves)


def _all_sum_small(vec, name):
    r = vec.shape[0]

    def body(v_ref, o_ref, buf, send, recv):
        x, y, c, _ = _mesh_pos()
        me = 4 * x + 2 * y + c
        buf[me] = v_ref[...]
        cps = []
        for mask in range(1, NDEV):
            fx, fy, fc = (mask >> 2) & 1, (mask >> 1) & 1, mask & 1
            peer = (1 - x if fx else x, 1 - y if fy else y, 1 - c if fc else c)
            cp = _remote(v_ref, buf.at[me], send.at[mask - 1], recv.at[mask - 1], peer)
            cp.start()
            cps.append(cp)
        for cp in cps:
            cp.wait()
        total = buf[0]
        for dev in range(1, NDEV):
            total = total + buf[dev]
        o_ref[...] = total

    vm = pl.BlockSpec(memory_space=pltpu.VMEM)
    return pl.pallas_call(
        body, name=name, in_specs=[vm], out_specs=vm, out_shape=jax.ShapeDtypeStruct((r, LANES), F32),
        scratch_shapes=[pltpu.VMEM((NDEV, r, LANES), F32), pltpu.SemaphoreType.DMA((NDEV - 1,)), pltpu.SemaphoreType.DMA((NDEV - 1,))],
    )(vec)


def _add_pairs(a, b, out_dtype, name):
    n0, n1, r, c = a.shape
    tr = _pick(r, max(8, (1 << 19) // c // 8 * 8), 8)

    def body(a_ref, b_ref, o_ref):
        o_ref[...] = (a_ref[...] + b_ref[...]).astype(out_dtype)

    blk = pl.BlockSpec((1, 1, tr, c), lambda i, j, t: (i, j, t, 0))
    return pl.pallas_call(
        body, name=name, grid=(n0, n1, r // tr), in_specs=[blk, blk], out_specs=blk,
        out_shape=jax.ShapeDtypeStruct(a.shape, out_dtype), compiler_params=_params("parallel", "parallel", "parallel"),
    )(a, b)


def _sum_chips(p, name):
    _, hl, r, c = p.shape
    tr = _pick(r, max(16, (1 << 19) // c // 16 * 16), 16)

    def body(p_ref, o_ref):
        total = p_ref[0].astype(F32)
        for j in range(1, NCHIP):
            total = total + p_ref[j].astype(F32)
        o_ref[...] = total

    return pl.pallas_call(
        body, name=name, grid=(hl, r // tr),
        in_specs=[pl.BlockSpec((NCHIP, 1, tr, c), lambda i, t: (0, i, t, 0))],
        out_specs=pl.BlockSpec((1, tr, c), lambda i, t: (i, t, 0)),
        out_shape=jax.ShapeDtypeStruct((hl, r, c), F32), compiler_params=_params("parallel", "parallel"),
    )(p)


def _reduce_scatter(parts):
    c = lax.axis_index("c")
    k = 2 * lax.axis_index("x") + lax.axis_index("y")
    hl = parts[0].shape[1] // 2
    theirs = _swap_halves(parts)
    mine = [lax.dynamic_slice_in_dim(p, c * hl, hl, axis=1) for p in parts]
    chip_sum = [_add_pairs(m, t, BF16, f"grad_pair_sum{i}") for i, (m, t) in enumerate(zip(mine, theirs))]
    gathered = _exchange_shards(chip_sum)
    gathered = [lax.dynamic_update_slice_in_dim(g, lax.dynamic_slice_in_dim(s, k, 1, axis=0), k, axis=0)
                for g, s in zip(gathered, chip_sum)]
    halves = [_sum_chips(g, f"grad_chip_sum{i}") for i, g in enumerate(gathered)]
    sibling = _swap_reduced(halves)
    south = c == 0
    return [jnp.concatenate([jnp.where(south, h, s), jnp.where(south, s, h)], axis=0) for h, s in zip(halves, sibling)]


SMALL = ("norm_w", "conv_w", "conv_b", "dt_bias", "a_log", "d_skip", "ssd_norm_w", "final_norm_w")


def _pack(arrays):
    flat = jnp.concatenate([a.reshape(-1).astype(F32) for a in arrays])
    rows = -(-flat.shape[0] // (8 * LANES)) * 8
    return jnp.pad(flat, (0, rows * LANES - flat.shape[0])).reshape(rows, LANES)


def _unpack(vec, shapes):
    flat = vec.reshape(-1)
    out, pos = [], 0
    for s in shapes:
        n = math.prod(s)
        out.append(flat[pos:pos + n].reshape(s))
        pos += n
    return out


def kernel(x, norm_w, w_in, conv_w, conv_b, dt_bias, a_log, d_skip, ssd_norm_w, w_out, final_norm_w, loss_target, m_norm_w, m_w_in, m_conv_w, m_conv_b, m_dt_bias, m_a_log, m_d_skip, m_ssd_norm_w, m_w_out, m_final_norm_w, v_norm_w, v_w_in, v_conv_w, v_conv_b, v_dt_bias, v_a_log, v_d_skip, v_ssd_norm_w, v_w_out, v_final_norm_w):
    depth, d, ics = w_in.shape
    cs = conv_w.shape[2]
    xi, yi, ci = lax.axis_index("x"), lax.axis_index("y"), lax.axis_index("c")
    k = 2 * xi + yi

    placed = lax.dynamic_update_slice(jnp.zeros((depth, KCONV, NCHIP, cs), F32), conv_w[:, :, None, :], (0, 0, k, 0))
    placed = jnp.where(ci == 0, placed, 0.0)
    conv_full = _unpack(_all_sum_small(_pack([placed]), "gather_conv_w"), [(depth, KCONV, NCHIP * cs)])[0]

    w_in16, w_out16 = w_in.astype(BF16), w_out.astype(BF16)
    wi_all, wo_all = _gather_weights([w_in16, w_out16])
    wi_all = lax.dynamic_update_slice_in_dim(wi_all, w_in16[None], k, axis=0)
    wo_all = lax.dynamic_update_slice_in_dim(wo_all, w_out16[None], k, axis=0)
    w_in_full = wi_all.transpose(1, 2, 0, 3).reshape(depth, d, NCHIP * ics)
    w_out_full = wo_all.transpose(1, 0, 2, 3).reshape(depth, 2 * d, d)

    small = dict(norm_w=norm_w, conv_b=conv_b, dt_bias=dt_bias, a_log=a_log, d_skip=d_skip, ssd_norm_w=ssd_norm_w, final_norm_w=final_norm_w)
    loss_local, gx, g = _local_step(x[0], loss_target[0], w_in_full, w_out_full, conv_full, small)

    g_in = g["w_in"].reshape(depth, d, NCHIP, ics).transpose(2, 0, 1, 3)
    g_out = g["w_out"].reshape(depth, NCHIP, 2 * d // NCHIP, d).transpose(1, 0, 2, 3)
    grad_w_in, grad_w_out = _reduce_scatter([g_in, g_out])

    names = list(SMALL)
    total = _all_sum_small(_pack([g[n] for n in names] + [loss_local]), "sum_small_grads")
    parts = _unpack(total, [g[n].shape for n in names] + [()])
    grads = dict(zip(names, parts[:-1]))
    loss = parts[-1]
    grads["conv_w"] = lax.dynamic_index_in_dim(grads["conv_w"].reshape(depth, KCONV, NCHIP, cs), k, axis=2, keepdims=False)
    grads["w_in"], grads["w_out"] = grad_w_in, grad_w_out

    w = dict(norm_w=norm_w, w_in=w_in, conv_w=conv_w, conv_b=conv_b, dt_bias=dt_bias, a_log=a_log, d_skip=d_skip,
             ssd_norm_w=ssd_norm_w, w_out=w_out, final_norm_w=final_norm_w)
    m = dict(norm_w=m_norm_w, w_in=m_w_in, conv_w=m_conv_w, conv_b=m_conv_b, dt_bias=m_dt_bias, a_log=m_a_log, d_skip=m_d_skip,
             ssd_norm_w=m_ssd_norm_w, w_out=m_w_out, final_norm_w=m_final_norm_w)
    v = dict(norm_w=v_norm_w, w_in=v_w_in, conv_w=v_conv_w, conv_b=v_conv_b, dt_bias=v_dt_bias, a_log=v_a_log, d_skip=v_d_skip,
             ssd_norm_w=v_ssd_norm_w, w_out=v_w_out, final_norm_w=v_final_norm_w)
    delta, new_m, new_v = {}, {}, {}
    swap = lambda t: jnp.swapaxes(t, 1, 2)
    delta["w_in"], new_m["w_in"], new_v["w_in"] = (
        swap(t) for t in _adamw(swap(w_in), swap(grads["w_in"]), swap(m_w_in), swap(v_w_in), "adamw_w_in"))
    delta["w_out"], new_m["w_out"], new_v["w_out"] = _adamw(w_out, grads["w_out"], m_w_out, v_w_out, "adamw_w_out")
    shapes = [w[n].shape for n in names]
    packed = [_pack([t[n] for n in names])[None] for t in (w, grads, m, v)]
    for res, out in zip(_adamw(*packed, "adamw_small"), (delta, new_m, new_v)):
        out.update(zip(names, _unpack(res[0], shapes)))

    order = ("norm_w", "w_in", "conv_w", "conv_b", "dt_bias", "a_log", "d_skip", "ssd_norm_w", "w_out", "final_norm_w")
    return (loss, gx[None], *[grads[n] for n in order], *[delta[n] for n in order], *[new_m[n] for n in order], *[new_v[n] for n in order])
```

```python
import math

import jax
import jax.numpy as jnp
from jax import lax
from jax.experimental import pallas as pl
from jax.experimental.pallas import tpu as pltpu

F32, BF16 = jnp.float32, jnp.bfloat16
EPS = 1e-6
CHUNK = 64
NGROUPS = 4
NSTATE = 128
KCONV = 4
SSD_HD = 64
SSD_HD_SHIFT = SSD_HD.bit_length() - 1
SBA_HD = 128
LANES = 128
VMEM_LIMIT = 56 * 1024 * 1024

ADAM_LR, ADAM_B1, ADAM_B2, ADAM_EPS, ADAM_WD, ADAM_STEP = 0.001, 0.9, 0.999, 1e-08, 0.01, 10

NN = ((1,), (0,))
NT = ((1,), (1,))
TN = ((0,), (0,))
MESH = pl.DeviceIdType.MESH


def _dot(a, b, dims):
    return lax.dot_general(a, b, (dims, ((), ())), preferred_element_type=F32)


def _params(*sem):
    return pltpu.CompilerParams(dimension_semantics=sem, vmem_limit_bytes=VMEM_LIMIT)


def _pick(n, target, mult):
    best = None
    for d in range(mult, min(n, target) + 1, mult):
        if n % d == 0:
            best = d
    return n if best is None else best


def _split3(x):
    x1 = x.astype(BF16)
    r1 = x - x1.astype(F32)
    x2 = r1.astype(BF16)
    x3 = (r1 - x2.astype(F32)).astype(BF16)
    return x1, x2, x3


def _split2(x):
    x1 = x.astype(BF16)
    return x1, (x - x1.astype(F32)).astype(BF16)


def _sigmoid(x):
    return 1.0 / (1.0 + jnp.exp(-x))


def _softplus(x):
    return jnp.maximum(x, 0.0) + jnp.log1p(jnp.exp(-jnp.abs(x)))


def _matmul(a, b, mode, out_dtype, name, add=None, tm=1024, tn=1024, tk=2048):
    if mode == "nn":
        (m, k), n = a.shape, b.shape[1]
    elif mode == "nt":
        (m, k), n = a.shape, b.shape[0]
    else:
        (k, m), n = a.shape, b.shape[1]
    tm, tn, tk = _pick(m, tm, LANES), _pick(n, tn, LANES), _pick(k, tk, LANES)
    nk = k // tk
    dims = {"nn": NN, "nt": NT, "tn": TN}[mode]
    a_spec = pl.BlockSpec((tk, tm), lambda i, j, kk: (kk, i)) if mode == "tn" else pl.BlockSpec((tm, tk), lambda i, j, kk: (i, kk))
    b_spec = pl.BlockSpec((tn, tk), lambda i, j, kk: (j, kk)) if mode == "nt" else pl.BlockSpec((tk, tn), lambda i, j, kk: (kk, j))
    o_spec = pl.BlockSpec((tm, tn), lambda i, j, kk: (i, j))
    has_add = add is not None

    def body(*refs):
        a_ref, b_ref = refs[0], refs[1]
        add_ref = refs[2] if has_add else None
        o_ref, acc_ref = refs[-2], refs[-1]
        kk = pl.program_id(2)
        part = _dot(a_ref[...], b_ref[...], dims)

        def finish(total):
            if has_add:
                total = total + add_ref[...].astype(F32)
            o_ref[...] = total.astype(out_dtype)

        if nk == 1:
            finish(part)
        else:
            @pl.when(kk == 0)
            def _():
                acc_ref[...] = part

            @pl.when(jnp.logical_and(kk > 0, kk < nk - 1))
            def _():
                acc_ref[...] += part

            @pl.when(kk == nk - 1)
            def _():
                finish(acc_ref[...] + part)

    in_specs = [a_spec, b_spec] + ([o_spec] if has_add else [])
    args = (a, b) + ((add,) if has_add else ())
    return pl.pallas_call(
        body, name=name, grid=(m // tm, n // tn, nk), in_specs=in_specs, out_specs=o_spec,
        out_shape=jax.ShapeDtypeStruct((m, n), out_dtype),
        scratch_shapes=[pltpu.VMEM((tm, tn), F32)],
        compiler_params=_params("parallel", "parallel", "arbitrary"),
    )(*args)


def _rmsnorm_fwd(x, w, name):
    l, d = x.shape
    tr = _pick(l, 512, 8)

    def body(x_ref, w_ref, h_ref):
        xv = x_ref[...]
        r = lax.rsqrt(jnp.mean(xv * xv, axis=-1, keepdims=True) + EPS)
        h_ref[...] = (xv * r * w_ref[...]).astype(BF16)

    return pl.pallas_call(
        body, name=name, grid=(l // tr,),
        in_specs=[pl.BlockSpec((tr, d), lambda i: (i, 0)), pl.BlockSpec((1, d), lambda i: (0, 0))],
        out_specs=pl.BlockSpec((tr, d), lambda i: (i, 0)),
        out_shape=jax.ShapeDtypeStruct((l, d), BF16), compiler_params=_params("parallel"),
    )(x, w)


def _rmsnorm_bwd(dh, x, w, dres, name):
    l, d = x.shape
    tr = _pick(l, 256, 8)

    def body(dh_ref, x_ref, w_ref, dres_ref, dx_ref, dw_ref):
        xv = x_ref[...]
        r = lax.rsqrt(jnp.mean(xv * xv, axis=-1, keepdims=True) + EPS)
        xh = xv * r
        dhv = dh_ref[...]
        dxh = dhv * w_ref[...]
        dx_ref[...] = dres_ref[...] + r * (dxh - xh * jnp.mean(dxh * xh, axis=-1, keepdims=True))
        part = jnp.sum(dhv * xh, axis=0, keepdims=True)

        @pl.when(pl.program_id(0) == 0)
        def _():
            dw_ref[...] = part

        @pl.when(pl.program_id(0) > 0)
        def _():
            dw_ref[...] += part

    row = pl.BlockSpec((tr, d), lambda i: (i, 0))
    vec = pl.BlockSpec((1, d), lambda i: (0, 0))
    return pl.pallas_call(
        body, name=name, grid=(l // tr,), in_specs=[row, row, vec, row], out_specs=[row, vec],
        out_shape=[jax.ShapeDtypeStruct((l, d), F32), jax.ShapeDtypeStruct((1, d), F32)],
        compiler_params=_params("arbitrary"),
    )(dh, x, w, dres)


def _final_loss(h, w, target, name):
    l, d = h.shape
    tr = _pick(l, 256, 8)

    def body(h_ref, w_ref, t_ref, dh_ref, dw_ref, loss_ref):
        xv = h_ref[...]
        r = lax.rsqrt(jnp.mean(xv * xv, axis=-1, keepdims=True) + EPS)
        xh = xv * r
        err = xh * w_ref[...] - t_ref[...]
        dy = err * (1.0 / d)
        dxh = dy * w_ref[...]
        dh_ref[...] = r * (dxh - xh * jnp.mean(dxh * xh, axis=-1, keepdims=True))
        part = jnp.sum(dy * xh, axis=0, keepdims=True)
        lpart = jnp.zeros((8, LANES), F32) + 0.5 * jnp.sum(jnp.mean(err * err, axis=-1, keepdims=True))

        @pl.when(pl.program_id(0) == 0)
        def _():
            dw_ref[...] = part
            loss_ref[...] = lpart

        @pl.when(pl.program_id(0) > 0)
        def _():
            dw_ref[...] += part
            loss_ref[...] += lpart

    row = pl.BlockSpec((tr, d), lambda i: (i, 0))
    vec = pl.BlockSpec((1, d), lambda i: (0, 0))
    return pl.pallas_call(
        body, name=name, grid=(l // tr,), in_specs=[row, vec, row],
        out_specs=[row, vec, pl.BlockSpec((8, LANES), lambda i: (0, 0))],
        out_shape=[jax.ShapeDtypeStruct((l, d), F32), jax.ShapeDtypeStruct((1, d), F32), jax.ShapeDtypeStruct((8, LANES), F32)],
        compiler_params=_params("arbitrary"),
    )(h, w, target)


def _gatenorm_fwd(y, proj_a, w, name):
    l, d = y.shape
    dg = d // NGROUPS
    tr = _pick(l, 256, 8)

    def body(y_ref, z_ref, w_ref, o_ref):
        for g in range(NGROUPS):
            sl = slice(g * dg, (g + 1) * dg)
            zv = z_ref[:, sl]
            u = y_ref[:, sl] * (zv * _sigmoid(zv))
            r = lax.rsqrt(jnp.mean(u * u, axis=-1, keepdims=True) + EPS)
            o_ref[:, sl] = (u * r * w_ref[:, sl]).astype(BF16)

    row = pl.BlockSpec((tr, d), lambda i: (i, 0))
    return pl.pallas_call(
        body, name=name, grid=(l // tr,), in_specs=[row, row, pl.BlockSpec((1, d), lambda i: (0, 0))],
        out_specs=row, out_shape=jax.ShapeDtypeStruct((l, 2 * d), BF16), compiler_params=_params("parallel"),
    )(y, proj_a, w)


def _gatenorm_bwd(dout, y, proj_a, w, name):
    l, d = y.shape
    dg = d // NGROUPS
    tr = _pick(l, 256, 8)

    def body(do_ref, y_ref, z_ref, w_ref, dy_ref, dz_ref, dw_ref):
        first = pl.program_id(0) == 0
        for g in range(NGROUPS):
            sl = slice(g * dg, (g + 1) * dg)
            zv = z_ref[:, sl]
            sg = _sigmoid(zv)
            sz = zv * sg
            yv = y_ref[:, sl]
            u = yv * sz
            r = lax.rsqrt(jnp.mean(u * u, axis=-1, keepdims=True) + EPS)
            nh = u * r
            dov = do_ref[:, sl]
            dn = dov * w_ref[:, sl]
            du = r * (dn - nh * jnp.mean(dn * nh, axis=-1, keepdims=True))
            dy_ref[:, sl] = du * sz
            dz_ref[:, sl] = (du * yv * (sg * (1.0 + zv * (1.0 - sg)))).astype(BF16)
            part = jnp.sum(dov * nh, axis=0, keepdims=True)

            @pl.when(first)
            def _():
                dw_ref[:, sl] = part

            @pl.when(jnp.logical_not(first))
            def _():
                dw_ref[:, sl] += part

    row = pl.BlockSpec((tr, d), lambda i: (i, 0))
    vec = pl.BlockSpec((1, d), lambda i: (0, 0))
    return pl.pallas_call(
        body, name=name, grid=(l // tr,), in_specs=[row, row, row, vec], out_specs=[row, row, vec],
        out_shape=[jax.ShapeDtypeStruct((l, d), F32), jax.ShapeDtypeStruct((l, d), BF16), jax.ShapeDtypeStruct((1, d), F32)],
        compiler_params=_params("arbitrary"),
    )(dout, y, proj_a, w)


def _conv_cols(d):
    return _pick(math.gcd(d, d + 2 * NGROUPS * NSTATE), 512, LANES)


def _conv_fwd(proj_a, conv_w, conv_b, d, name):
    l = proj_a.shape[0]
    cdim = d + 2 * NGROUPS * NSTATE
    cw = _conv_cols(d)
    off = d // cw
    tl = _pick(l, 512, 8)
    hb = tl // 8

    def body(u_ref, up_ref, w_ref, b_ref, acc_ref, xbc_ref, ext):
        i = pl.program_id(1)
        ext[0:8, :] = jnp.where(i > 0, up_ref[...], 0.0)
        ext[8:8 + tl, :] = u_ref[...]
        acc = jnp.zeros((tl, cw), F32) + b_ref[...]
        for j in range(KCONV):
            acc = acc + w_ref[j:j + 1, :] * ext[pl.ds(8 - (KCONV - 1) + j, tl), :]
        acc_ref[...] = acc
        xbc_ref[...] = acc * _sigmoid(acc)

    blk = pl.BlockSpec((tl, cw), lambda c, i: (i, c))
    return pl.pallas_call(
        body, name=name, grid=(cdim // cw, l // tl),
        in_specs=[pl.BlockSpec((tl, cw), lambda c, i: (i, c + off)),
                  pl.BlockSpec((8, cw), lambda c, i: (jnp.maximum(i * hb - 1, 0), c + off)),
                  pl.BlockSpec((KCONV, cw), lambda c, i: (0, c)), pl.BlockSpec((1, cw), lambda c, i: (0, c))],
        out_specs=[blk, blk],
        out_shape=[jax.ShapeDtypeStruct((l, cdim), F32), jax.ShapeDtypeStruct((l, cdim), F32)],
        scratch_shapes=[pltpu.VMEM((tl + 8, cw), F32)], compiler_params=_params("parallel", "parallel"),
    )(proj_a, proj_a, conv_w, conv_b)


def _conv_bwd(dxbc, acc, proj_a, conv_w, d, name):
    l, cdim = acc.shape
    cw = _conv_cols(d)
    off = d // cw
    tl = _pick(l, 512, 8)
    hb = tl // 8
    nb = l // tl

    def dsilu(g, a):
        s = _sigmoid(a)
        return g * (s * (1.0 + a * (1.0 - s)))

    def body(g_ref, gn_ref, a_ref, an_ref, u_ref, up_ref, w_ref, du_ref, dw_ref, ext, dext):
        i = pl.program_id(1)
        da = dsilu(g_ref[...], a_ref[...])
        dext[0:tl, :] = da
        dext[tl:tl + 8, :] = jnp.where(i < nb - 1, dsilu(gn_ref[...], an_ref[...]), 0.0)
        ext[0:8, :] = jnp.where(i > 0, up_ref[...], 0.0)
        ext[8:8 + tl, :] = u_ref[...]
        du = jnp.zeros((tl, cw), F32)
        rows = []
        for j in range(KCONV):
            du = du + w_ref[j:j + 1, :] * dext[pl.ds(KCONV - 1 - j, tl), :]
            rows.append(jnp.sum(da * ext[pl.ds(8 - (KCONV - 1) + j, tl), :], axis=0, keepdims=True))
        rows.append(jnp.sum(da, axis=0, keepdims=True))
        du_ref[...] = du.astype(BF16)

        @pl.when(i == 0)
        def _():
            dw_ref[...] = jnp.zeros_like(dw_ref)

        for j, rv in enumerate(rows):
            dw_ref[j:j + 1, :] += rv

    blk = pl.BlockSpec((tl, cw), lambda c, i: (i, c))
    nxt = pl.BlockSpec((8, cw), lambda c, i: (jnp.minimum((i + 1) * hb, l // 8 - 1), c))
    return pl.pallas_call(
        body, name=name, grid=(cdim // cw, nb),
        in_specs=[blk, nxt, blk, nxt,
                  pl.BlockSpec((tl, cw), lambda c, i: (i, c + off)),
                  pl.BlockSpec((8, cw), lambda c, i: (jnp.maximum(i * hb - 1, 0), c + off)),
                  pl.BlockSpec((KCONV, cw), lambda c, i: (0, c))],
        out_specs=[blk, pl.BlockSpec((8, cw), lambda c, i: (0, c))],
        out_shape=[jax.ShapeDtypeStruct((l, cdim), BF16), jax.ShapeDtypeStruct((8, cdim), F32)],
        scratch_shapes=[pltpu.VMEM((tl + 8, cw), F32), pltpu.VMEM((tl + 8, cw), F32)],
        compiler_params=_params("parallel", "arbitrary"),
    )(dxbc, dxbc, acc, acc, proj_a, proj_a, conv_w)


def _ssd_common(dt_ref, bias_ref, alog_ref):
    li = lax.broadcasted_iota(jnp.int32, (CHUNK, CHUNK), 0)
    si = lax.broadcasted_iota(jnp.int32, (CHUNK, CHUNK), 1)
    tri = si <= li
    dtv = _softplus(dt_ref[...] + bias_ref[0])
    a_neg = -jnp.exp(alog_ref[0])
    da = dtv * a_neg
    cs_col = sum(_dot(tri.astype(BF16), p, NN) for p in _split3(da))
    cs_row = sum(_dot(p, (li <= si).astype(BF16), TN) for p in _split3(da))
    return tri, dtv, a_neg, cs_col, cs_row


def _dot2(x, t16, dims):
    hi, lo = _split2(x)
    if dims == NN:
        return _dot(jnp.concatenate([hi, lo], axis=1), jnp.concatenate([t16, t16], axis=0), NN)
    return _dot(hi, t16, dims) + _dot(lo, t16, dims)


def _ssd_expand(dtv, cs_col, dsk, pg):
    expm = (lax.shift_right_logical(lax.broadcasted_iota(jnp.int32, (LANES, pg), 1), SSD_HD_SHIFT)
            == lax.broadcasted_iota(jnp.int32, (LANES, pg), 0)).astype(BF16)
    tot = cs_col[CHUNK - 1:CHUNK, :]
    stack = jnp.concatenate([dtv, jnp.exp(cs_col), jnp.exp(tot - cs_col), jnp.broadcast_to(dsk, (CHUNK, LANES)),
                             jnp.broadcast_to(jnp.exp(tot), (CHUNK, LANES))], axis=0)
    ex = _dot2(stack, expm, NN)
    return tuple(ex[CHUNK * a:CHUNK * (a + 1)] for a in range(5))


SSD_GPS = 4


def _ssd_specs(d, r_heads):
    pg = r_heads * SSD_HD
    gw = SSD_GPS * LANES
    nb = d // gw
    dt_blk = (2 * d + 2 * NGROUPS * NSTATE) // gw
    x_spec = lambda cmap: pl.BlockSpec((CHUNK, SSD_GPS * pg), lambda g, c: (cmap(c), g))
    b_spec = lambda cmap: pl.BlockSpec((CHUNK, gw), lambda g, c: (cmap(c), nb + g))
    c_spec = lambda cmap: pl.BlockSpec((CHUNK, gw), lambda g, c: (cmap(c), nb + NGROUPS // SSD_GPS + g))
    dt_spec = lambda cmap: pl.BlockSpec((CHUNK, gw), lambda g, c: (cmap(c), dt_blk + g))
    const = pl.BlockSpec((SSD_GPS, 1, LANES), lambda g, c: (g, 0, 0))
    return pg, x_spec, b_spec, c_spec, dt_spec, const


def _ssd_group_views(refs, kinds, gi, pg):
    out = []
    for r, kind in zip(refs, kinds):
        if kind == "x":
            out.append(r.at[:, pl.ds(gi * pg, pg)])
        elif kind == "g":
            out.append(r.at[:, pl.ds(gi * LANES, LANES)])
        elif kind == "c":
            out.append(r.at[pl.ds(gi, 1)])
        elif kind == "h":
            out.append(r.at[:, pl.ds(gi, 1)])
        else:
            out.append(r.at[gi])
    return out


def _ssd_fwd(xbc, proj_a, bias_g, alog_g, dsk_g, d, name):
    l = xbc.shape[0]
    nc = l // CHUNK
    r_heads = d // SSD_HD // NGROUPS
    pg, x_spec, b_spec, c_spec, dt_spec, const = _ssd_specs(d, r_heads)
    ident = lambda c: c

    def body(*refs):
        for gi in range(SSD_GPS):
            group(*_ssd_group_views(refs, "xgggcccxhs", gi, pg))

    def group(x_ref, b_ref, c_ref, dt_ref, bias_ref, alog_ref, dsk_ref, y_ref, hs_ref, h_scr):
        @pl.when(pl.program_id(1) == 0)
        def _():
            h_scr[...] = jnp.zeros_like(h_scr)

        tri, dtv, _, cs_col, cs_row = _ssd_common(dt_ref, bias_ref, alog_ref)
        hin = h_scr[...]
        hs_ref[0, 0] = hin
        bm = b_ref[...].astype(BF16)
        cm = c_ref[...].astype(BF16)
        xv = x_ref[...]
        e_dt, e_ecs, e_decs, e_dsk, e_etot = _ssd_expand(dtv, cs_col, dsk_ref[0], pg)
        xd = xv * e_dt
        xd16 = xd.astype(BF16)
        gmat = _dot(cm, bm, NT)
        for r in range(r_heads):
            sl = slice(SSD_HD * r, SSD_HD * (r + 1))
            lm = jnp.exp(jnp.where(tri, cs_col[:, r:r + 1] - cs_row[r:r + 1, :], -jnp.inf))
            y_ref[:, sl] = _dot((gmat * lm).astype(BF16), xd16[:, sl], NN)
        y_ref[...] += e_ecs * _dot(cm, hin.astype(BF16), NN) + e_dsk * xv
        h_scr[...] = hin * e_etot[0:1] + _dot(bm, (xd * e_decs).astype(BF16), TN)

    return pl.pallas_call(
        body, name=name, grid=(NGROUPS // SSD_GPS, nc),
        in_specs=[x_spec(ident), b_spec(ident), c_spec(ident), dt_spec(ident), const, const, const],
        out_specs=[x_spec(ident), pl.BlockSpec((1, SSD_GPS, NSTATE, pg), lambda g, c: (c, g, 0, 0))],
        out_shape=[jax.ShapeDtypeStruct((l, d), F32), jax.ShapeDtypeStruct((nc, NGROUPS, NSTATE, pg), F32)],
        scratch_shapes=[pltpu.VMEM((SSD_GPS, NSTATE, pg), F32)], compiler_params=_params("parallel", "arbitrary"),
    )(xbc, xbc, xbc, proj_a, bias_g, alog_g, dsk_g)


def _ssd_bwd(dy, xbc, proj_a, hs, bias_g, alog_g, dsk_g, d, name):
    l = xbc.shape[0]
    nc = l // CHUNK
    r_heads = d // SSD_HD // NGROUPS
    pg, x_spec, b_spec, c_spec, dt_spec, const = _ssd_specs(d, r_heads)
    rev = lambda c: nc - 1 - c
    cdim = d + 2 * NGROUPS * NSTATE

    def body(*refs):
        for gi in range(SSD_GPS):
            group(*_ssd_group_views(refs, "xxggghcccxgggcssss", gi, pg))

    def group(dy_ref, x_ref, b_ref, c_ref, dt_ref, hs_ref, bias_ref, alog_ref, dsk_ref,
              dx_ref, db_ref, dc_ref, ddt_ref, sums_ref, dh_scr, p_scr, pt_scr, dxd_scr):
        first = pl.program_id(1) == 0

        @pl.when(first)
        def _():
            dh_scr[...] = jnp.zeros_like(dh_scr)
            sums_ref[...] = jnp.zeros_like(sums_ref)

        tri, dtv, a_neg, cs_col, cs_row = _ssd_common(dt_ref, bias_ref, alog_ref)
        li = lax.broadcasted_iota(jnp.int32, (CHUNK, CHUNK), 0)
        si = lax.broadcasted_iota(jnp.int32, (CHUNK, CHUNK), 1)
        tri_t = li <= si
        lastrow = lax.broadcasted_iota(jnp.int32, (CHUNK, 1), 0) == CHUNK - 1
        indm = (lax.shift_right_logical(lax.broadcasted_iota(jnp.int32, (pg, LANES), 0), SSD_HD_SHIFT)
                == lax.broadcasted_iota(jnp.int32, (pg, LANES), 1)).astype(BF16)
        hin = hs_ref[0, 0]
        dhout = dh_scr[...]
        hin16 = hin.astype(BF16)
        dhout16 = dhout.astype(BF16)
        bm = b_ref[...].astype(BF16)
        cm = c_ref[...].astype(BF16)
        xv = x_ref[...]
        dyv = dy_ref[...]
        e_dt, e_ecs, e_decs, e_dsk, e_etot = _ssd_expand(dtv, cs_col, dsk_ref[0], pg)
        xd = xv * e_dt
        xd16 = xd.astype(BF16)
        dy16 = dyv.astype(BF16)
        dye = dyv * e_ecs
        dye16 = dye.astype(BF16)
        gmat = _dot(cm, bm, NT)
        gmat_t = _dot(bm, cm, NT)
        dg = jnp.zeros((CHUNK, CHUNK), F32)
        dg_t = jnp.zeros((CHUNK, CHUNK), F32)
        for r in range(r_heads):
            sl = slice(SSD_HD * r, SSD_HD * (r + 1))
            col = cs_col[:, r:r + 1]
            row = cs_row[r:r + 1, :]
            lm = jnp.exp(jnp.where(tri, col - row, -jnp.inf))
            lm_t = jnp.exp(jnp.where(tri_t, row - col, -jnp.inf))
            dm = _dot(dy16[:, sl], xd16[:, sl], NT)
            dm_t = _dot(xd16[:, sl], dy16[:, sl], NT)
            dg = dg + dm * lm
            dg_t = dg_t + dm_t * lm_t
            m_t = gmat_t * lm_t
            p_scr[:, sl] = dm * (gmat * lm)
            pt_scr[:, sl] = dm_t * m_t
            dxd_scr[:, sl] = _dot(m_t.astype(BF16), dy16[:, sl], NN)
        yoff = _dot(cm, hin16, NN)
        qall = _dot(bm, dhout16, NN)
        dxd = dxd_scr[...] + qall * e_decs
        hh = jnp.broadcast_to(jnp.sum(dhout * hin, axis=0, keepdims=True), (8, pg))
        red = _dot2(jnp.concatenate([dye * yoff, qall * xd, dxd * xv, dyv * xv, p_scr[...], pt_scr[...], hh], axis=0), indm, NN)
        r_yoff, r_q, r_dt, r_dsk, r_p, r_pt = (red[CHUNK * a:CHUNK * (a + 1)] for a in range(6))
        tot = cs_col[CHUNK - 1:CHUNK, :]
        ddec = r_q * jnp.exp(tot - cs_col)
        dtot = jnp.sum(ddec, axis=0, keepdims=True) + jnp.exp(tot) * red[6 * CHUNK:6 * CHUNK + 1]
        dcs = r_p - r_pt + r_yoff - ddec + jnp.where(lastrow, dtot, 0.0)
        dda = sum(_dot(tri_t.astype(BF16), p, NN) for p in _split3(dcs))
        draw = (r_dt + dda * a_neg) * _sigmoid(dt_ref[...] + bias_ref[0])
        ddt_ref[...] = draw
        sums_ref[0, 0:1, :] += jnp.sum(draw, axis=0, keepdims=True)
        sums_ref[0, 1:2, :] += jnp.sum(dda * dtv, axis=0, keepdims=True)
        sums_ref[0, 2:3, :] += jnp.sum(r_dsk, axis=0, keepdims=True)
        dx_ref[...] = dxd * e_dt + e_dsk * dyv
        dc_ref[...] = _dot(dg.astype(BF16), bm, NN) + _dot(dye16, hin16, NT)
        db_ref[...] = _dot(dg_t.astype(BF16), cm, NN) + _dot((xd * e_decs).astype(BF16), dhout16, NT)
        dh_scr[...] = dhout * e_etot[0:1] + _dot(cm, dye16, TN)

    grp = pl.BlockSpec((CHUNK, SSD_GPS * LANES), lambda g, c: (rev(c), g))
    return pl.pallas_call(
        body, name=name, grid=(NGROUPS // SSD_GPS, nc),
        in_specs=[x_spec(rev), x_spec(rev), b_spec(rev), c_spec(rev), dt_spec(rev),
                  pl.BlockSpec((1, SSD_GPS, NSTATE, pg), lambda g, c: (rev(c), g, 0, 0)), const, const, const],
        out_specs=[x_spec(rev), grp, grp, grp, pl.BlockSpec((SSD_GPS, 8, LANES), lambda g, c: (g, 0, 0))],
        out_shape=[jax.ShapeDtypeStruct((l, d), F32), jax.ShapeDtypeStruct((l, NGROUPS * NSTATE), F32),
                   jax.ShapeDtypeStruct((l, NGROUPS * NSTATE), F32),
                   jax.ShapeDtypeStruct((l, NGROUPS * LANES), F32), jax.ShapeDtypeStruct((NGROUPS, 8, LANES), F32)],
        scratch_shapes=[pltpu.VMEM((SSD_GPS, NSTATE, pg), F32)] + [pltpu.VMEM((SSD_GPS, CHUNK, pg), F32)] * 3,
        compiler_params=_params("parallel", "arbitrary"),
    )(dy, xbc, xbc, xbc, proj_a, hs, bias_g, alog_g, dsk_g)


SBA_TK = 256
SBA_TQ_FWD = 1024
SBA_TQ_BWD = 1024
SBA_ROW_PARTS = 8


def _sba_tiles(l, tq_target):
    tk = _pick(l, SBA_TK, LANES)
    tq = _pick(l, tq_target, tk)
    assert l // tk <= LANES
    return tq, tk


def _sba_scores(qk, valid, scale):
    z = qk * scale
    nz = -z
    lk = jnp.minimum(nz, 0.0) - jnp.log(1.0 + jnp.exp(jnp.minimum(z, nz)))
    if valid is not None:
        lk = jnp.where(valid, lk, 0.0)
    return z, lk


def _sba_fwd(proj_a, proj_c, mix, d, g_off, name):
    l = proj_c.shape[0]
    nh = d // SBA_HD
    tq, tk = _sba_tiles(l, SBA_TQ_FWD)
    band = tq // tk
    scale = 1.0 / math.sqrt(SBA_HD)
    rq = tq // SBA_ROW_PARTS if tq % (SBA_ROW_PARTS * 16) == 0 else tq
    parts = [pl.ds(p * rq, rq) for p in range(tq // rq)]

    def body(q_ref, k_ref, v_ref, g_ref, mix_ref, o_ref, y_ref, rs_ref, rs_scr, qk_a, qk_b):
        del mix_ref
        i = pl.program_id(1)
        ki = lax.broadcasted_iota(jnp.int32, (tk, tk), 0)
        kj = lax.broadcasted_iota(jnp.int32, (tk, tk), 1)
        uinc = (ki >= kj).astype(BF16)
        lane = lax.broadcasted_iota(jnp.int32, (rq, LANES), 1)
        rs_scr[...] = jnp.zeros_like(rs_scr)
        qs = [q_ref[ps, :] for ps in parts]

        def qk_into(j, qk_scr):
            kb = k_ref[pl.ds(pl.multiple_of(j * tk, tk), tk), :]
            for p, ps in enumerate(parts):
                qk_scr[ps, :] = _dot(qs[p], kb, NT)

        def tile(j, carry, band_pos=None, qk_scr=None):
            start = pl.multiple_of(j * tk, tk)
            vb = v_ref[pl.ds(start, tk), :]
            live, valid = list(range(len(parts))), {}
            if band_pos is not None:
                k0, k1 = band_pos * tk, (band_pos + 1) * tk - 1
                live = [p for p in live if (p + 1) * rq - 1 > k0]
                rows = lax.broadcasted_iota(jnp.int32, (rq, tk), 0)
                cols = lax.broadcasted_iota(jnp.int32, (rq, tk), 1)
                valid = {p: cols - rows < p * rq - k0 for p in live if p * rq <= k1}
            if qk_scr is None:
                kb = k_ref[pl.ds(start, tk), :]
                qk = {p: _dot(qs[p], kb, NT) for p in live}
            else:
                qk = {p: qk_scr[parts[p], :] for p in live}
            sc = {p: _sba_scores(qk[p], valid.get(p), scale) for p in live}
            tail = {p: _dot2(sc[p][1], uinc, NN) for p in live}
            out = list(carry)
            for p in live:
                rsum, acc = carry[p]
                w = jnp.exp(sc[p][0] + tail[p] + rsum)
                if p in valid:
                    w = jnp.where(valid[p], w, 0.0)
                acc = acc + _dot(w.astype(BF16), vb, NN)
                rs_scr[parts[p], :] = jnp.where(lane == j, rsum, rs_scr[parts[p], :])
                out[p] = (rsum + tail[p][:, 0:1], acc)
            return tuple(out)

        carry = tuple((jnp.zeros((rq, 1), F32), jnp.zeros((rq, SBA_HD), F32)) for _ in parts)
        nfull = i * band
        for band_pos in reversed(range(band)):
            carry = tile(nfull + band_pos, carry, band_pos)
        if band % 2 == 0:
            @pl.when(nfull > 0)
            def _():
                qk_into(nfull - 1, qk_a)

            def pair(m, c):
                ja = nfull - 1 - 2 * m
                qk_into(ja - 1, qk_b)
                c = tile(ja, c, None, qk_a)
                qk_into(jnp.maximum(ja - 2, 0), qk_a)
                return tile(ja - 1, c, None, qk_b)

            carry = lax.fori_loop(0, nfull // 2, pair, carry)
        else:
            carry = lax.fori_loop(0, nfull, lambda jj, c: tile(nfull - 1 - jj, c), carry)
        rs_ref[...] = rs_scr[...]
        for p, ps in enumerate(parts):
            acc = carry[p][1]
            o_ref[ps, :] = acc
            gv = g_ref[ps, :]
            y_ref[ps, :] = (acc * (gv * _sigmoid(gv))).astype(BF16)

    goff = g_off // SBA_HD
    blk = lambda off: pl.BlockSpec((tq, SBA_HD), lambda h, i: (i, h + off))
    full = lambda off: pl.BlockSpec((l, SBA_HD), lambda h, i: (0, h + off))
    out = pl.BlockSpec((tq, SBA_HD), lambda h, i: (i, h))
    return pl.pallas_call(
        body, name=name, grid=(nh, l // tq), in_specs=[blk(0), full(nh), full(2 * nh), blk(goff), HBM],
        out_specs=[out, blk(nh), out],
        scratch_shapes=[pltpu.VMEM((tq, LANES), F32), pltpu.VMEM((tq, tk), F32), pltpu.VMEM((tq, tk), F32)],
        out_shape=[jax.ShapeDtypeStruct((l, d), F32), jax.ShapeDtypeStruct((l, 2 * d), BF16), jax.ShapeDtypeStruct((l, d), F32)],
        input_output_aliases={4: 1}, compiler_params=_params("parallel", "arbitrary"),
    )(proj_c, proj_c, proj_c, proj_a, mix)


def _sba_bwd(dys, o, rs, proj_a, proj_c, d, g_off, name):
    l = proj_c.shape[0]
    nh = d // SBA_HD
    tq, tk = _sba_tiles(l, SBA_TQ_BWD)
    band = tq // tk
    scale = 1.0 / math.sqrt(SBA_HD)

    def body(dy_ref, o_ref, rs_ref, q_ref, k_ref, v_ref, g_ref, dq_ref, dk_ref, dv_ref, dg_ref, dk_acc, dv_acc,
             qk_a, qk_b, dw_a, dw_b):
        i = pl.program_id(1)
        nq = pl.num_programs(1)

        @pl.when(i == 0)
        def _():
            dk_acc[...] = jnp.zeros_like(dk_acc)
            dv_acc[...] = jnp.zeros_like(dv_acc)

        ki = lax.broadcasted_iota(jnp.int32, (tk, tk), 0)
        kj = lax.broadcasted_iota(jnp.int32, (tk, tk), 1)
        uinc = (ki >= kj).astype(BF16)
        ulow = (ki < kj).astype(BF16)
        lane = lax.broadcasted_iota(jnp.int32, (tq, LANES), 1)
        q = q_ref[...]
        gv = g_ref[...]
        sg = _sigmoid(gv)
        dyv = dy_ref[...]
        dg_ref[...] = (dyv * o_ref[...] * (sg * (1.0 + gv * (1.0 - sg)))).astype(BF16)
        do = dyv * (gv * sg)
        do16 = do.astype(BF16)
        q_t = q.astype(F32).T.astype(BF16)
        do_t = do.T.astype(BF16)

        def ahead(j, qk_scr, dw_scr):
            start = pl.multiple_of(j * tk, tk)
            qk_scr[...] = _dot(q, k_ref[pl.ds(start, tk), :], NT)
            dw_scr[...] = _dot(do16, v_ref[pl.ds(start, tk), :], NT)

        def tile(j, carry, band_pos=None, qk_scr=None, dw_scr=None):
            r0 = 0 if band_pos is None else band_pos * tk
            epre, dq = carry[0][r0:], carry[1][r0:]
            qr, dor = q[r0:], do16[r0:]
            start = pl.multiple_of(j * tk, tk)
            kb = k_ref[pl.ds(start, tk), :]
            rsum = jnp.sum(jnp.where(lane[r0:] == j, rs_ref[r0:, :], 0.0), axis=1, keepdims=True)
            valid = None
            if band_pos is not None:
                valid = (lax.broadcasted_iota(jnp.int32, (tq - r0, tk), 1) < lax.broadcasted_iota(jnp.int32, (tq - r0, tk), 0))
            if qk_scr is None:
                qk = _dot(qr, kb, NT)
                dw = _dot(dor, v_ref[pl.ds(start, tk), :], NT)
            else:
                qk = qk_scr[...]
                dw = dw_scr[...]
            z, lk = _sba_scores(qk, valid, scale)
            w = jnp.exp(z + _dot2(lk, uinc, NN) + rsum)
            if valid is not None:
                w = jnp.where(valid, w, 0.0)
            e = w * dw
            epx = _dot2(e, ulow, NN)
            dz = (e - jnp.exp(z + lk) * (e + epre + epx)) * scale
            if valid is not None:
                dz = jnp.where(valid, dz, 0.0)
            dz16 = dz.astype(BF16)
            dq = dq + _dot(dz16, kb, NN)
            dk_acc[j] += _dot(q_t[:, r0:], dz16, NN)
            dv_acc[j] += _dot(do_t[:, r0:], w.astype(BF16), NN)
            epre = epre + epx[:, tk - 1:tk] + e[:, tk - 1:tk]
            if r0:
                epre = jnp.concatenate([carry[0][:r0], epre], axis=0)
                dq = jnp.concatenate([carry[1][:r0], dq], axis=0)
            return epre, dq

        carry = (jnp.zeros((tq, 1), F32), jnp.zeros((tq, SBA_HD), F32))
        nfull = i * band
        if band % 2 == 0:
            @pl.when(nfull > 0)
            def _():
                ahead(0, qk_a, dw_a)

            def pair(m, c):
                ja = 2 * m
                ahead(ja + 1, qk_b, dw_b)
                c = tile(ja, c, None, qk_a, dw_a)
                ahead(jnp.minimum(ja + 2, nfull - 1), qk_a, dw_a)
                return tile(ja + 1, c, None, qk_b, dw_b)

            carry = lax.fori_loop(0, nfull // 2, pair, carry)
        else:
            carry = lax.fori_loop(0, nfull, lambda j, c: tile(j, c), carry)
        for band_pos in range(band):
            carry = tile(nfull + band_pos, carry, band_pos)
        dq_ref[...] = carry[1].astype(BF16)

        @pl.when(i == nq - 1)
        def _():
            for jj in range(l // tk):
                dk_ref[jj * tk:(jj + 1) * tk, :] = dk_acc[jj].T.astype(BF16)
                dv_ref[jj * tk:(jj + 1) * tk, :] = dv_acc[jj].T.astype(BF16)

    goff = g_off // SBA_HD
    blk = lambda off: pl.BlockSpec((tq, SBA_HD), lambda h, i: (i, h + off))
    full = lambda off: pl.BlockSpec((l, SBA_HD), lambda h, i: (0, h + off))
    out = pl.BlockSpec((tq, SBA_HD), lambda h, i: (i, h))
    outfull = pl.BlockSpec((l, SBA_HD), lambda h, i: (0, h))
    sd = jax.ShapeDtypeStruct((l, d), BF16)
    return pl.pallas_call(
        body, name=name, grid=(nh, l // tq),
        in_specs=[blk(nh), out, out, blk(0), full(nh), full(2 * nh), blk(goff)],
        out_specs=[out, outfull, outfull, out], out_shape=[sd, sd, sd, sd],
        scratch_shapes=[pltpu.VMEM((l // tk, SBA_HD, tk), F32)] * 2 + [pltpu.VMEM((tq, tk), F32)] * 4,
        compiler_params=_params("parallel", "arbitrary"),
    )(dys, o, rs, proj_c, proj_c, proj_c, proj_a)


def _adamw_math(w, g, m, v):
    m = ADAM_B1 * m + (1.0 - ADAM_B1) * g
    v = ADAM_B2 * v + (1.0 - ADAM_B2) * (g * g)
    m_hat = m / (1.0 - ADAM_B1 ** ADAM_STEP)
    v_hat = v / (1.0 - ADAM_B2 ** ADAM_STEP)
    delta = -ADAM_LR * (m_hat / (jnp.sqrt(v_hat) + ADAM_EPS) + ADAM_WD * w)
    return delta, m, v


def _adamw(w, g, m, v, name):
    a, r, c = w.shape
    tc = _pick(c, 512, LANES)
    tr = _pick(r, max(8, (5 << 18) // tc // 8 * 8), 8)

    def body(w_ref, g_ref, m_ref, v_ref, d_ref, nm_ref, nv_ref):
        dl, nm, nv = _adamw_math(w_ref[...], g_ref[...], m_ref[...], v_ref[...])
        d_ref[...] = dl
        nm_ref[...] = nm
        nv_ref[...] = nv

    blk = pl.BlockSpec((1, tr, tc), lambda i, j, k: (i, j, k))
    sd = jax.ShapeDtypeStruct(w.shape, F32)
    return pl.pallas_call(
        body, name=name, grid=(a, r // tr, c // tc), in_specs=[blk] * 4, out_specs=[blk] * 3, out_shape=[sd] * 3,
        compiler_params=_params("parallel", "parallel", "parallel"),
    )(w, g, m, v)


def _dims(d):
    cdim = d + 2 * NGROUPS * NSTATE
    heads = d // SSD_HD
    r_heads = heads // NGROUPS
    g_off = d + cdim + NGROUPS * LANES
    na = g_off + d
    nc = 3 * d
    return cdim, heads, r_heads, na, nc, g_off


def _pack_w_in(w_in, d):
    cdim, heads, r_heads, na, nc, g_off = _dims(d)
    o = d + cdim
    w_dt = w_in[:, o:o + heads].reshape(d, NGROUPS, r_heads)
    w_dt = jnp.pad(w_dt, ((0, 0), (0, 0), (0, LANES - r_heads))).reshape(d, NGROUPS * LANES)
    return jnp.concatenate([w_in[:, :o], w_dt, w_in[:, o + heads + nc:]], axis=1), w_in[:, o + heads:o + heads + nc]


def _unpack_w_in(ga, gc, d):
    cdim, heads, r_heads, na, nc, g_off = _dims(d)
    o = d + cdim
    g_dt = ga[:, o:g_off].reshape(d, NGROUPS, LANES)[:, :, :r_heads].reshape(d, heads)
    return jnp.concatenate([ga[:, :o], g_dt, gc, ga[:, g_off:]], axis=1)


def _group_vec(v, r_heads):
    return jnp.pad(v.reshape(NGROUPS, 1, r_heads), ((0, 0), (0, 0), (0, LANES - r_heads)))


def _layer_fwd(x, p, d, tag):
    cdim, heads, r_heads, na, nc, g_off = _dims(d)
    h = _rmsnorm_fwd(x, p["norm_w"], f"norm_f{tag}")
    proj_a = _matmul(h, p["wa"], "nn", F32, f"inproj_a{tag}")
    proj_c = _matmul(h, p["wc"], "nn", BF16, f"inproj_c{tag}")
    acc, xbc = _conv_fwd(proj_a, p["conv_w"], p["conv_b"], d, f"conv_f{tag}")
    y, hs = _ssd_fwd(xbc, proj_a, p["bias_g"], p["alog_g"], p["dsk_g"], d, f"ssd_f{tag}")
    mix = _gatenorm_fwd(y, proj_a, p["ssd_norm_w"], f"gate_f{tag}")
    o, mix, rs = _sba_fwd(proj_a, proj_c, mix, d, g_off, f"sba_f{tag}")
    x_next = _matmul(mix, p["w_out"], "nn", F32, f"outproj{tag}", add=x)
    return x_next, dict(x=x, h=h, proj_a=proj_a, proj_c=proj_c, acc=acc, xbc=xbc, y=y, hs=hs, o=o, rs=rs, mix=mix)


def _layer_bwd(dxn, s, p, d, tag):
    cdim, heads, r_heads, na, nc, g_off = _dims(d)
    dxn16 = dxn.astype(BF16)
    dmix = _matmul(dxn16, p["w_out"], "nt", F32, f"dmix{tag}")
    g_w_out = _matmul(s["mix"], dxn16, "tn", F32, f"dwout{tag}")
    dq, dk, dv, dg = _sba_bwd(dmix, s["o"], s["rs"], s["proj_a"], s["proj_c"], d, g_off, f"sba_b{tag}")
    dy, dz, g_ssd_norm = _gatenorm_bwd(dmix, s["y"], s["proj_a"], p["ssd_norm_w"], f"gate_b{tag}")
    dx_s, db_s, dc_s, ddt, sums = _ssd_bwd(dy, s["xbc"], s["proj_a"], s["hs"], p["bias_g"], p["alog_g"], p["dsk_g"], d, f"ssd_b{tag}")
    dxbc = jnp.concatenate([dx_s, db_s, dc_s], axis=1)
    du, g_conv = _conv_bwd(dxbc, s["acc"], s["proj_a"], p["conv_w"], d, f"conv_b{tag}")
    dproj_a = jnp.concatenate([dz, du, ddt.astype(BF16), dg], axis=1)
    dproj_c = jnp.concatenate([dq, dk, dv], axis=1)
    g_wa = _matmul(s["h"], dproj_a, "tn", F32, f"dwin_a{tag}")
    g_wc = _matmul(s["h"], dproj_c, "tn", F32, f"dwin_c{tag}")
    dh = _matmul(dproj_a, p["wa"], "nt", F32, f"dh_a{tag}")
    dh = _matmul(dproj_c, p["wc"], "nt", F32, f"dh_c{tag}", add=dh)
    dx, g_norm = _rmsnorm_bwd(dh, s["x"], p["norm_w"], dxn, f"norm_b{tag}")
    a_neg = -jnp.exp(p["alog_g"][:, 0, :r_heads].reshape(heads))
    grads = dict(
        norm_w=g_norm[0], w_in=_unpack_w_in(g_wa, g_wc, d), conv_w=g_conv[:KCONV], conv_b=g_conv[KCONV],
        dt_bias=sums[:, 0, :r_heads].reshape(heads), a_log=sums[:, 1, :r_heads].reshape(heads) * a_neg,
        d_skip=sums[:, 2, :r_heads].reshape(heads), ssd_norm_w=g_ssd_norm[0], w_out=g_w_out)
    return dx, grads


def _local_step(x, target, w_in16, w_out16, conv_w, small):
    l, d = x.shape
    depth = w_in16.shape[0]
    r_heads = _dims(d)[2]
    layers = []
    for i in range(depth):
        wa, wc = _pack_w_in(w_in16[i], d)
        layers.append(dict(
            norm_w=small["norm_w"][i][None], wa=wa, wc=wc, conv_w=conv_w[i], conv_b=small["conv_b"][i][None],
            bias_g=_group_vec(small["dt_bias"][i], r_heads), alog_g=_group_vec(small["a_log"][i], r_heads),
            dsk_g=_group_vec(small["d_skip"][i], r_heads), ssd_norm_w=small["ssd_norm_w"][i][None], w_out=w_out16[i]))
    saved = []
    hcur = x
    for i in range(depth):
        hcur, s = _layer_fwd(hcur, layers[i], d, str(i))
        saved.append(s)
    dh, g_final, loss = _final_loss(hcur, small["final_norm_w"][None], target, "final_loss")
    grads = [None] * depth
    for i in reversed(range(depth)):
        dh, grads[i] = _layer_bwd(dh, saved[i], layers[i], d, str(i))
    stacked = {k: jnp.stack([g[k] for g in grads]) for k in grads[0]}
    stacked["final_norm_w"] = g_final[0]
    return loss[0, 0], dh, stacked


HBM = pl.BlockSpec(memory_space=pl.ANY)
NCHIP = 4
NDEV = 8


def _mesh_pos():
    x, y, c = lax.axis_index("x"), lax.axis_index("y"), lax.axis_index("c")
    chips = [(1 - x, y), (x, 1 - y), (1 - x, 1 - y)]
    return x, y, c, chips


def _remote(src, dst, send_sem, recv_sem, dev):
    return pltpu.make_async_remote_copy(src_ref=src, dst_ref=dst, send_sem=send_sem, recv_sem=recv_sem,
                                        device_id=dev, device_id_type=MESH)


def _gather_weights(shards):
    n = len(shards)
    hl = shards[0].shape[0] // 2

    def body(*refs):
        ins, outs = refs[:n], refs[n:2 * n]
        send, recv = refs[2 * n:]
        x, y, c, chips = _mesh_pos()
        k = 2 * x + y
        half = pl.ds(c * hl, hl)
        other = pl.ds((1 - c) * hl, hl)
        for a in range(n):
            for j, (px, py) in enumerate(chips):
                _remote(ins[a].at[half], outs[a].at[k, half], send.at[a, j], recv.at[a, j], (px, py, c)).start()
        for a in range(n):
            for j, (px, py) in enumerate(chips):
                kj = 2 * px + py
                got = outs[a].at[kj, half]
                _remote(got, got, send.at[a, j], recv.at[a, j], (px, py, c)).wait_recv()
                _remote(got, got, send.at[a, 3 + j], recv.at[a, 3 + j], (x, y, 1 - c)).start()
        for a in range(n):
            for j, (px, py) in enumerate(chips):
                kj = 2 * px + py
                _remote(outs[a].at[kj, other], outs[a].at[kj, other], send.at[a, 3 + j], recv.at[a, 3 + j], (x, y, 1 - c)).wait_recv()
            for j, (px, py) in enumerate(chips):
                kj = 2 * px + py
                _remote(ins[a].at[half], outs[a].at[k, half], send.at[a, j], recv.at[a, j], (px, py, c)).wait_send()
                _remote(outs[a].at[kj, half], outs[a].at[kj, half], send.at[a, 3 + j], recv.at[a, 3 + j], (x, y, 1 - c)).wait_send()

    return pl.pallas_call(
        body, name="gather_weights", in_specs=[HBM] * n, out_specs=[HBM] * n,
        out_shape=[jax.ShapeDtypeStruct((NCHIP,) + s.shape, s.dtype) for s in shards],
        scratch_shapes=[pltpu.SemaphoreType.DMA((n, 6)), pltpu.SemaphoreType.DMA((n, 6))],
    )(*shards)


def _swap_halves(parts):
    n = len(parts)
    hl = parts[0].shape[1] // 2

    def body(*refs):
        ins, outs = refs[:n], refs[n:2 * n]
        send, recv = refs[2 * n:]
        x, y, c, _ = _mesh_pos()
        cps = [_remote(ins[a].at[:, pl.ds((1 - c) * hl, hl)], outs[a], send.at[a], recv.at[a], (x, y, 1 - c)) for a in range(n)]
        for cp in cps:
            cp.start()
        for cp in cps:
            cp.wait()

    return pl.pallas_call(
        body, name="grad_swap_halves", in_specs=[HBM] * n, out_specs=[HBM] * n,
        out_shape=[jax.ShapeDtypeStruct((NCHIP, hl) + p.shape[2:], p.dtype) for p in parts],
        scratch_shapes=[pltpu.SemaphoreType.DMA((n,)), pltpu.SemaphoreType.DMA((n,))],
    )(*parts)


def _exchange_shards(parts):
    n = len(parts)

    def body(*refs):
        ins, outs = refs[:n], refs[n:2 * n]
        send, recv = refs[2 * n:]
        x, y, c, chips = _mesh_pos()
        k = 2 * x + y
        work = []
        for a in range(n):
            for j, (px, py) in enumerate(chips):
                cp = _remote(ins[a].at[2 * px + py], outs[a].at[k], send.at[a, j], recv.at[a, j], (px, py, c))
                cp.start()
                work.append(cp)
        for w in work:
            w.wait()

    return pl.pallas_call(
        body, name="grad_exchange", in_specs=[HBM] * n, out_specs=[HBM] * n,
        out_shape=[jax.ShapeDtypeStruct(p.shape, p.dtype) for p in parts],
        scratch_shapes=[pltpu.SemaphoreType.DMA((n, 3)), pltpu.SemaphoreType.DMA((n, 3))],
    )(*parts)


def _swap_reduced(halves):
    n = len(halves)

    def body(*refs):
        ins, outs = refs[:n], refs[n:2 * n]
        send, recv = refs[2 * n:]
        x, y, c, _ = _mesh_pos()
        work = [_remote(ins[a], outs[a], send.at[a], recv.at[a], (x, y, 1 - c)) for a in range(n)]
        for w in work:
            w.start()
        for w in work:
            w.wait()

    return pl.pallas_call(
        body, name="grad_swap_reduced", in_specs=[HBM] * n, out_specs=[HBM] * n,
        out_shape=[jax.ShapeDtypeStruct(h.shape, h.dtype) for h in halves],
        scratch_shapes=[pltpu.SemaphoreType.DMA((n,)), pltpu.SemaphoreType.DMA((n,))],
    )(*halves)


def _all_sum_small(vec, name):
    r = vec.shape[0]

    def body(v_ref, o_ref, buf, send, recv):
        x, y, c, _ = _mesh_pos()
        me = 4 * x + 2 * y + c
        buf[me] = v_ref[...]
        cps = []
        for mask in range(1, NDEV):
            fx, fy, fc = (mask >> 2) & 1, (mask >> 1) & 1, mask & 1
            peer = (1 - x if fx else x, 1 - y if fy else y, 1 - c if fc else c)
            cp = _remote(v_ref, buf.at[me], send.at[mask - 1], recv.at[mask - 1], peer)
            cp.start()
            cps.append(cp)
        for cp in cps:
            cp.wait()
        total = buf[0]
        for dev in range(1, NDEV):
            total = total + buf[dev]
        o_ref[...] = total

    vm = pl.BlockSpec(memory_space=pltpu.VMEM)
    return pl.pallas_call(
        body, name=name, in_specs=[vm], out_specs=vm, out_shape=jax.ShapeDtypeStruct((r, LANES), F32),
        scratch_shapes=[pltpu.VMEM((NDEV, r, LANES), F32), pltpu.SemaphoreType.DMA((NDEV - 1,)), pltpu.SemaphoreType.DMA((NDEV - 1,))],
    )(vec)


def _add_pairs(a, b, out_dtype, name):
    n0, n1, r, c = a.shape
    tr = _pick(r, max(8, (1 << 19) // c // 8 * 8), 8)

    def body(a_ref, b_ref, o_ref):
        o_ref[...] = (a_ref[...] + b_ref[...]).astype(out_dtype)

    blk = pl.BlockSpec((1, 1, tr, c), lambda i, j, t: (i, j, t, 0))
    return pl.pallas_call(
        body, name=name, grid=(n0, n1, r // tr), in_specs=[blk, blk], out_specs=blk,
        out_shape=jax.ShapeDtypeStruct(a.shape, out_dtype), compiler_params=_params("parallel", "parallel", "parallel"),
    )(a, b)


def _sum_chips(p, name):
    _, hl, r, c = p.shape
    tr = _pick(r, max(16, (1 << 19) // c // 16 * 16), 16)

    def body(p_ref, o_ref):
        total = p_ref[0].astype(F32)
        for j in range(1, NCHIP):
            total = total + p_ref[j].astype(F32)
        o_ref[...] = total

    return pl.pallas_call(
        body, name=name, grid=(hl, r // tr),
        in_specs=[pl.BlockSpec((NCHIP, 1, tr, c), lambda i, t: (0, i, t, 0))],
        out_specs=pl.BlockSpec((1, tr, c), lambda i, t: (i, t, 0)),
        out_shape=jax.ShapeDtypeStruct((hl, r, c), F32), compiler_params=_params("parallel", "parallel"),
    )(p)


def _reduce_scatter(parts):
    c = lax.axis_index("c")
    k = 2 * lax.axis_index("x") + lax.axis_index("y")
    hl = parts[0].shape[1] // 2
    theirs = _swap_halves(parts)
    mine = [lax.dynamic_slice_in_dim(p, c * hl, hl, axis=1) for p in parts]
    chip_sum = [_add_pairs(m, t, BF16, f"grad_pair_sum{i}") for i, (m, t) in enumerate(zip(mine, theirs))]
    gathered = _exchange_shards(chip_sum)
    gathered = [lax.dynamic_update_slice_in_dim(g, lax.dynamic_slice_in_dim(s, k, 1, axis=0), k, axis=0)
                for g, s in zip(gathered, chip_sum)]
    halves = [_sum_chips(g, f"grad_chip_sum{i}") for i, g in enumerate(gathered)]
    sibling = _swap_reduced(halves)
    south = c == 0
    return [jnp.concatenate([jnp.where(south, h, s), jnp.where(south, s, h)], axis=0) for h, s in zip(halves, sibling)]


SMALL = ("norm_w", "conv_w", "conv_b", "dt_bias", "a_log", "d_skip", "ssd_norm_w", "final_norm_w")


def _pack(arrays):
    flat = jnp.concatenate([a.reshape(-1).astype(F32) for a in arrays])
    rows = -(-flat.shape[0] // (8 * LANES)) * 8
    return jnp.pad(flat, (0, rows * LANES - flat.shape[0])).reshape(rows, LANES)


def _unpack(vec, shapes):
    flat = vec.reshape(-1)
    out, pos = [], 0
    for s in shapes:
        n = math.prod(s)
        out.append(flat[pos:pos + n].reshape(s))
        pos += n
    return out


def kernel(x, norm_w, w_in, conv_w, conv_b, dt_bias, a_log, d_skip, ssd_norm_w, w_out, final_norm_w, loss_target, m_norm_w, m_w_in, m_conv_w, m_conv_b, m_dt_bias, m_a_log, m_d_skip, m_ssd_norm_w, m_w_out, m_final_norm_w, v_norm_w, v_w_in, v_conv_w, v_conv_b, v_dt_bias, v_a_log, v_d_skip, v_ssd_norm_w, v_w_out, v_final_norm_w):
    depth, d, ics = w_in.shape
    cs = conv_w.shape[2]
    xi, yi, ci = lax.axis_index("x"), lax.axis_index("y"), lax.axis_index("c")
    k = 2 * xi + yi

    placed = lax.dynamic_update_slice(jnp.zeros((depth, KCONV, NCHIP, cs), F32), conv_w[:, :, None, :], (0, 0, k, 0))
    placed = jnp.where(ci == 0, placed, 0.0)
    conv_full = _unpack(_all_sum_small(_pack([placed]), "gather_conv_w"), [(depth, KCONV, NCHIP * cs)])[0]

    w_in16, w_out16 = w_in.astype(BF16), w_out.astype(BF16)
    wi_all, wo_all = _gather_weights([w_in16, w_out16])
    wi_all = lax.dynamic_update_slice_in_dim(wi_all, w_in16[None], k, axis=0)
    wo_all = lax.dynamic_update_slice_in_dim(wo_all, w_out16[None], k, axis=0)
    w_in_full = wi_all.transpose(1, 2, 0, 3).reshape(depth, d, NCHIP * ics)
    w_out_full = wo_all.transpose(1, 0, 2, 3).reshape(depth, 2 * d, d)

    small = dict(norm_w=norm_w, conv_b=conv_b, dt_bias=dt_bias, a_log=a_log, d_skip=d_skip, ssd_norm_w=ssd_norm_w, final_norm_w=final_norm_w)
    loss_local, gx, g = _local_step(x[0], loss_target[0], w_in_full, w_out_full, conv_full, small)

    g_in = g["w_in"].reshape(depth, d, NCHIP, ics).transpose(2, 0, 1, 3)
    g_out = g["w_out"].reshape(depth, NCHIP, 2 * d // NCHIP, d).transpose(1, 0, 2, 3)
    grad_w_in, grad_w_out = _reduce_scatter([g_in, g_out])

    names = list(SMALL)
    total = _all_sum_small(_pack([g[n] for n in names] + [loss_local]), "sum_small_grads")
    parts = _unpack(total, [g[n].shape for n in names] + [()])
    grads = dict(zip(names, parts[:-1]))
    loss = parts[-1]
    grads["conv_w"] = lax.dynamic_index_in_dim(grads["conv_w"].reshape(depth, KCONV, NCHIP, cs), k, axis=2, keepdims=False)
    grads["w_in"], grads["w_out"] = grad_w_in, grad_w_out

    w = dict(norm_w=norm_w, w_in=w_in, conv_w=conv_w, conv_b=conv_b, dt_bias=dt_bias, a_log=a_log, d_skip=d_skip,
             ssd_norm_w=ssd_norm_w, w_out=w_out, final_norm_w=final_norm_w)
    m = dict(norm_w=m_norm_w, w_in=m_w_in, conv_w=m_conv_w, conv_b=m_conv_b, dt_bias=m_dt_bias, a_log=m_a_log, d_skip=m_d_skip,
             ssd_norm_w=m_ssd_norm_w, w_out=m_w_out, final_norm_w=m_final_norm_w)
    v = dict(norm_w=v_norm_w, w_in=v_w_in, conv_w=v_conv_w, conv_b=v_conv_b, dt_bias=v_dt_bias, a_log=v_a_log, d_skip=v_d_skip,
             ssd_norm_w=v_ssd_norm_w, w_out=v_w_out, final_norm_w=v_final_norm_w)
    delta, new_m, new_v = {}, {}, {}
    swap = lambda t: jnp.swapaxes(t, 1, 2)
    delta["w_in"], new_m["w_in"], new_v["w_in"] = (
        swap(t) for t in _adamw(swap(w_in), swap(grads["w_in"]), swap(m_w_in), swap(v_w_in), "adamw_w_in"))
    delta["w_out"], new_m["w_out"], new_v["w_out"] = _adamw(w_out, grads["w_out"], m_w_out, v_w_out, "adamw_w_out")
    shapes = [w[n].shape for n in names]
    packed = [_pack([t[n] for n in names])[None] for t in (w, grads, m, v)]
    for res, out in zip(_adamw(*packed, "adamw_small"), (delta, new_m, new_v)):
        out.update(zip(names, _unpack(res[0], shapes)))

    order = ("norm_w", "w_in", "conv_w", "conv_b", "dt_bias", "a_log", "d_skip", "ssd_norm_w", "w_out", "final_norm_w")
    return (loss, gx[None], *[grads[n] for n in order], *[delta[n] for n in order], *[new_m[n] for n in order], *[new_v[n] for n in order])
```
